```python
import jax, jax.numpy as jnp
from jax import lax
import numpy as np

D_MODEL = 1024
BATCH = 2
SEQ = 16384
DEPTH = 2

HGRN_HEADS = 4
HGRN_HEAD_DIM = 128
HGRN_WIDTH = HGRN_HEADS * HGRN_HEAD_DIM
GLA_HEADS = 4
GLA_KEY_DIM = 64
GLA_VAL_DIM = 128
GLA_KEY_WIDTH = GLA_HEADS * GLA_KEY_DIM
GLA_VAL_WIDTH = GLA_HEADS * GLA_VAL_DIM
GLA_GATE_RANK = 16
GLA_GATE_NORMALIZER = 16.0
LIN_CHUNK = 64
AB_SIZES = (HGRN_WIDTH, HGRN_WIDTH, HGRN_WIDTH, HGRN_WIDTH,
            GLA_KEY_WIDTH, GLA_KEY_WIDTH, GLA_VAL_WIDTH, GLA_VAL_WIDTH, GLA_GATE_RANK)
AB_WIDTH = sum(AB_SIZES)
AB_OUT_WIDTH = HGRN_WIDTH + GLA_VAL_WIDTH
MLSTM_HEADS = 4
MLSTM_QK_DIM = 128
MLSTM_V_DIM = 256
MLSTM_QK_WIDTH = MLSTM_HEADS * MLSTM_QK_DIM
MLSTM_V_WIDTH = MLSTM_HEADS * MLSTM_V_DIM
MLSTM_CONV = 4
MLSTM_CHUNK = 128
C_SIZES = (2 * MLSTM_QK_WIDTH, MLSTM_V_WIDTH, MLSTM_V_WIDTH, MLSTM_HEADS, MLSTM_HEADS)
C_WIDTH = sum(C_SIZES)
N_EXPERTS = 256
TOP_K = 8
N_GROUPS = 8
TOPK_GROUPS = 4
EXPERT_FF = 256
SHARED_FF = 256
ROUTED_SCALE = 2.5
MOE_BLOCK = 128
ALPHA = (2 * DEPTH) ** 0.25
BETA = (8 * DEPTH) ** -0.25
LN_EPS = 1e-5
RMS_EPS = 1e-6
N_EVEN = (DEPTH + 1) // 2
N_ODD = DEPTH // 2

kernel_name = "hgrn2_gla_mlstm_deepnorm_moe"


def _split(a, sizes):
    out, o = [], 0
    for s in sizes:
        out.append(a[..., o:o + s])
        o += s
    return out


def _heads(a, n):
    B, T, C = a.shape
    return a.reshape(B, T, n, C // n).transpose(0, 2, 1, 3)


def _layernorm(x, g, b):
    xf = x.astype(jnp.float32)
    mu = jnp.mean(xf, -1, keepdims=True)
    var = jnp.mean(jnp.square(xf - mu), -1, keepdims=True)
    return ((xf - mu) * lax.rsqrt(var + LN_EPS) * g.astype(jnp.float32) + b.astype(jnp.float32)).astype(x.dtype)


def _head_rmsnorm_gate(o, g, gate):
    B, H, T, dv = o.shape
    o = o * lax.rsqrt(jnp.mean(jnp.square(o), -1, keepdims=True) + RMS_EPS)
    o = o.transpose(0, 2, 1, 3).reshape(B, T, H * dv)
    return o * g.astype(jnp.float32) * jax.nn.silu(gate.astype(jnp.float32))


def _chunked_gla(q, k, v, log_a):
    B, H, T, dk = q.shape
    dv = v.shape[-1]
    L = LIN_CHUNK
    nc = T // L

    def to_chunks(a):
        return a.astype(jnp.float32).reshape(B, H, nc, L, a.shape[-1]).transpose(2, 0, 1, 3, 4)

    causal = jnp.tril(jnp.ones((L, L), dtype=bool))[:, :, None]

    def step(S, inp):
        qj, kj, vj, aj = inp
        b = jnp.cumsum(aj, axis=2)
        g = b[:, :, -1, :]
        diff = b[:, :, :, None, :] - b[:, :, None, :, :]
        decay = jnp.exp(jnp.where(causal, diff, -jnp.inf))
        scores = jnp.einsum('bhjsd,bhsd->bhjs', decay * qj[:, :, :, None, :], kj)
        o = (jnp.einsum('bhjs,bhsv->bhjv', scores, vj)
             + jnp.einsum('bhjd,bhdv->bhjv', qj * jnp.exp(b), S))
        S = (jnp.exp(g)[..., None] * S
             + jnp.einsum('bhsd,bhsv->bhdv', kj * jnp.exp(g[:, :, None, :] - b), vj))
        return S, o

    S0 = jnp.zeros((B, H, dk, dv), jnp.float32)
    _, o = lax.scan(step, S0, (to_chunks(q), to_chunks(k), to_chunks(v), to_chunks(log_a)))
    return o.transpose(1, 2, 0, 3, 4).reshape(B, H, T, dv)


def _chunked_mlstm(q, k, v, i_pre, log_f):
    B, H, T, dk = q.shape
    dv = v.shape[-1]
    L = MLSTM_CHUNK
    nc = T // L

    def to_chunks(a):
        return a.astype(jnp.float32).reshape(B, H, nc, L, a.shape[-1]).transpose(2, 0, 1, 3, 4)

    def gate_chunks(a):
        return a.astype(jnp.float32).reshape(B, H, nc, L).transpose(2, 0, 1, 3)

    causal = jnp.tril(jnp.ones((L, L), dtype=bool))

    def step(carry, inp):
        C, n, m = carry
        qj, kj, vj, ij, fj = inp
        b = jnp.cumsum(fj, axis=-1)
        g = b[..., -1]
        D = jnp.where(causal, b[..., :, None] - b[..., None, :] + ij[..., None, :], -jnp.inf)
        inter = b + m[..., None]
        m_j = jnp.maximum(inter, jnp.max(D, axis=-1))
        s = jnp.einsum('bhjd,bhsd->bhjs', qj, kj) * jnp.exp(D - m_j[..., None])
        w_inter = jnp.exp(inter - m_j)
        num = (jnp.einsum('bhjs,bhsv->bhjv', s, vj)
               + w_inter[..., None] * jnp.einsum('bhjd,bhdv->bhjv', qj, C))
        den = jnp.sum(s, -1) + w_inter * jnp.einsum('bhjd,bhd->bhj', qj, n)
        h = num / jnp.maximum(jnp.abs(den), jnp.exp(-m_j))[..., None]
        u = g[..., None] - b + ij
        m_new = jnp.maximum(g + m, jnp.max(u, -1))
        wk = kj * jnp.exp(u - m_new[..., None])[..., None]
        decay = jnp.exp(g + m - m_new)
        C = decay[..., None, None] * C + jnp.einsum('bhsd,bhsv->bhdv', wk, vj)
        n = decay[..., None] * n + jnp.sum(wk, axis=2)
        return (C, n, m_new), h

    init = (jnp.zeros((B, H, dk, dv), jnp.float32), jnp.zeros((B, H, dk), jnp.float32),
            jnp.zeros((B, H), jnp.float32))
    _, h = lax.scan(step, init, (to_chunks(q), to_chunks(k), to_chunks(v), gate_chunks(i_pre), gate_chunks(log_f)))
    return h.transpose(1, 2, 0, 3, 4).reshape(B, H, T, dv)


def _causal_conv(a, w):
    C = a.shape[-1]
    return lax.conv_general_dilated(a, w[:, None, :].astype(a.dtype), window_strides=(1,),
                                    padding=[(w.shape[0] - 1, 0)], dimension_numbers=('NWC', 'WIO', 'NWC'),
                                    feature_group_count=C)


def _hgrn2_gla_mixer(x, w_in, w_gk_up, b_gk, hgrn_g, gla_g, w_out, lb):
    proj = x @ w_in
    hq, hf, hi, hg, gq, gk, gv, gg, glr = _split(proj, AB_SIZES)
    z = hf.astype(jnp.float32)
    log_f = jnp.logaddexp(jnp.log(lb), jnp.log1p(-lb) + jax.nn.log_sigmoid(z))
    k_h = (1.0 - lb) * jax.nn.sigmoid(-z)
    o_h = _chunked_gla(_heads(jax.nn.silu(hq), HGRN_HEADS), _heads(k_h, HGRN_HEADS),
                       _heads(hi, HGRN_HEADS), _heads(log_f, HGRN_HEADS))
    log_a = jax.nn.log_sigmoid((glr @ w_gk_up + b_gk).astype(jnp.float32)) / GLA_GATE_NORMALIZER
    o_g = _chunked_gla(_heads(gq, GLA_HEADS) * (GLA_KEY_DIM ** -0.5), _heads(gk, GLA_HEADS),
                       _heads(gv, GLA_HEADS), _heads(log_a, GLA_HEADS))
    y = jnp.concatenate([_head_rmsnorm_gate(o_h, hgrn_g, hg), _head_rmsnorm_gate(o_g, gla_g, gg)], axis=-1)
    return y.astype(x.dtype) @ w_out


def _mlstm_mixer(x, w_in, conv_w, b_if, w_out):
    proj = x @ w_in
    qk, v, og, ig, fg = _split(proj, C_SIZES)
    qk = jax.nn.silu(_causal_conv(qk, conv_w))
    q, k = _split(qk, (MLSTM_QK_WIDTH, MLSTM_QK_WIDTH))
    i_pre = (ig.astype(jnp.float32) + b_if[:MLSTM_HEADS]).transpose(0, 2, 1)
    log_f = jax.nn.log_sigmoid(fg.astype(jnp.float32) + b_if[MLSTM_HEADS:]).transpose(0, 2, 1)
    h = _chunked_mlstm(_heads(q, MLSTM_HEADS), _heads(k, MLSTM_HEADS) * (MLSTM_QK_DIM ** -0.5),
                       _heads(v, MLSTM_HEADS), i_pre, log_f)
    B, H, T, dv = h.shape
    y = h.transpose(0, 2, 1, 3).reshape(B, T, H * dv) * jax.nn.sigmoid(og.astype(jnp.float32))
    return y.astype(x.dtype) @ w_out


def _moe(x2, w_router, r_bias, w_e_in, w_e_down, w_s_in, w_s_down):
    N, D = x2.shape
    E, G = N_EXPERTS, N_GROUPS
    scores = jax.nn.sigmoid((x2 @ w_router).astype(jnp.float32))
    biased = scores + r_bias.astype(jnp.float32)
    group_score = jnp.sum(lax.top_k(biased.reshape(N, G, E // G), 2)[0], -1)
    _, gidx = lax.top_k(group_score, TOPK_GROUPS)
    gsel = jnp.any(gidx[:, :, None] == jnp.arange(G)[None, None, :], axis=1)
    masked = jnp.where(jnp.repeat(gsel, E // G, axis=1), biased, -jnp.inf)
    _, eidx = lax.top_k(masked, TOP_K)
    w = jnp.take_along_axis(scores, eidx, axis=1)
    w = w / jnp.sum(w, -1, keepdims=True) * ROUTED_SCALE
    NK = N * TOP_K
    e_flat = eidx.reshape(-1)
    tok_flat = jnp.arange(NK, dtype=jnp.int32) // TOP_K
    order = jnp.argsort(e_flat)
    e_sorted, tok_sorted, w_sorted = e_flat[order], tok_flat[order], w.reshape(-1)[order]
    counts = jnp.bincount(e_flat, length=E)
    padded = (counts + MOE_BLOCK - 1) // MOE_BLOCK * MOE_BLOCK
    block_end = jnp.cumsum(padded)
    start_pad = block_end - padded
    start = jnp.cumsum(counts) - counts
    dest = start_pad[e_sorted] + jnp.arange(NK, dtype=jnp.int32) - start[e_sorted]
    n_blocks = (NK + E * (MOE_BLOCK - 1)) // MOE_BLOCK
    P = n_blocks * MOE_BLOCK
    row_tok = jnp.zeros((P,), jnp.int32).at[dest].set(tok_sorted).reshape(n_blocks, MOE_BLOCK)
    row_w = jnp.zeros((P,), jnp.float32).at[dest].set(w_sorted).reshape(n_blocks, MOE_BLOCK)
    block_expert = jnp.minimum(
        jnp.searchsorted(block_end, jnp.arange(n_blocks, dtype=jnp.int32) * MOE_BLOCK, side='right'), E - 1)

    def body(acc, inp):
        e, rows, rw = inp
        xb = x2[rows]
        gate, up = _split(xb @ w_e_in[e], (EXPERT_FF, EXPERT_FF))
        yb = (jax.nn.silu(gate) * up) @ w_e_down[e]
        return acc.at[rows].add(yb.astype(jnp.float32) * rw[:, None]), None

    routed, _ = lax.scan(body, jnp.zeros((N, D), jnp.float32), (block_expert, row_tok, row_w))
    sg, su = _split(x2 @ w_s_in, (SHARED_FF, SHARED_FF))
    shared = (jax.nn.silu(sg) * su) @ w_s_down
    return (routed + shared.astype(jnp.float32)).astype(x2.dtype)


def setup_inputs(seed: int = 0) -> dict:
    key = jax.random.key(seed)
    ks = jax.random.split(key, 24)
    f32 = jnp.float32

    def nrm(k, shape, scale):
        return jax.random.normal(k, shape, f32) * scale

    NE, NO = N_EVEN, N_ODD
    b_if_c = jnp.concatenate([
        nrm(ks[10], (NO, MLSTM_HEADS), 0.1),
        jnp.broadcast_to(jnp.linspace(3.0, 6.0, MLSTM_HEADS, dtype=f32), (NO, MLSTM_HEADS))
        + nrm(ks[11], (NO, MLSTM_HEADS), 0.1)], axis=-1)
    return {
        "x": nrm(ks[0], (BATCH, SEQ, D_MODEL), 1.0),
        "w_in_ab": nrm(ks[1], (NE, D_MODEL, AB_WIDTH), D_MODEL ** -0.5),
        "w_gla_gate_up": nrm(ks[2], (NE, GLA_GATE_RANK, GLA_KEY_WIDTH), GLA_GATE_RANK ** -0.5),
        "b_gla_gate": nrm(ks[3], (NE, GLA_KEY_WIDTH), 0.02),
        "hgrn_norm_g": 1.0 + nrm(ks[4], (NE, HGRN_WIDTH), 0.02),
        "gla_norm_g": 1.0 + nrm(ks[5], (NE, GLA_VAL_WIDTH), 0.02),
        "w_out_ab": nrm(ks[6], (NE, AB_OUT_WIDTH, D_MODEL), AB_OUT_WIDTH ** -0.5 * BETA),
        "hgrn_lb_logits": nrm(ks[7], (DEPTH + 1, HGRN_WIDTH), 0.1),
        "w_in_c": nrm(ks[8], (NO, D_MODEL, C_WIDTH), D_MODEL ** -0.5),
        "conv_c": nrm(ks[9], (NO, MLSTM_CONV, 2 * MLSTM_QK_WIDTH), MLSTM_CONV ** -0.5),
        "b_if_c": b_if_c,
        "w_out_c": nrm(ks[12], (NO, MLSTM_V_WIDTH, D_MODEL), MLSTM_V_WIDTH ** -0.5 * BETA),
        "w_router": nrm(ks[13], (DEPTH, D_MODEL, N_EXPERTS), D_MODEL ** -0.5),
        "router_bias": nrm(ks[14], (DEPTH, N_EXPERTS), 0.01),
        "w_exp_in": nrm(ks[15], (DEPTH, N_EXPERTS, D_MODEL, 2 * EXPERT_FF), D_MODEL ** -0.5),
        "w_exp_down": nrm(ks[16], (DEPTH, N_EXPERTS, EXPERT_FF, D_MODEL), EXPERT_FF ** -0.5 * BETA),
        "w_sh_in": nrm(ks[17], (DEPTH, D_MODEL, 2 * SHARED_FF), D_MODEL ** -0.5),
        "w_sh_down": nrm(ks[18], (DEPTH, SHARED_FF, D_MODEL), SHARED_FF ** -0.5 * BETA),
        "ln_mix_g": 1.0 + nrm(ks[19], (DEPTH, D_MODEL), 0.02),
        "ln_mix_b": nrm(ks[20], (DEPTH, D_MODEL), 0.02),
        "ln_ffn_g": 1.0 + nrm(ks[21], (DEPTH, D_MODEL), 0.02),
        "ln_ffn_b": nrm(ks[22], (DEPTH, D_MODEL), 0.02),
    }


def reference(x, w_in_ab, w_gla_gate_up, b_gla_gate, hgrn_norm_g, gla_norm_g, w_out_ab, hgrn_lb_logits,
              w_in_c, conv_c, b_if_c, w_out_c, w_router, router_bias, w_exp_in, w_exp_down,
              w_sh_in, w_sh_down, ln_mix_g, ln_mix_b, ln_ffn_g, ln_ffn_b):
    B, T, D = x.shape
    lower_bounds = jnp.cumsum(jax.nn.softmax(hgrn_lb_logits.astype(jnp.float32), axis=0), axis=0)
    h = x
    for l in range(DEPTH):
        j = l // 2
        if l % 2 == 0:
            mix = _hgrn2_gla_mixer(h, w_in_ab[j], w_gla_gate_up[j], b_gla_gate[j], hgrn_norm_g[j],
                                   gla_norm_g[j], w_out_ab[j], lower_bounds[l])
        else:
            mix = _mlstm_mixer(h, w_in_c[j], conv_c[j], b_if_c[j].astype(jnp.float32), w_out_c[j])
        h = _layernorm(ALPHA * h + mix, ln_mix_g[l], ln_mix_b[l])
        ffn = _moe(h.reshape(B * T, D), w_router[l], router_bias[l], w_exp_in[l], w_exp_down[l],
                   w_sh_in[l], w_sh_down[l]).reshape(B, T, D)
        h = _layernorm(ALPHA * h + ffn, ln_ffn_g[l], ln_ffn_b[l])
    return h
```

```python
import functools

import jax
import jax.numpy as jnp
from jax import lax
from jax.experimental import pallas as pl
from jax.experimental.pallas import tpu as pltpu

F32 = jnp.float32
BF16 = jnp.bfloat16
I32 = jnp.int32

D_MODEL = 1024
DEPTH = 2
HGRN_HEADS = 4
HGRN_HEAD_DIM = 128
HGRN_WIDTH = HGRN_HEADS * HGRN_HEAD_DIM
GLA_HEADS = 4
GLA_KEY_DIM = 64
GLA_VAL_DIM = 128
GLA_KEY_WIDTH = GLA_HEADS * GLA_KEY_DIM
GLA_VAL_WIDTH = GLA_HEADS * GLA_VAL_DIM
GLA_GATE_RANK = 16
GLA_GATE_NORMALIZER = 16.0
LIN_CHUNK = 64
LIN_SUB = 16
AB_WIDTH = 4 * HGRN_WIDTH + 2 * GLA_KEY_WIDTH + 2 * GLA_VAL_WIDTH + GLA_GATE_RANK
MLSTM_HEADS = 4
MLSTM_QK_DIM = 128
MLSTM_V_DIM = 256
MLSTM_QK_WIDTH = MLSTM_HEADS * MLSTM_QK_DIM
MLSTM_V_WIDTH = MLSTM_HEADS * MLSTM_V_DIM
MLSTM_CONV = 4
MLSTM_CHUNK = 128
C_WIDTH = 2 * MLSTM_QK_WIDTH + 2 * MLSTM_V_WIDTH + 2 * MLSTM_HEADS
N_EXPERTS = 256
TOP_K = 8
N_GROUPS = 8
TOPK_GROUPS = 4
GROUP_SIZE = N_EXPERTS // N_GROUPS
EXPERT_FF = 256
SHARED_FF = 256
ROUTED_SCALE = 2.5
ALPHA = (2 * DEPTH) ** 0.25
LN_EPS = 1e-5
RMS_EPS = 1e-6

LANES = 128
VMEM_LIMIT = 56 * 1024 * 1024

ROW_TILE = 256
MIX_ROWS = 128
ROUTE_TOK = 128
DISPATCH_TOK = 256
COMBINE_TOK = 128
GMM_ROWS = 256


def _cparams(*sem):
    return pltpu.CompilerParams(dimension_semantics=sem, vmem_limit_bytes=VMEM_LIMIT)


def _sigmoid(x):
    return 1.0 / (1.0 + jnp.exp(-x))


def _silu(x):
    return x * _sigmoid(x)


def _log_sigmoid(x):
    return jnp.minimum(x, 0.0) - jnp.log(1.0 + jnp.exp(-jnp.abs(x)))


def _split3(x):
    hi = x.astype(BF16)
    r = x - hi.astype(F32)
    mid = r.astype(BF16)
    lo = (r - mid.astype(F32)).astype(BF16)
    return hi, mid, lo


def _tri_dot_left(tri, x):
    return sum(jnp.dot(tri, p, preferred_element_type=F32) for p in _split3(x))


def _tri_dot_right(x, tri):
    return sum(jnp.dot(p, tri, preferred_element_type=F32) for p in _split3(x))


def _dot_nt(a, b):
    return lax.dot_general(a, b, (((1,), (1,)), ((), ())), preferred_element_type=F32)


def _dot_tn(a, b):
    return lax.dot_general(a, b, (((0,), (0,)), ((), ())), preferred_element_type=F32)


def _layernorm(x, g, b):
    mu = jnp.mean(x, axis=-1, keepdims=True)
    xc = x - mu
    var = jnp.mean(xc * xc, axis=-1, keepdims=True)
    return xc * lax.rsqrt(var + LN_EPS) * g + b


def _linear_kernel(x_ref, w_ref, o_ref):
    o_ref[...] = jnp.dot(x_ref[...].astype(BF16), w_ref[...], preferred_element_type=F32)


def _linear(x, w):
    n, k = x.shape
    m = w.shape[1]
    return pl.pallas_call(
        _linear_kernel,
        grid=(n // ROW_TILE,),
        in_specs=[pl.BlockSpec((ROW_TILE, k), lambda i: (i, 0)),
                  pl.BlockSpec((k, m), lambda i: (0, 0))],
        out_specs=pl.BlockSpec((ROW_TILE, m), lambda i: (i, 0)),
        out_shape=jax.ShapeDtypeStruct((n, m), F32),
        compiler_params=_cparams("parallel"),
        name="linear",
    )(x, w)


def _outproj_ln_kernel(x_ref, y_ref, w_ref, g_ref, b_ref, o_ref):
    mix = jnp.dot(y_ref[...], w_ref[...], preferred_element_type=F32)
    o_ref[...] = _layernorm(ALPHA * x_ref[...] + mix, g_ref[...], b_ref[...])


def _outproj_ln(x, y, w, g, b):
    n, d = x.shape
    k = y.shape[1]
    return pl.pallas_call(
        _outproj_ln_kernel,
        grid=(n // ROW_TILE,),
        in_specs=[pl.BlockSpec((ROW_TILE, d), lambda i: (i, 0)),
                  pl.BlockSpec((ROW_TILE, k), lambda i: (i, 0)),
                  pl.BlockSpec((k, d), lambda i: (0, 0)),
                  pl.BlockSpec((1, d), lambda i: (0, 0)),
                  pl.BlockSpec((1, d), lambda i: (0, 0))],
        out_specs=pl.BlockSpec((ROW_TILE, d), lambda i: (i, 0)),
        out_shape=jax.ShapeDtypeStruct((n, d), F32),
        compiler_params=_cparams("parallel"),
        name="outproj_ln",
    )(x, y, w, g, b)


def _gla_chunk(q, k, v, la, st_ref, tril):
    L, c = LIN_CHUNK, LIN_SUB
    dk = q.shape[1]
    b = _tri_dot_left(tril, la)
    g = b[L - 1:L, :]
    st = st_ref[...]
    o = _dot_nt((q * jnp.exp(b)).astype(BF16), st.astype(BF16))
    k_end = (k * jnp.exp(g - b)).astype(BF16)
    st_ref[...] = st * jnp.exp(g) + _dot_tn(v.astype(BF16), k_end)

    col = lax.broadcasted_iota(I32, (c, L), 1)
    row = lax.broadcasted_iota(I32, (c, L), 0)
    rows_a = []
    for blk in range(L // c):
        lo = blk * c
        qb, kb, bb = q[lo:lo + c], k[lo:lo + c], b[lo:lo + c]
        acc = jnp.zeros((c, L), F32)
        for j in range(c):
            e = jnp.exp(jnp.minimum(bb - bb[j:j + 1], 0.0))
            colsum = jnp.sum(qb * e * kb[j:j + 1], axis=-1, keepdims=True)
            acc = jnp.where(col == lo + j, colsum, acc)
        a_blk = jnp.where(col <= row + lo, acc, 0.0)
        if blk > 0:
            ref = b[lo - 1:lo]
            q_in = (qb * jnp.exp(bb - ref)).astype(BF16)
            k_in = (k * jnp.exp(jnp.minimum(ref - b, 0.0))).astype(BF16)
            a_blk = jnp.where(col < lo, _dot_nt(q_in, k_in), a_blk)
        rows_a.append(a_blk)
    a = jnp.concatenate(rows_a, axis=0)
    return o + jnp.dot(a.astype(BF16), v.astype(BF16), preferred_element_type=F32)


def _rms_gate(o, g, gate):
    o = o * lax.rsqrt(jnp.mean(o * o, axis=-1, keepdims=True) + RMS_EPS)
    return o * g * _silu(gate)


def _hgrn_gla_kernel(hq_ref, hf_ref, hi_ref, hg_ref, gq_ref, gk_ref, gv_ref, gg_ref, glr_ref,
                     lb_ref, wup_ref, bgk_ref, hng_ref, gng_ref, y_ref, *st_refs):
    sth_refs, stg_refs = st_refs[:HGRN_HEADS], st_refs[HGRN_HEADS:]

    @pl.when(pl.program_id(1) == 0)
    def _():
        for r in st_refs:
            r[...] = jnp.zeros_like(r)

    L = LIN_CHUNK
    tril = (lax.broadcasted_iota(I32, (L, L), 0) >= lax.broadcasted_iota(I32, (L, L), 1)).astype(BF16)
    lb = lb_ref[...]
    for ck in range(MIX_ROWS // L):
        rs = slice(ck * L, (ck + 1) * L)
        u = jnp.dot(glr_ref[rs, :].astype(BF16), wup_ref[...], preferred_element_type=F32) + bgk_ref[...]
        la_g = _log_sigmoid(u) * (1.0 / GLA_GATE_NORMALIZER)
        for h in range(HGRN_HEADS):
            cs = slice(h * HGRN_HEAD_DIM, (h + 1) * HGRN_HEAD_DIM)
            z = hf_ref[rs, cs]
            lbh = lb[:, cs]
            la = jnp.log(lbh + (1.0 - lbh) * _sigmoid(z))
            k = (1.0 - lbh) * _sigmoid(-z)
            o = _gla_chunk(_silu(hq_ref[rs, cs]), k, hi_ref[rs, cs], la, sth_refs[h], tril)
            y_ref[rs, cs] = _rms_gate(o, hng_ref[:, cs], hg_ref[rs, cs]).astype(y_ref.dtype)
        for h in range(GLA_HEADS):
            ks = slice(h * GLA_KEY_DIM, (h + 1) * GLA_KEY_DIM)
            vs = slice(h * GLA_VAL_DIM, (h + 1) * GLA_VAL_DIM)
            o = _gla_chunk(gq_ref[rs, ks] * (GLA_KEY_DIM ** -0.5), gk_ref[rs, ks], gv_ref[rs, vs],
                           la_g[:, ks], stg_refs[h], tril)
            ys = slice(HGRN_WIDTH + h * GLA_VAL_DIM, HGRN_WIDTH + (h + 1) * GLA_VAL_DIM)
            y_ref[rs, ys] = _rms_gate(o, gng_ref[:, vs], gg_ref[rs, vs]).astype(y_ref.dtype)


def _hgrn_gla_mixer(proj, batch, seq, lb, w_up, b_gk, hgrn_g, gla_g):
    steps = seq // MIX_ROWS
    R = MIX_ROWS

    def col_spec(width, block_idx):
        return pl.BlockSpec((R, width), lambda b, i, _j=block_idx: (b * steps + i, _j))

    def const_spec(shape):
        return pl.BlockSpec(shape, lambda b, i: (0,) * len(shape))

    W = HGRN_WIDTH
    in_specs = [col_spec(W, 0), col_spec(W, 1), col_spec(W, 2), col_spec(W, 3),
                col_spec(GLA_KEY_WIDTH, 4 * W // GLA_KEY_WIDTH),
                col_spec(GLA_KEY_WIDTH, 4 * W // GLA_KEY_WIDTH + 1),
                col_spec(GLA_VAL_WIDTH, (4 * W + 2 * GLA_KEY_WIDTH) // GLA_VAL_WIDTH),
                col_spec(GLA_VAL_WIDTH, (4 * W + 2 * GLA_KEY_WIDTH) // GLA_VAL_WIDTH + 1),
                col_spec(LANES, (AB_WIDTH - GLA_GATE_RANK) // LANES),
                const_spec((1, W)), const_spec((LANES, GLA_KEY_WIDTH)), const_spec((1, GLA_KEY_WIDTH)),
                const_spec((1, W)), const_spec((1, GLA_VAL_WIDTH))]
    return pl.pallas_call(
        _hgrn_gla_kernel,
        grid=(batch, steps),
        in_specs=in_specs,
        out_specs=pl.BlockSpec((R, W + GLA_VAL_WIDTH), lambda b, i: (b * steps + i, 0)),
        out_shape=jax.ShapeDtypeStruct((batch * seq, W + GLA_VAL_WIDTH), BF16),
        scratch_shapes=([pltpu.VMEM((HGRN_HEAD_DIM, HGRN_HEAD_DIM), F32)] * HGRN_HEADS
                        + [pltpu.VMEM((GLA_VAL_DIM, GLA_KEY_DIM), F32)] * GLA_HEADS),
        compiler_params=_cparams("parallel", "arbitrary"),
        name="hgrn_gla",
    )(*([proj] * 9), lb, w_up, b_gk, hgrn_g, gla_g)


def _mlstm_kernel(qk_ref, v_ref, og_ref, gt_ref, cw_ref, bif_ref, y_ref, ext_ref, *state_refs):
    L = MLSTM_CHUNK
    PAD = 8
    cn_refs, m_refs = state_refs[:MLSTM_HEADS], state_refs[MLSTM_HEADS:]

    @pl.when(pl.program_id(1) == 0)
    def _():
        ext_ref[0:PAD, :] = jnp.zeros((PAD, ext_ref.shape[1]), F32)
        for r in state_refs:
            r[...] = jnp.zeros_like(r)

    ext_ref[PAD:PAD + L, :] = qk_ref[...]
    conv = jnp.zeros((L, ext_ref.shape[1]), F32)
    for w in range(MLSTM_CONV):
        conv = conv + cw_ref[w:w + 1, :] * ext_ref[pl.ds(PAD - (MLSTM_CONV - 1) + w, L), :]
    ext_ref[0:PAD, :] = ext_ref[L:L + PAD, :]
    qk = _silu(conv)

    gt = gt_ref[...] + bif_ref[...]
    gt_t = gt.T
    r_i = lax.broadcasted_iota(I32, (L, L), 0)
    c_i = lax.broadcasted_iota(I32, (L, L), 1)
    causal = c_i <= r_i
    tril = causal.astype(BF16)
    triu = (r_i <= c_i).astype(BF16)
    b_cols = _tri_dot_left(tril, _log_sigmoid(gt))
    b_rows = _tri_dot_right(_log_sigmoid(gt_t), triu)
    ones_col = (lax.broadcasted_iota(I32, (L, LANES), 1) == 0).astype(F32)

    H = MLSTM_HEADS
    for h in range(H):
        q = qk[:, h * MLSTM_QK_DIM:(h + 1) * MLSTM_QK_DIM].astype(BF16)
        kf = qk[:, MLSTM_QK_WIDTH + h * MLSTM_QK_DIM:MLSTM_QK_WIDTH + (h + 1) * MLSTM_QK_DIM] * (MLSTM_QK_DIM ** -0.5)
        v_ext = jnp.concatenate([v_ref[:, h * MLSTM_V_DIM:(h + 1) * MLSTM_V_DIM], ones_col], axis=1).astype(BF16)
        bc, br = b_cols[:, H + h:H + h + 1], b_rows[H + h:H + h + 1, :]
        ic, ir = gt[:, h:h + 1], gt_t[h:h + 1, :]
        m_prev = m_refs[h][0:1, 0:1]
        g = bc[L - 1:L, :]
        dmat = jnp.where(causal, bc - br + ir, -jnp.inf)
        inter = bc + m_prev
        m_j = jnp.maximum(inter, jnp.max(dmat, axis=-1, keepdims=True))
        s = _dot_nt(q, kf.astype(BF16)) * jnp.exp(dmat - m_j)
        w_inter = jnp.exp(inter - m_j)
        cn = cn_refs[h][...]
        nd = (jnp.dot(s.astype(BF16), v_ext, preferred_element_type=F32)
              + w_inter * jnp.dot(q, cn.astype(BF16), preferred_element_type=F32))
        num, den = nd[:, :MLSTM_V_DIM], nd[:, MLSTM_V_DIM:MLSTM_V_DIM + 1]
        hid = num / jnp.maximum(jnp.abs(den), jnp.exp(-m_j))
        u = g - bc + ic
        m_new = jnp.maximum(g + m_prev, jnp.max(u, axis=0, keepdims=True))
        wk = (kf * jnp.exp(u - m_new)).astype(BF16)
        cn_refs[h][...] = jnp.exp(g + m_prev - m_new) * cn + _dot_tn(wk, v_ext)
        m_refs[h][...] = jnp.broadcast_to(m_new, m_refs[h].shape)
        vs = slice(h * MLSTM_V_DIM, (h + 1) * MLSTM_V_DIM)
        y_ref[:, vs] = (hid * _sigmoid(og_ref[:, vs])).astype(y_ref.dtype)


def _mlstm_mixer(proj, batch, seq, conv_w, b_if):
    steps = seq // MLSTM_CHUNK
    R = MLSTM_CHUNK
    QK2 = 2 * MLSTM_QK_WIDTH
    VW = MLSTM_V_WIDTH

    def col_spec(width, block_idx):
        return pl.BlockSpec((R, width), lambda b, i, _j=block_idx: (b * steps + i, _j))

    return pl.pallas_call(
        _mlstm_kernel,
        grid=(batch, steps),
        in_specs=[col_spec(QK2, 0), col_spec(VW, QK2 // VW), col_spec(VW, QK2 // VW + 1),
                  col_spec(LANES, (QK2 + 2 * VW) // LANES),
                  pl.BlockSpec((MLSTM_CONV, QK2), lambda b, i: (0, 0)),
                  pl.BlockSpec((1, LANES), lambda b, i: (0, 0))],
        out_specs=pl.BlockSpec((R, VW), lambda b, i: (b * steps + i, 0)),
        out_shape=jax.ShapeDtypeStruct((batch * seq, VW), BF16),
        scratch_shapes=([pltpu.VMEM((R + 8, QK2), F32)]
                        + [pltpu.VMEM((MLSTM_QK_DIM, MLSTM_V_DIM + LANES), F32)] * MLSTM_HEADS
                        + [pltpu.VMEM((8, LANES), F32)] * MLSTM_HEADS),
        compiler_params=_cparams("parallel", "arbitrary"),
        name="mlstm",
    )(proj, proj, proj, proj, conv_w, b_if)


def _router_kernel(h_ref, whi_ref, wlo_ref, bias_ref, eidx_ref, wgt_ref, rank_ref, cnt_ref, carry_ref):
    TT, E, G, GS = ROUTE_TOK, N_EXPERTS, N_GROUPS, GROUP_SIZE

    @pl.when(pl.program_id(0) == 0)
    def _():
        carry_ref[...] = jnp.zeros_like(carry_ref)

    h = h_ref[...]
    h_hi = h.astype(BF16)
    h_lo = (h - h_hi.astype(F32)).astype(BF16)
    logits = _dot_nt(whi_ref[...], h_hi) + _dot_nt(whi_ref[...], h_lo) + _dot_nt(wlo_ref[...], h_hi)
    scores = _sigmoid(logits)
    biased = scores + bias_ref[:, 0:1]
    neg = -jnp.inf

    io_g = lax.broadcasted_iota(I32, (GS, TT), 0)
    io8 = lax.broadcasted_iota(I32, (G, TT), 0)
    gs = jnp.zeros((G, TT), F32)
    for g in range(G):
        blk = biased[g * GS:(g + 1) * GS, :]
        m1 = jnp.max(blk, axis=0, keepdims=True)
        i1 = jnp.min(jnp.where(blk == m1, io_g, GS), axis=0, keepdims=True)
        m2 = jnp.max(jnp.where(io_g == i1, neg, blk), axis=0, keepdims=True)
        gs = jnp.where(io8 == g, m1 + m2, gs)
    gsel = jnp.zeros((G, TT), F32)
    for _ in range(TOPK_GROUPS):
        m = jnp.max(gs, axis=0, keepdims=True)
        idx = jnp.min(jnp.where(gs == m, io8, G), axis=0, keepdims=True)
        hit = io8 == idx
        gsel = jnp.where(hit, 1.0, gsel)
        gs = jnp.where(hit, neg, gs)
    sel = jnp.concatenate([jnp.broadcast_to(gsel[g:g + 1, :], (GS, TT)) for g in range(G)], axis=0)
    masked = jnp.where(sel > 0.0, biased, neg)

    io_e = lax.broadcasted_iota(I32, (E, TT), 0)
    eidx = jnp.zeros((TOP_K, TT), I32)
    wsel = jnp.zeros((TOP_K, TT), F32)
    chosen = jnp.zeros((E, TT), F32)
    for k in range(TOP_K):
        m = jnp.max(masked, axis=0, keepdims=True)
        idx = jnp.min(jnp.where(masked == m, io_e, E), axis=0, keepdims=True)
        hit = io_e == idx
        sc = jnp.sum(jnp.where(hit, scores, 0.0), axis=0, keepdims=True)
        eidx = jnp.where(io8 == k, idx, eidx)
        wsel = jnp.where(io8 == k, sc, wsel)
        chosen = jnp.where(hit, 1.0, chosen)
        masked = jnp.where(hit, neg, masked)
    wgt_ref[...] = wsel / jnp.sum(wsel, axis=0, keepdims=True) * ROUTED_SCALE
    eidx_ref[...] = eidx

    triu = (lax.broadcasted_iota(I32, (TT, TT), 0) <= lax.broadcasted_iota(I32, (TT, TT), 1)).astype(BF16)
    cum = jnp.dot(chosen.astype(BF16), triu, preferred_element_type=F32)
    carry = carry_ref[:, 0:1]
    before = cum - chosen + carry
    rank = jnp.zeros((TOP_K, TT), F32)
    for k in range(TOP_K):
        r = jnp.sum(jnp.where(io_e == eidx[k:k + 1, :], before, 0.0), axis=0, keepdims=True)
        rank = jnp.where(io8 == k, r, rank)
    rank_ref[...] = rank.astype(I32)
    total = carry + cum[:, TT - 1:TT]
    carry_ref[...] = jnp.broadcast_to(total, carry_ref.shape)
    cnt_ref[...] = jnp.broadcast_to(total, cnt_ref.shape)


def _router(h2, wt_hi, wt_lo, bias_col):
    n, d = h2.shape
    TT, E = ROUTE_TOK, N_EXPERTS
    tok_spec = pl.BlockSpec((TOP_K, TT), lambda i: (0, i))
    return pl.pallas_call(
        _router_kernel,
        grid=(n // TT,),
        in_specs=[pl.BlockSpec((TT, d), lambda i: (i, 0)),
                  pl.BlockSpec((E, d), lambda i: (0, 0)),
                  pl.BlockSpec((E, d), lambda i: (0, 0)),
                  pl.BlockSpec((E, LANES), lambda i: (0, 0))],
        out_specs=[tok_spec, tok_spec, tok_spec, pl.BlockSpec((E, LANES), lambda i: (0, 0))],
        out_shape=[jax.ShapeDtypeStruct((TOP_K, n), I32), jax.ShapeDtypeStruct((TOP_K, n), F32),
                   jax.ShapeDtypeStruct((TOP_K, n), I32), jax.ShapeDtypeStruct((E, LANES), F32)],
        scratch_shapes=[pltpu.VMEM((E, LANES), F32)],
        compiler_params=_cparams("arbitrary"),
        name="router",
    )(h2, wt_hi, wt_lo, bias_col)


def _dest_kernel(eidx_ref, rank_ref, offs_ref, dest_ref):
    TT, E = eidx_ref.shape[1], N_EXPERTS
    io_e = lax.broadcasted_iota(I32, (E, TT), 0)
    io8 = lax.broadcasted_iota(I32, (TOP_K, TT), 0)
    offs = offs_ref[:, 0:1]
    base = jnp.zeros((TOP_K, TT), F32)
    for k in range(TOP_K):
        r = jnp.sum(jnp.where(io_e == eidx_ref[k:k + 1, :], offs, 0.0), axis=0, keepdims=True)
        base = jnp.where(io8 == k, r, base)
    dest_ref[...] = base.astype(I32) + rank_ref[...]


def _dest_rows(eidx, rank, offs_col):
    n = eidx.shape[1]
    TT = 512
    spec = pl.BlockSpec((TOP_K, TT), lambda i: (0, i))
    return pl.pallas_call(
        _dest_kernel,
        grid=(n // TT,),
        in_specs=[spec, spec, pl.BlockSpec((N_EXPERTS, LANES), lambda i: (0, 0))],
        out_specs=spec,
        out_shape=jax.ShapeDtypeStruct((TOP_K, n), I32),
        compiler_params=_cparams("parallel"),
        name="dest_rows",
    )(eidx, rank, offs_col)


def _dispatch_kernel(dest_ref, x_ref, out_ref, sem):
    TT = x_ref.shape[0]

    def row_copy(t, k):
        return pltpu.make_async_copy(x_ref.at[pl.ds(t, 1)], out_ref.at[pl.ds(dest_ref[k, t], 1)], sem)

    def issue(t, carry):
        for k in range(TOP_K):
            row_copy(t, k).start()
        return carry

    lax.fori_loop(0, TT, issue, 0)
    for _ in range(TOP_K):
        pltpu.make_async_copy(x_ref, out_ref.at[pl.ds(0, TT)], sem).wait()


def _dispatch(dest, x2):
    n, d = x2.shape
    TT = DISPATCH_TOK
    return pl.pallas_call(
        _dispatch_kernel,
        grid=(n // TT,),
        in_specs=[pl.BlockSpec((TOP_K, TT), lambda i: (0, i), memory_space=pltpu.SMEM),
                  pl.BlockSpec((TT, d), lambda i: (i, 0))],
        out_specs=pl.BlockSpec(memory_space=pl.ANY),
        out_shape=jax.ShapeDtypeStruct((n * TOP_K, d), F32),
        scratch_shapes=[pltpu.SemaphoreType.DMA],
        compiler_params=_cparams("arbitrary"),
        name="dispatch",
    )(dest, x2)


def _gmm_kernel(tile_ref, grp_ref, lo_ref, hi_ref, first_ref, x_ref, win_ref, wdn_ref, o_ref):
    v = pl.program_id(0)
    lo, hi = lo_ref[v], hi_ref[v]

    @pl.when(hi > lo)
    def _():
        gu = jnp.dot(x_ref[...].astype(BF16), win_ref[...].astype(BF16), preferred_element_type=F32)
        act = (_silu(gu[:, :EXPERT_FF]) * gu[:, EXPERT_FF:]).astype(BF16)
        y = jnp.dot(act, wdn_ref[...].astype(BF16), preferred_element_type=F32)
        rows = lax.broadcasted_iota(I32, (GMM_ROWS, 1), 0)
        mine = (rows >= lo) & (rows < hi)

        @pl.when(first_ref[v] == 1)
        def _():
            o_ref[...] = jnp.where(mine, y, 0.0)

        @pl.when(first_ref[v] == 0)
        def _():
            o_ref[...] = jnp.where(mine, y, o_ref[...])


def _gmm(meta, xs, w_in, w_dn):
    p, d = xs.shape
    visits = meta[0].shape[0]
    TM = GMM_ROWS
    grid_spec = pltpu.PrefetchScalarGridSpec(
        num_scalar_prefetch=5,
        grid=(visits,),
        in_specs=[pl.BlockSpec((TM, d), lambda v, tile, grp, lo, hi, first: (tile[v], 0)),
                  pl.BlockSpec((None, d, 2 * EXPERT_FF), lambda v, tile, grp, lo, hi, first: (grp[v], 0, 0)),
                  pl.BlockSpec((None, EXPERT_FF, d), lambda v, tile, grp, lo, hi, first: (grp[v], 0, 0))],
        out_specs=pl.BlockSpec((TM, d), lambda v, tile, grp, lo, hi, first: (tile[v], 0)),
    )
    return pl.pallas_call(
        _gmm_kernel,
        grid_spec=grid_spec,
        out_shape=jax.ShapeDtypeStruct((p, d), F32),
        compiler_params=_cparams("arbitrary"),
        name="gmm",
    )(*meta, xs, w_in, w_dn)


def _gmm_schedule(counts, n_rows):
    TM = GMM_ROWS
    E = counts.shape[0]
    max_visits = n_rows // TM + E - 1
    ends = jnp.cumsum(counts)
    starts = ends - counts
    first_tile = starts // TM
    n_vis = jnp.where(counts > 0, jnp.maximum(ends - 1, 0) // TM - first_tile + 1, 0)
    vis_end = jnp.cumsum(n_vis)
    vis_start = vis_end - n_vis
    v = jnp.arange(max_visits, dtype=I32)
    n_real = vis_end[-1]
    grp = jnp.minimum(jnp.searchsorted(vis_end, v, side="right").astype(I32), E - 1)
    real = v < n_real
    tile = first_tile[grp] + (v - vis_start[grp])
    lo = jnp.maximum(starts[grp], tile * TM) - tile * TM
    hi = jnp.minimum(ends[grp], (tile + 1) * TM) - tile * TM
    last_real = jnp.maximum(n_real - 1, 0)
    last_grp = jnp.minimum(jnp.searchsorted(vis_end, last_real, side="right").astype(I32), E - 1)
    last_tile = first_tile[last_grp] + (last_real - vis_start[last_grp])
    tile = jnp.where(real, tile, last_tile).astype(I32)
    grp = jnp.where(real, grp, last_grp).astype(I32)
    lo = jnp.where(real, lo, 0).astype(I32)
    hi = jnp.where(real, hi, 0).astype(I32)
    prev_tile = jnp.concatenate([jnp.full((1,), -1, I32), tile[:-1]])
    first = (real & (tile != prev_tile)).astype(I32)
    return tile, grp, lo, hi, first


def _combine_kernel(dest_ref, wgt_ref, h_ref, ys_ref, wsi_ref, wsd_ref, g_ref, b_ref, o_ref, buf_ref, sem):
    TT = h_ref.shape[0]

    def row_copy(t, k):
        return pltpu.make_async_copy(ys_ref.at[pl.ds(dest_ref[k, t], 1)], buf_ref.at[k, pl.ds(t, 1)], sem)

    def issue(t, carry):
        for k in range(TOP_K):
            row_copy(t, k).start()
        return carry

    lax.fori_loop(0, TT, issue, 0)

    h = h_ref[...]
    gu = jnp.dot(h.astype(BF16), wsi_ref[...], preferred_element_type=F32)
    act = (_silu(gu[:, :SHARED_FF]) * gu[:, SHARED_FF:]).astype(BF16)
    acc = ALPHA * h + jnp.dot(act, wsd_ref[...], preferred_element_type=F32)

    for k in range(TOP_K):
        pltpu.make_async_copy(ys_ref.at[pl.ds(0, TT)], buf_ref.at[k], sem).wait()
    wgt = wgt_ref[...]
    for k in range(TOP_K):
        acc = acc + wgt[:, k:k + 1] * buf_ref[k]
    o_ref[...] = _layernorm(acc, g_ref[...], b_ref[...])


def _combine(dest, wgt_tk, h2, ys, w_si, w_sd, g, b):
    n, d = h2.shape
    TT = COMBINE_TOK
    const2 = lambda shape: pl.BlockSpec(shape, lambda i: (0, 0))
    return pl.pallas_call(
        _combine_kernel,
        grid=(n // TT,),
        in_specs=[pl.BlockSpec((TOP_K, TT), lambda i: (0, i), memory_space=pltpu.SMEM),
                  pl.BlockSpec((TT, TOP_K), lambda i: (i, 0)),
                  pl.BlockSpec((TT, d), lambda i: (i, 0)),
                  pl.BlockSpec(memory_space=pl.ANY),
                  const2((d, 2 * SHARED_FF)), const2((SHARED_FF, d)), const2((1, d)), const2((1, d))],
        out_specs=pl.BlockSpec((TT, d), lambda i: (i, 0)),
        out_shape=jax.ShapeDtypeStruct((n, d), F32),
        scratch_shapes=[pltpu.VMEM((TOP_K, TT, d), F32), pltpu.SemaphoreType.DMA],
        compiler_params=_cparams("arbitrary"),
        name="combine",
    )(dest, wgt_tk, h2, ys, w_si, w_sd, g, b)


def _moe_ln(h2, w_router, r_bias, w_e_in, w_e_dn, w_s_in, w_s_dn, ln_g, ln_b):
    n, d = h2.shape
    wt = w_router.T
    wt_hi = wt.astype(BF16)
    wt_lo = (wt - wt_hi.astype(F32)).astype(BF16)
    bias_col = jnp.broadcast_to(r_bias.astype(F32)[:, None], (N_EXPERTS, LANES))
    eidx, wgt, rank, cnt = _router(h2, wt_hi, wt_lo, bias_col)
    counts = cnt[:, 0].astype(I32)
    offs = jnp.cumsum(counts) - counts
    offs_col = jnp.broadcast_to(offs.astype(F32)[:, None], (N_EXPERTS, LANES))
    dest = _dest_rows(eidx, rank, offs_col)
    xs = _dispatch(dest, h2)
    ys = _gmm(_gmm_schedule(counts, n * TOP_K), xs, w_e_in, w_e_dn)
    return _combine(dest, wgt.T, h2, ys, w_s_in.astype(BF16), w_s_dn.astype(BF16),
                    ln_g[None, :], ln_b[None, :])


def _pad_cols(w, width):
    return jnp.pad(w, ((0, 0), (0, width - w.shape[1])))


def kernel(x, w_in_ab, w_gla_gate_up, b_gla_gate, hgrn_norm_g, gla_norm_g, w_out_ab, hgrn_lb_logits, w_in_c, conv_c, b_if_c, w_out_c, w_router, router_bias, w_exp_in, w_exp_down, w_sh_in, w_sh_down, ln_mix_g, ln_mix_b, ln_ffn_g, ln_ffn_b):
    B, T, D = x.shape
    lower_bounds = jnp.cumsum(jax.nn.softmax(hgrn_lb_logits.astype(F32), axis=0), axis=0)
    h = x.reshape(B * T, D)
    for l in range(DEPTH):
        j = l // 2
        if l % 2 == 0:
            ab_pad = AB_WIDTH - GLA_GATE_RANK + LANES
            proj = _linear(h, _pad_cols(w_in_ab[j], ab_pad).astype(BF16))
            w_up = jnp.pad(w_gla_gate_up[j], ((0, LANES - GLA_GATE_RANK), (0, 0))).astype(BF16)
            y = _hgrn_gla_mixer(proj, B, T, lower_bounds[l][None, :], w_up, b_gla_gate[j][None, :],
                                hgrn_norm_g[j][None, :], gla_norm_g[j][None, :])
            w_out = w_out_ab[j]
        else:
            c_pad = C_WIDTH - 2 * MLSTM_HEADS + LANES
            proj = _linear(h, _pad_cols(w_in_c[j], c_pad).astype(BF16))
            b_if = jnp.pad(b_if_c[j].astype(F32), (0, LANES - 2 * MLSTM_HEADS))[None, :]
            y = _mlstm_mixer(proj, B, T, conv_c[j], b_if)
            w_out = w_out_c[j]
        h = _outproj_ln(h, y, w_out.astype(BF16), ln_mix_g[l][None, :], ln_mix_b[l][None, :])
        h = _moe_ln(h, w_router[l], router_bias[l], w_exp_in[l], w_exp_down[l], w_sh_in[l], w_sh_down[l],
                    ln_ffn_g[l], ln_ffn_b[l])
    return h.reshape(B, T, D)
```

```python
import functools

import jax
import jax.numpy as jnp
from jax import lax
from jax.experimental import pallas as pl
from jax.experimental.pallas import tpu as pltpu

F32 = jnp.float32
BF16 = jnp.bfloat16
I32 = jnp.int32

D_MODEL = 1024
DEPTH = 2
HGRN_HEADS = 4
HGRN_HEAD_DIM = 128
HGRN_WIDTH = HGRN_HEADS * HGRN_HEAD_DIM
GLA_HEADS = 4
GLA_KEY_DIM = 64
GLA_VAL_DIM = 128
GLA_KEY_WIDTH = GLA_HEADS * GLA_KEY_DIM
GLA_VAL_WIDTH = GLA_HEADS * GLA_VAL_DIM
GLA_GATE_RANK = 16
GLA_GATE_NORMALIZER = 16.0
LIN_CHUNK = 64
LIN_SUB = 16
AB_WIDTH = 4 * HGRN_WIDTH + 2 * GLA_KEY_WIDTH + 2 * GLA_VAL_WIDTH + GLA_GATE_RANK
MLSTM_HEADS = 4
MLSTM_QK_DIM = 128
MLSTM_V_DIM = 256
MLSTM_QK_WIDTH = MLSTM_HEADS * MLSTM_QK_DIM
MLSTM_V_WIDTH = MLSTM_HEADS * MLSTM_V_DIM
MLSTM_CONV = 4
MLSTM_CHUNK = 128
C_WIDTH = 2 * MLSTM_QK_WIDTH + 2 * MLSTM_V_WIDTH + 2 * MLSTM_HEADS
N_EXPERTS = 256
TOP_K = 8
N_GROUPS = 8
TOPK_GROUPS = 4
GROUP_SIZE = N_EXPERTS // N_GROUPS
EXPERT_FF = 256
SHARED_FF = 256
ROUTED_SCALE = 2.5
ALPHA = (2 * DEPTH) ** 0.25
LN_EPS = 1e-5
RMS_EPS = 1e-6

LANES = 128
VMEM_LIMIT = 56 * 1024 * 1024

ROW_TILE = 256
MIX_ROWS = 128
ROUTE_TOK = 128
DISPATCH_TOK = 256
COMBINE_TOK = 128
GMM_ROWS = 256


def _cparams(*sem):
    return pltpu.CompilerParams(dimension_semantics=sem, vmem_limit_bytes=VMEM_LIMIT)


def _sigmoid(x):
    return 1.0 / (1.0 + jnp.exp(-x))


def _silu(x):
    return x * _sigmoid(x)


def _log_sigmoid(x):
    return jnp.minimum(x, 0.0) - jnp.log(1.0 + jnp.exp(-jnp.abs(x)))


def _split3(x):
    hi = x.astype(BF16)
    r = x - hi.astype(F32)
    mid = r.astype(BF16)
    lo = (r - mid.astype(F32)).astype(BF16)
    return hi, mid, lo


def _tri_dot_left(tri, x):
    return sum(jnp.dot(tri, p, preferred_element_type=F32) for p in _split3(x))


def _tri_dot_right(x, tri):
    return sum(jnp.dot(p, tri, preferred_element_type=F32) for p in _split3(x))


def _dot_nt(a, b):
    return lax.dot_general(a, b, (((1,), (1,)), ((), ())), preferred_element_type=F32)


def _dot_tn(a, b):
    return lax.dot_general(a, b, (((0,), (0,)), ((), ())), preferred_element_type=F32)


def _layernorm(x, g, b):
    mu = jnp.mean(x, axis=-1, keepdims=True)
    xc = x - mu
    var = jnp.mean(xc * xc, axis=-1, keepdims=True)
    return xc * lax.rsqrt(var + LN_EPS) * g + b


def _linear_kernel(x_ref, w_ref, o_ref):
    o_ref[...] = jnp.dot(x_ref[...].astype(BF16), w_ref[...], preferred_element_type=F32)


def _linear(x, w):
    n, k = x.shape
    m = w.shape[1]
    return pl.pallas_call(
        _linear_kernel,
        grid=(n // ROW_TILE,),
        in_specs=[pl.BlockSpec((ROW_TILE, k), lambda i: (i, 0)),
                  pl.BlockSpec((k, m), lambda i: (0, 0))],
        out_specs=pl.BlockSpec((ROW_TILE, m), lambda i: (i, 0)),
        out_shape=jax.ShapeDtypeStruct((n, m), F32),
        compiler_params=_cparams("parallel"),
        name="linear",
    )(x, w)


def _outproj_ln_kernel(x_ref, y_ref, w_ref, g_ref, b_ref, o_ref):
    mix = jnp.dot(y_ref[...], w_ref[...], preferred_element_type=F32)
    o_ref[...] = _layernorm(ALPHA * x_ref[...] + mix, g_ref[...], b_ref[...])


def _outproj_ln(x, y, w, g, b):
    n, d = x.shape
    k = y.shape[1]
    return pl.pallas_call(
        _outproj_ln_kernel,
        grid=(n // ROW_TILE,),
        in_specs=[pl.BlockSpec((ROW_TILE, d), lambda i: (i, 0)),
                  pl.BlockSpec((ROW_TILE, k), lambda i: (i, 0)),
                  pl.BlockSpec((k, d), lambda i: (0, 0)),
                  pl.BlockSpec((1, d), lambda i: (0, 0)),
                  pl.BlockSpec((1, d), lambda i: (0, 0))],
        out_specs=pl.BlockSpec((ROW_TILE, d), lambda i: (i, 0)),
        out_shape=jax.ShapeDtypeStruct((n, d), F32),
        compiler_params=_cparams("parallel"),
        name="outproj_ln",
    )(x, y, w, g, b)


N_UNITS = HGRN_HEADS + GLA_HEADS // 2
HALF_LANES = LANES // 2
LOG2E = 1.4426950408889634


def _pair_selector():
    assert LIN_CHUNK == HALF_LANES == GLA_KEY_DIM
    r = jnp.arange(LIN_SUB * LANES)[:, None]
    c = jnp.arange(LANES)[None, :]
    same_slot = (r // LANES) == (c % LIN_SUB)
    same_half = ((r % LANES) < HALF_LANES) == (c < HALF_LANES)
    return (same_slot & same_half).astype(BF16)


def _pairwise_tiles(q, k, b, lhs_ref, row0):
    c = LIN_SUB
    b2 = b * LOG2E
    for blk in range(LIN_CHUNK // c):
        lo = blk * c
        qb, kb, bb = q[lo:lo + c], k[lo:lo + c], b2[lo:lo + c]
        for j in range(c):
            e = jnp.exp2(jnp.minimum(bb - bb[j:j + 1], 0.0))
            lhs_ref[row0 + lo:row0 + lo + c, j * LANES:(j + 1) * LANES] = (qb * kb[j:j + 1] * e).astype(BF16)


def _unit_chunk(q, k, b, vs, r, st_ref, masks):
    L, c = LIN_CHUNK, LIN_SUB
    lane_lo, col_mod, same_blk_causal = masks
    heads = [lane_lo, ~lane_lo] if len(vs) == 2 else [None]

    def pick(x, m):
        return x if m is None else jnp.where(m, x, jnp.zeros_like(x))

    g = b[L - 1:L, :]
    st = st_ref[...]
    st_b = st.astype(BF16)
    qx = (q * jnp.exp(b)).astype(BF16)
    outs = [_dot_nt(pick(qx, m), st_b) for m in heads]
    k_end = (k * jnp.exp(g - b)).astype(BF16)
    upd = [_dot_tn(v.astype(BF16), k_end) for v in vs]
    st_ref[...] = st * jnp.exp(g) + (upd[0] if len(vs) == 1 else jnp.where(lane_lo, upd[0], upd[1]))

    off_rows = [jnp.zeros((c, LANES), F32)]
    for blk in range(1, L // c):
        lo = blk * c
        ref = b[lo - 1:lo]
        q_in = (q[lo:lo + c] * jnp.exp(b[lo:lo + c] - ref)).astype(BF16)
        k_in = (k * jnp.exp(jnp.minimum(ref - b, 0.0))).astype(BF16)
        if len(vs) == 2:
            k_cat = jnp.concatenate([pick(k_in, heads[0]), pick(k_in, heads[1])], axis=0)
        else:
            k_cat = jnp.concatenate([k_in, jnp.zeros_like(k_in)], axis=0)
        off_rows.append(jnp.where(col_mod < lo, _dot_nt(q_in, k_cat), 0.0))
    a = jnp.where(same_blk_causal, r, jnp.concatenate(off_rows, axis=0)).astype(BF16)
    v_cat = jnp.concatenate([vs[0], vs[-1]], axis=0).astype(BF16)
    return [o + jnp.dot(pick(a, m), v_cat, preferred_element_type=F32) for o, m in zip(outs, heads)]


def _rms_gate(o, g, gate):
    o = o * lax.rsqrt(jnp.mean(o * o, axis=-1, keepdims=True) + RMS_EPS)
    return o * g * _silu(gate)


def _hgrn_gla_kernel(hq_ref, hf_ref, hi_ref, hg_ref, gq_ref, gk_ref, gv_ref, gg_ref, glr_ref,
                     lb_ref, wup_ref, bgk_ref, hng_ref, gng_ref, sel_ref, y_ref, lhs_ref, *st_refs):
    @pl.when(pl.program_id(1) == 0)
    def _():
        for r in st_refs:
            r[...] = jnp.zeros_like(r)

    L = LIN_CHUNK
    n_chunks = MIX_ROWS // L
    r_i = lax.broadcasted_iota(I32, (L, LANES), 0)
    c_i = lax.broadcasted_iota(I32, (L, LANES), 1)
    col_mod = jnp.where(c_i < HALF_LANES, c_i, c_i - HALF_LANES)
    same_blk_causal = (col_mod // LIN_SUB == r_i // LIN_SUB) & (col_mod <= r_i)
    c_sub = lax.broadcasted_iota(I32, (LIN_SUB, LANES), 1)
    col_mod_sub = jnp.where(c_sub < HALF_LANES, c_sub, c_sub - HALF_LANES)
    lane_lo = lax.broadcasted_iota(I32, (1, LANES), 1) < HALF_LANES
    masks = (lane_lo, col_mod_sub, same_blk_causal)
    tril = (lax.broadcasted_iota(I32, (L, L), 0) >= lax.broadcasted_iota(I32, (L, L), 1)).astype(BF16)
    lb = lb_ref[...]

    units = []
    for ck in range(n_chunks):
        rs = slice(ck * L, (ck + 1) * L)
        u = jnp.dot(glr_ref[rs, :].astype(BF16), wup_ref[...], preferred_element_type=F32) + bgk_ref[...]
        la_g = _log_sigmoid(u) * (1.0 / GLA_GATE_NORMALIZER)
        qs, ks, las = [], [], []
        for h in range(HGRN_HEADS):
            cs = slice(h * LANES, (h + 1) * LANES)
            z = hf_ref[rs, cs]
            lbh = lb[:, cs]
            qs.append(_silu(hq_ref[rs, cs]))
            ks.append((1.0 - lbh) * _sigmoid(-z))
            las.append(jnp.log(lbh + (1.0 - lbh) * _sigmoid(z)))
        for p in range(GLA_HEADS // 2):
            cs = slice(p * LANES, (p + 1) * LANES)
            qs.append(gq_ref[rs, cs] * (GLA_KEY_DIM ** -0.5))
            ks.append(gk_ref[rs, cs])
            las.append(la_g[:, cs])
        b_all = _tri_dot_left(tril, jnp.concatenate(las, axis=1))
        for un in range(N_UNITS):
            b = b_all[:, un * LANES:(un + 1) * LANES]
            _pairwise_tiles(qs[un], ks[un], b, lhs_ref, (ck * N_UNITS + un) * L)
            units.append((qs[un], ks[un], b))

    r_all = jnp.dot(lhs_ref[...], sel_ref[...], preferred_element_type=F32)

    for ck in range(n_chunks):
        rs = slice(ck * L, (ck + 1) * L)
        for un in range(N_UNITS):
            uc = ck * N_UNITS + un
            q, k, b = units[uc]
            r = r_all[uc * L:(uc + 1) * L]
            if un < HGRN_HEADS:
                cs = slice(un * LANES, (un + 1) * LANES)
                (o,) = _unit_chunk(q, k, b, [hi_ref[rs, cs]], r, st_refs[un], masks)
                y_ref[rs, cs] = _rms_gate(o, hng_ref[:, cs], hg_ref[rs, cs]).astype(y_ref.dtype)
            else:
                p = un - HGRN_HEADS
                vcs = [slice((2 * p + i) * GLA_VAL_DIM, (2 * p + i + 1) * GLA_VAL_DIM) for i in range(2)]
                outs = _unit_chunk(q, k, b, [gv_ref[rs, vc] for vc in vcs], r, st_refs[un], masks)
                for o, vc in zip(outs, vcs):
                    ys = slice(HGRN_WIDTH + vc.start, HGRN_WIDTH + vc.stop)
                    y_ref[rs, ys] = _rms_gate(o, gng_ref[:, vc], gg_ref[rs, vc]).astype(y_ref.dtype)


def _hgrn_gla_mixer(proj, batch, seq, lb, w_up, b_gk, hgrn_g, gla_g):
    steps = seq // MIX_ROWS
    R = MIX_ROWS

    def col_spec(width, block_idx):
        return pl.BlockSpec((R, width), lambda b, i, _j=block_idx: (b * steps + i, _j))

    def const_spec(shape):
        return pl.BlockSpec(shape, lambda b, i: (0,) * len(shape))

    W = HGRN_WIDTH
    in_specs = [col_spec(W, 0), col_spec(W, 1), col_spec(W, 2), col_spec(W, 3),
                col_spec(GLA_KEY_WIDTH, 4 * W // GLA_KEY_WIDTH),
                col_spec(GLA_KEY_WIDTH, 4 * W // GLA_KEY_WIDTH + 1),
                col_spec(GLA_VAL_WIDTH, (4 * W + 2 * GLA_KEY_WIDTH) // GLA_VAL_WIDTH),
                col_spec(GLA_VAL_WIDTH, (4 * W + 2 * GLA_KEY_WIDTH) // GLA_VAL_WIDTH + 1),
                col_spec(LANES, (AB_WIDTH - GLA_GATE_RANK) // LANES),
                const_spec((1, W)), const_spec((LANES, GLA_KEY_WIDTH)), const_spec((1, GLA_KEY_WIDTH)),
                const_spec((1, W)), const_spec((1, GLA_VAL_WIDTH)),
                const_spec((LIN_SUB * LANES, LANES))]
    n_uc = (R // LIN_CHUNK) * N_UNITS
    return pl.pallas_call(
        _hgrn_gla_kernel,
        grid=(batch, steps),
        in_specs=in_specs,
        out_specs=pl.BlockSpec((R, W + GLA_VAL_WIDTH), lambda b, i: (b * steps + i, 0)),
        out_shape=jax.ShapeDtypeStruct((batch * seq, W + GLA_VAL_WIDTH), BF16),
        scratch_shapes=([pltpu.VMEM((n_uc * LIN_CHUNK, LIN_SUB * LANES), BF16)]
                        + [pltpu.VMEM((GLA_VAL_DIM, LANES), F32)] * N_UNITS),
        compiler_params=_cparams("parallel", "arbitrary"),
        name="hgrn_gla",
    )(*([proj] * 9), lb, w_up, b_gk, hgrn_g, gla_g, _pair_selector())


def _mlstm_kernel(qk_ref, v_ref, og_ref, gt_ref, cw_ref, bif_ref, y_ref, ext_ref, *state_refs):
    L = MLSTM_CHUNK
    PAD = 8
    cn_refs, m_refs = state_refs[:MLSTM_HEADS], state_refs[MLSTM_HEADS:]

    @pl.when(pl.program_id(1) == 0)
    def _():
        ext_ref[0:PAD, :] = jnp.zeros((PAD, ext_ref.shape[1]), F32)
        for r in state_refs:
            r[...] = jnp.zeros_like(r)

    ext_ref[PAD:PAD + L, :] = qk_ref[...]
    conv = jnp.zeros((L, ext_ref.shape[1]), F32)
    for w in range(MLSTM_CONV):
        conv = conv + cw_ref[w:w + 1, :] * ext_ref[pl.ds(PAD - (MLSTM_CONV - 1) + w, L), :]
    ext_ref[0:PAD, :] = ext_ref[L:L + PAD, :]
    qk = _silu(conv)

    gt = gt_ref[...] + bif_ref[...]
    gt_t = gt.T
    r_i = lax.broadcasted_iota(I32, (L, L), 0)
    c_i = lax.broadcasted_iota(I32, (L, L), 1)
    causal = c_i <= r_i
    tril = causal.astype(BF16)
    triu = (r_i <= c_i).astype(BF16)
    b_cols = _tri_dot_left(tril, _log_sigmoid(gt))
    b_rows = _tri_dot_right(_log_sigmoid(gt_t), triu)
    ones_col = (lax.broadcasted_iota(I32, (L, LANES), 1) == 0).astype(F32)

    H = MLSTM_HEADS
    for h in range(H):
        q = qk[:, h * MLSTM_QK_DIM:(h + 1) * MLSTM_QK_DIM].astype(BF16)
        kf = qk[:, MLSTM_QK_WIDTH + h * MLSTM_QK_DIM:MLSTM_QK_WIDTH + (h + 1) * MLSTM_QK_DIM] * (MLSTM_QK_DIM ** -0.5)
        v_ext = jnp.concatenate([v_ref[:, h * MLSTM_V_DIM:(h + 1) * MLSTM_V_DIM], ones_col], axis=1).astype(BF16)
        bc, br = b_cols[:, H + h:H + h + 1], b_rows[H + h:H + h + 1, :]
        ic, ir = gt[:, h:h + 1], gt_t[h:h + 1, :]
        m_prev = m_refs[h][0:1, 0:1]
        g = bc[L - 1:L, :]
        dmat = jnp.where(causal, bc - br + ir, -jnp.inf)
        inter = bc + m_prev
        m_j = jnp.maximum(inter, jnp.max(dmat, axis=-1, keepdims=True))
        s = _dot_nt(q, kf.astype(BF16)) * jnp.exp(dmat - m_j)
        w_inter = jnp.exp(inter - m_j)
        cn = cn_refs[h][...]
        nd = (jnp.dot(s.astype(BF16), v_ext, preferred_element_type=F32)
              + w_inter * jnp.dot(q, cn.astype(BF16), preferred_element_type=F32))
        num, den = nd[:, :MLSTM_V_DIM], nd[:, MLSTM_V_DIM:MLSTM_V_DIM + 1]
        hid = num / jnp.maximum(jnp.abs(den), jnp.exp(-m_j))
        u = g - bc + ic
        m_new = jnp.maximum(g + m_prev, jnp.max(u, axis=0, keepdims=True))
        wk = (kf * jnp.exp(u - m_new)).astype(BF16)
        cn_refs[h][...] = jnp.exp(g + m_prev - m_new) * cn + _dot_tn(wk, v_ext)
        m_refs[h][...] = jnp.broadcast_to(m_new, m_refs[h].shape)
        vs = slice(h * MLSTM_V_DIM, (h + 1) * MLSTM_V_DIM)
        y_ref[:, vs] = (hid * _sigmoid(og_ref[:, vs])).astype(y_ref.dtype)


def _mlstm_mixer(proj, batch, seq, conv_w, b_if):
    steps = seq // MLSTM_CHUNK
    R = MLSTM_CHUNK
    QK2 = 2 * MLSTM_QK_WIDTH
    VW = MLSTM_V_WIDTH

    def col_spec(width, block_idx):
        return pl.BlockSpec((R, width), lambda b, i, _j=block_idx: (b * steps + i, _j))

    return pl.pallas_call(
        _mlstm_kernel,
        grid=(batch, steps),
        in_specs=[col_spec(QK2, 0), col_spec(VW, QK2 // VW), col_spec(VW, QK2 // VW + 1),
                  col_spec(LANES, (QK2 + 2 * VW) // LANES),
                  pl.BlockSpec((MLSTM_CONV, QK2), lambda b, i: (0, 0)),
                  pl.BlockSpec((1, LANES), lambda b, i: (0, 0))],
        out_specs=pl.BlockSpec((R, VW), lambda b, i: (b * steps + i, 0)),
        out_shape=jax.ShapeDtypeStruct((batch * seq, VW), BF16),
        scratch_shapes=([pltpu.VMEM((R + 8, QK2), F32)]
                        + [pltpu.VMEM((MLSTM_QK_DIM, MLSTM_V_DIM + LANES), F32)] * MLSTM_HEADS
                        + [pltpu.VMEM((8, LANES), F32)] * MLSTM_HEADS),
        compiler_params=_cparams("parallel", "arbitrary"),
        name="mlstm",
    )(proj, proj, proj, proj, conv_w, b_if)


def _router_kernel(h_ref, whi_ref, wlo_ref, bias_ref, eidx_ref, wgt_ref, rank_ref, cnt_ref, carry_ref):
    TT, E, G, GS = ROUTE_TOK, N_EXPERTS, N_GROUPS, GROUP_SIZE

    @pl.when(pl.program_id(0) == 0)
    def _():
        carry_ref[...] = jnp.zeros_like(carry_ref)

    h = h_ref[...]
    h_hi = h.astype(BF16)
    h_lo = (h - h_hi.astype(F32)).astype(BF16)
    logits = _dot_nt(whi_ref[...], h_hi) + _dot_nt(whi_ref[...], h_lo) + _dot_nt(wlo_ref[...], h_hi)
    scores = _sigmoid(logits)
    biased = scores + bias_ref[:, 0:1]
    neg = -jnp.inf

    io_g = lax.broadcasted_iota(I32, (GS, TT), 0)
    io8 = lax.broadcasted_iota(I32, (G, TT), 0)
    gs = jnp.zeros((G, TT), F32)
    for g in range(G):
        blk = biased[g * GS:(g + 1) * GS, :]
        m1 = jnp.max(blk, axis=0, keepdims=True)
        i1 = jnp.min(jnp.where(blk == m1, io_g, GS), axis=0, keepdims=True)
        m2 = jnp.max(jnp.where(io_g == i1, neg, blk), axis=0, keepdims=True)
        gs = jnp.where(io8 == g, m1 + m2, gs)
    gsel = jnp.zeros((G, TT), F32)
    for _ in range(TOPK_GROUPS):
        m = jnp.max(gs, axis=0, keepdims=True)
        idx = jnp.min(jnp.where(gs == m, io8, G), axis=0, keepdims=True)
        hit = io8 == idx
        gsel = jnp.where(hit, 1.0, gsel)
        gs = jnp.where(hit, neg, gs)
    sel = jnp.concatenate([jnp.broadcast_to(gsel[g:g + 1, :], (GS, TT)) for g in range(G)], axis=0)
    masked = jnp.where(sel > 0.0, biased, neg)

    io_e = lax.broadcasted_iota(I32, (E, TT), 0)
    eidx = jnp.zeros((TOP_K, TT), I32)
    wsel = jnp.zeros((TOP_K, TT), F32)
    chosen = jnp.zeros((E, TT), F32)
    for k in range(TOP_K):
        m = jnp.max(masked, axis=0, keepdims=True)
        idx = jnp.min(jnp.where(masked == m, io_e, E), axis=0, keepdims=True)
        hit = io_e == idx
        sc = jnp.sum(jnp.where(hit, scores, 0.0), axis=0, keepdims=True)
        eidx = jnp.where(io8 == k, idx, eidx)
        wsel = jnp.where(io8 == k, sc, wsel)
        chosen = jnp.where(hit, 1.0, chosen)
        masked = jnp.where(hit, neg, masked)
    wgt_ref[...] = wsel / jnp.sum(wsel, axis=0, keepdims=True) * ROUTED_SCALE
    eidx_ref[...] = eidx

    triu = (lax.broadcasted_iota(I32, (TT, TT), 0) <= lax.broadcasted_iota(I32, (TT, TT), 1)).astype(BF16)
    cum = jnp.dot(chosen.astype(BF16), triu, preferred_element_type=F32)
    carry = carry_ref[:, 0:1]
    before = cum - chosen + carry
    rank = jnp.zeros((TOP_K, TT), F32)
    for k in range(TOP_K):
        r = jnp.sum(jnp.where(io_e == eidx[k:k + 1, :], before, 0.0), axis=0, keepdims=True)
        rank = jnp.where(io8 == k, r, rank)
    rank_ref[...] = rank.astype(I32)
    total = carry + cum[:, TT - 1:TT]
    carry_ref[...] = jnp.broadcast_to(total, carry_ref.shape)
    cnt_ref[...] = jnp.broadcast_to(total, cnt_ref.shape)


def _router(h2, wt_hi, wt_lo, bias_col):
    n, d = h2.shape
    TT, E = ROUTE_TOK, N_EXPERTS
    tok_spec = pl.BlockSpec((TOP_K, TT), lambda i: (0, i))
    return pl.pallas_call(
        _router_kernel,
        grid=(n // TT,),
        in_specs=[pl.BlockSpec((TT, d), lambda i: (i, 0)),
                  pl.BlockSpec((E, d), lambda i: (0, 0)),
                  pl.BlockSpec((E, d), lambda i: (0, 0)),
                  pl.BlockSpec((E, LANES), lambda i: (0, 0))],
        out_specs=[tok_spec, tok_spec, tok_spec, pl.BlockSpec((E, LANES), lambda i: (0, 0))],
        out_shape=[jax.ShapeDtypeStruct((TOP_K, n), I32), jax.ShapeDtypeStruct((TOP_K, n), F32),
                   jax.ShapeDtypeStruct((TOP_K, n), I32), jax.ShapeDtypeStruct((E, LANES), F32)],
        scratch_shapes=[pltpu.VMEM((E, LANES), F32)],
        compiler_params=_cparams("arbitrary"),
        name="router",
    )(h2, wt_hi, wt_lo, bias_col)


def _dest_kernel(eidx_ref, rank_ref, offs_ref, dest_ref):
    TT, E = eidx_ref.shape[1], N_EXPERTS
    io_e = lax.broadcasted_iota(I32, (E, TT), 0)
    io8 = lax.broadcasted_iota(I32, (TOP_K, TT), 0)
    offs = offs_ref[:, 0:1]
    base = jnp.zeros((TOP_K, TT), F32)
    for k in range(TOP_K):
        r = jnp.sum(jnp.where(io_e == eidx_ref[k:k + 1, :], offs, 0.0), axis=0, keepdims=True)
        base = jnp.where(io8 == k, r, base)
    dest_ref[...] = base.astype(I32) + rank_ref[...]


def _dest_rows(eidx, rank, offs_col):
    n = eidx.shape[1]
    TT = 512
    spec = pl.BlockSpec((TOP_K, TT), lambda i: (0, i))
    return pl.pallas_call(
        _dest_kernel,
        grid=(n // TT,),
        in_specs=[spec, spec, pl.BlockSpec((N_EXPERTS, LANES), lambda i: (0, 0))],
        out_specs=spec,
        out_shape=jax.ShapeDtypeStruct((TOP_K, n), I32),
        compiler_params=_cparams("parallel"),
        name="dest_rows",
    )(eidx, rank, offs_col)


def _dispatch_kernel(dest_ref, x_ref, out_ref, sem):
    TT = x_ref.shape[0]

    def row_copy(t, k):
        return pltpu.make_async_copy(x_ref.at[pl.ds(t, 1)], out_ref.at[pl.ds(dest_ref[k, t], 1)], sem)

    def issue(t, carry):
        for k in range(TOP_K):
            row_copy(t, k).start()
        return carry

    lax.fori_loop(0, TT, issue, 0)
    for _ in range(TOP_K):
        pltpu.make_async_copy(x_ref, out_ref.at[pl.ds(0, TT)], sem).wait()


def _dispatch(dest, x2):
    n, d = x2.shape
    TT = DISPATCH_TOK
    return pl.pallas_call(
        _dispatch_kernel,
        grid=(n // TT,),
        in_specs=[pl.BlockSpec((TOP_K, TT), lambda i: (0, i), memory_space=pltpu.SMEM),
                  pl.BlockSpec((TT, d), lambda i: (i, 0))],
        out_specs=pl.BlockSpec(memory_space=pl.ANY),
        out_shape=jax.ShapeDtypeStruct((n * TOP_K, d), F32),
        scratch_shapes=[pltpu.SemaphoreType.DMA],
        compiler_params=_cparams("arbitrary"),
        name="dispatch",
    )(dest, x2)


def _gmm_kernel(tile_ref, grp_ref, lo_ref, hi_ref, first_ref, newgrp_ref, x_ref, win_ref, wdn_ref, o_ref,
                win_bf, wdn_bf):
    v = pl.program_id(0)
    lo, hi = lo_ref[v], hi_ref[v]

    @pl.when(newgrp_ref[v] == 1)
    def _():
        win_bf[...] = win_ref[...].astype(BF16)
        wdn_bf[...] = wdn_ref[...].astype(BF16)

    @pl.when(hi > lo)
    def _():
        gu = jnp.dot(x_ref[...].astype(BF16), win_bf[...], preferred_element_type=F32)
        act = (_silu(gu[:, :EXPERT_FF]) * gu[:, EXPERT_FF:]).astype(BF16)
        y = jnp.dot(act, wdn_bf[...], preferred_element_type=F32)
        rows = lax.broadcasted_iota(I32, (GMM_ROWS, 1), 0)
        mine = (rows >= lo) & (rows < hi)

        @pl.when(first_ref[v] == 1)
        def _():
            o_ref[...] = jnp.where(mine, y, 0.0)

        @pl.when(first_ref[v] == 0)
        def _():
            o_ref[...] = jnp.where(mine, y, o_ref[...])


def _gmm(meta, xs, w_in_all, w_dn_all, layer):
    p, d = xs.shape
    visits = meta[0].shape[0]
    TM = GMM_ROWS
    grid_spec = pltpu.PrefetchScalarGridSpec(
        num_scalar_prefetch=len(meta),
        grid=(visits,),
        in_specs=[pl.BlockSpec((TM, d), lambda v, tile, grp, *_: (tile[v], 0)),
                  pl.BlockSpec((None, None, d, 2 * EXPERT_FF), lambda v, tile, grp, *_: (layer, grp[v], 0, 0)),
                  pl.BlockSpec((None, None, EXPERT_FF, d), lambda v, tile, grp, *_: (layer, grp[v], 0, 0))],
        out_specs=pl.BlockSpec((TM, d), lambda v, tile, grp, *_: (tile[v], 0)),
        scratch_shapes=[pltpu.VMEM((d, 2 * EXPERT_FF), BF16), pltpu.VMEM((EXPERT_FF, d), BF16)],
    )
    return pl.pallas_call(
        _gmm_kernel,
        grid_spec=grid_spec,
        out_shape=jax.ShapeDtypeStruct((p, d), F32),
        compiler_params=_cparams("arbitrary"),
        name="gmm",
    )(*meta, xs, w_in_all, w_dn_all)


def _gmm_schedule(counts, n_rows):
    TM = GMM_ROWS
    E = counts.shape[0]
    max_visits = n_rows // TM + E - 1
    ends = jnp.cumsum(counts)
    starts = ends - counts
    first_tile = starts // TM
    n_vis = jnp.where(counts > 0, jnp.maximum(ends - 1, 0) // TM - first_tile + 1, 0)
    vis_end = jnp.cumsum(n_vis)
    vis_start = vis_end - n_vis
    n_real = vis_end[-1]
    v = jnp.minimum(jnp.arange(max_visits, dtype=I32), jnp.maximum(n_real - 1, 0))
    real = jnp.arange(max_visits, dtype=I32) < n_real
    grp = jnp.minimum(jnp.sum((vis_end[None, :] <= v[:, None]).astype(I32), axis=1), E - 1)
    onehot = grp[:, None] == jnp.arange(E, dtype=I32)[None, :]

    def take(table):
        return jnp.sum(jnp.where(onehot, table[None, :], 0), axis=1)

    tile = take(first_tile) + (v - take(vis_start))
    lo = jnp.where(real, jnp.maximum(take(starts), tile * TM) - tile * TM, 0)
    hi = jnp.where(real, jnp.minimum(take(ends), (tile + 1) * TM) - tile * TM, 0)
    prev = lambda a: jnp.concatenate([jnp.full((1,), -1, I32), a[:-1]])
    first = real & (tile != prev(tile))
    newgrp = real & (grp != prev(grp))
    return tuple(a.astype(I32) for a in (tile, grp, lo, hi, first, newgrp))


def _combine_kernel(dest_ref, wgt_ref, h_ref, ys_ref, wsi_ref, wsd_ref, g_ref, b_ref, o_ref, buf_ref, sem):
    TT = h_ref.shape[0]

    def row_copy(t, k):
        return pltpu.make_async_copy(ys_ref.at[pl.ds(dest_ref[k, t], 1)], buf_ref.at[k, pl.ds(t, 1)], sem)

    def issue(t, carry):
        for k in range(TOP_K):
            row_copy(t, k).start()
        return carry

    lax.fori_loop(0, TT, issue, 0)

    h = h_ref[...]
    gu = jnp.dot(h.astype(BF16), wsi_ref[...], preferred_element_type=F32)
    act = (_silu(gu[:, :SHARED_FF]) * gu[:, SHARED_FF:]).astype(BF16)
    acc = ALPHA * h + jnp.dot(act, wsd_ref[...], preferred_element_type=F32)

    for k in range(TOP_K):
        pltpu.make_async_copy(ys_ref.at[pl.ds(0, TT)], buf_ref.at[k], sem).wait()
    wgt = wgt_ref[...]
    for k in range(TOP_K):
        acc = acc + wgt[:, k:k + 1] * buf_ref[k]
    o_ref[...] = _layernorm(acc, g_ref[...], b_ref[...])


def _combine(dest, wgt_tk, h2, ys, w_si, w_sd, g, b):
    n, d = h2.shape
    TT = COMBINE_TOK
    const2 = lambda shape: pl.BlockSpec(shape, lambda i: (0, 0))
    return pl.pallas_call(
        _combine_kernel,
        grid=(n // TT,),
        in_specs=[pl.BlockSpec((TOP_K, TT), lambda i: (0, i), memory_space=pltpu.SMEM),
                  pl.BlockSpec((TT, TOP_K), lambda i: (i, 0)),
                  pl.BlockSpec((TT, d), lambda i: (i, 0)),
                  pl.BlockSpec(memory_space=pl.ANY),
                  const2((d, 2 * SHARED_FF)), const2((SHARED_FF, d)), const2((1, d)), const2((1, d))],
        out_specs=pl.BlockSpec((TT, d), lambda i: (i, 0)),
        out_shape=jax.ShapeDtypeStruct((n, d), F32),
        scratch_shapes=[pltpu.VMEM((TOP_K, TT, d), F32), pltpu.SemaphoreType.DMA],
        compiler_params=_cparams("arbitrary"),
        name="combine",
    )(dest, wgt_tk, h2, ys, w_si, w_sd, g, b)


def _moe_ln(h2, w_router, r_bias, w_e_in_all, w_e_dn_all, layer, w_s_in, w_s_dn, ln_g, ln_b):
    n, d = h2.shape
    wt = w_router.T
    wt_hi = wt.astype(BF16)
    wt_lo = (wt - wt_hi.astype(F32)).astype(BF16)
    bias_col = jnp.broadcast_to(r_bias.astype(F32)[:, None], (N_EXPERTS, LANES))
    eidx, wgt, rank, cnt = _router(h2, wt_hi, wt_lo, bias_col)
    counts = cnt[:, 0].astype(I32)
    offs = jnp.cumsum(counts) - counts
    offs_col = jnp.broadcast_to(offs.astype(F32)[:, None], (N_EXPERTS, LANES))
    dest = _dest_rows(eidx, rank, offs_col)
    xs = _dispatch(dest, h2)
    ys = _gmm(_gmm_schedule(counts, n * TOP_K), xs, w_e_in_all, w_e_dn_all, layer)
    return _combine(dest, wgt.T, h2, ys, w_s_in.astype(BF16), w_s_dn.astype(BF16),
                    ln_g[None, :], ln_b[None, :])


def _pad_cols(w, width):
    return jnp.pad(w, ((0, 0), (0, width - w.shape[1])))


def kernel(x, w_in_ab, w_gla_gate_up, b_gla_gate, hgrn_norm_g, gla_norm_g, w_out_ab, hgrn_lb_logits, w_in_c, conv_c, b_if_c, w_out_c, w_router, router_bias, w_exp_in, w_exp_down, w_sh_in, w_sh_down, ln_mix_g, ln_mix_b, ln_ffn_g, ln_ffn_b):
    B, T, D = x.shape
    lower_bounds = jnp.cumsum(jax.nn.softmax(hgrn_lb_logits.astype(F32), axis=0), axis=0)
    h = x.reshape(B * T, D)
    for l in range(DEPTH):
        j = l // 2
        if l % 2 == 0:
            ab_pad = AB_WIDTH - GLA_GATE_RANK + LANES
            proj = _linear(h, _pad_cols(w_in_ab[j], ab_pad).astype(BF16))
            w_up = jnp.pad(w_gla_gate_up[j], ((0, LANES - GLA_GATE_RANK), (0, 0))).astype(BF16)
            y = _hgrn_gla_mixer(proj, B, T, lower_bounds[l][None, :], w_up, b_gla_gate[j][None, :],
                                hgrn_norm_g[j][None, :], gla_norm_g[j][None, :])
            w_out = w_out_ab[j]
        else:
            c_pad = C_WIDTH - 2 * MLSTM_HEADS + LANES
            proj = _linear(h, _pad_cols(w_in_c[j], c_pad).astype(BF16))
            b_if = jnp.pad(b_if_c[j].astype(F32), (0, LANES - 2 * MLSTM_HEADS))[None, :]
            y = _mlstm_mixer(proj, B, T, conv_c[j], b_if)
            w_out = w_out_c[j]
        h = _outproj_ln(h, y, w_out.astype(BF16), ln_mix_g[l][None, :], ln_mix_b[l][None, :])
        h = _moe_ln(h, w_router[l], router_bias[l], w_exp_in, w_exp_down, l, w_sh_in[l], w_sh_down[l],
                    ln_ffn_g[l], ln_ffn_b[l])
    return h.reshape(B, T, D)
```

```python
import functools

import jax
import jax.numpy as jnp
from jax import lax
from jax.experimental import pallas as pl
from jax.experimental.pallas import tpu as pltpu
from jax.experimental.pallas import tpu_sc as plsc

F32 = jnp.float32
BF16 = jnp.bfloat16
I32 = jnp.int32

D_MODEL = 1024
DEPTH = 2
HGRN_HEADS = 4
HGRN_HEAD_DIM = 128
HGRN_WIDTH = HGRN_HEADS * HGRN_HEAD_DIM
GLA_HEADS = 4
GLA_KEY_DIM = 64
GLA_VAL_DIM = 128
GLA_KEY_WIDTH = GLA_HEADS * GLA_KEY_DIM
GLA_VAL_WIDTH = GLA_HEADS * GLA_VAL_DIM
GLA_GATE_RANK = 16
GLA_GATE_NORMALIZER = 16.0
LIN_CHUNK = 64
LIN_SUB = 16
AB_WIDTH = 4 * HGRN_WIDTH + 2 * GLA_KEY_WIDTH + 2 * GLA_VAL_WIDTH + GLA_GATE_RANK
MLSTM_HEADS = 4
MLSTM_QK_DIM = 128
MLSTM_V_DIM = 256
MLSTM_QK_WIDTH = MLSTM_HEADS * MLSTM_QK_DIM
MLSTM_V_WIDTH = MLSTM_HEADS * MLSTM_V_DIM
MLSTM_CONV = 4
MLSTM_CHUNK = 128
C_WIDTH = 2 * MLSTM_QK_WIDTH + 2 * MLSTM_V_WIDTH + 2 * MLSTM_HEADS
N_EXPERTS = 256
TOP_K = 8
N_GROUPS = 8
TOPK_GROUPS = 4
GROUP_SIZE = N_EXPERTS // N_GROUPS
EXPERT_FF = 256
SHARED_FF = 256
ROUTED_SCALE = 2.5
ALPHA = (2 * DEPTH) ** 0.25
LN_EPS = 1e-5
RMS_EPS = 1e-6

LANES = 128
VMEM_LIMIT = 56 * 1024 * 1024

ROW_TILE = 256
MIX_ROWS = 128
ROUTE_TOK = 128
COMBINE_TOK = 256
GMM_ROWS = 256
SC_CORES, SC_SUBCORES = 2, 16
SC_WORKERS = SC_CORES * SC_SUBCORES
SC_ROWS = 64


def _cparams(*sem):
    return pltpu.CompilerParams(dimension_semantics=sem, vmem_limit_bytes=VMEM_LIMIT)


def _sigmoid(x):
    return 1.0 / (1.0 + jnp.exp(-x))


def _silu(x):
    return x * _sigmoid(x)


def _log_sigmoid(x):
    return jnp.minimum(x, 0.0) - jnp.log(1.0 + jnp.exp(-jnp.abs(x)))


def _split3(x):
    hi = x.astype(BF16)
    r = x - hi.astype(F32)
    mid = r.astype(BF16)
    lo = (r - mid.astype(F32)).astype(BF16)
    return hi, mid, lo


def _tri_dot_left(tri, x):
    return sum(jnp.dot(tri, p, preferred_element_type=F32) for p in _split3(x))


def _tri_dot_right(x, tri):
    return sum(jnp.dot(p, tri, preferred_element_type=F32) for p in _split3(x))


def _dot_nt(a, b):
    return lax.dot_general(a, b, (((1,), (1,)), ((), ())), preferred_element_type=F32)


def _dot_tn(a, b):
    return lax.dot_general(a, b, (((0,), (0,)), ((), ())), preferred_element_type=F32)


HI16 = -65536


def _pack_bf16_pair(x):
    c = x.shape[1] // 2
    bits = lambda v: lax.bitcast_convert_type(v.astype(BF16).astype(F32), I32)
    return (bits(x[:, c:]) & HI16) | lax.shift_right_logical(bits(x[:, :c]), 16)


def _unpack_bf16_pair(w):
    lo = lax.bitcast_convert_type(lax.shift_left(w, 16), F32)
    hi = lax.bitcast_convert_type(w & HI16, F32)
    return lo.astype(BF16), hi.astype(BF16)


def _layernorm(x, g, b):
    mu = jnp.mean(x, axis=-1, keepdims=True)
    xc = x - mu
    var = jnp.mean(xc * xc, axis=-1, keepdims=True)
    return xc * lax.rsqrt(var + LN_EPS) * g + b


def _linear_kernel(x_ref, w_ref, o_ref):
    o_ref[...] = jnp.dot(x_ref[...].astype(BF16), w_ref[...], preferred_element_type=F32)


def _linear(x, w):
    n, k = x.shape
    m = w.shape[1]
    return pl.pallas_call(
        _linear_kernel,
        grid=(n // ROW_TILE,),
        in_specs=[pl.BlockSpec((ROW_TILE, k), lambda i: (i, 0)),
                  pl.BlockSpec((k, m), lambda i: (0, 0))],
        out_specs=pl.BlockSpec((ROW_TILE, m), lambda i: (i, 0)),
        out_shape=jax.ShapeDtypeStruct((n, m), F32),
        compiler_params=_cparams("parallel"),
        name="linear",
    )(x, w)


def _outproj_ln_kernel(x_ref, y_ref, w_ref, g_ref, b_ref, o_ref, op_ref):
    mix = jnp.dot(y_ref[...], w_ref[...], preferred_element_type=F32)
    h = _layernorm(ALPHA * x_ref[...] + mix, g_ref[...], b_ref[...])
    o_ref[...] = h
    op_ref[...] = _pack_bf16_pair(h)


def _outproj_ln(x, y, w, g, b):
    n, d = x.shape
    k = y.shape[1]
    return pl.pallas_call(
        _outproj_ln_kernel,
        grid=(n // ROW_TILE,),
        in_specs=[pl.BlockSpec((ROW_TILE, d), lambda i: (i, 0)),
                  pl.BlockSpec((ROW_TILE, k), lambda i: (i, 0)),
                  pl.BlockSpec((k, d), lambda i: (0, 0)),
                  pl.BlockSpec((1, d), lambda i: (0, 0)),
                  pl.BlockSpec((1, d), lambda i: (0, 0))],
        out_specs=[pl.BlockSpec((ROW_TILE, d), lambda i: (i, 0)),
                   pl.BlockSpec((ROW_TILE, d // 2), lambda i: (i, 0))],
        out_shape=[jax.ShapeDtypeStruct((n, d), F32), jax.ShapeDtypeStruct((n, d // 2), I32)],
        compiler_params=_cparams("parallel"),
        name="outproj_ln",
    )(x, y, w, g, b)


N_UNITS = HGRN_HEADS + GLA_HEADS // 2
HALF_LANES = LANES // 2
LOG2E = 1.4426950408889634


def _pair_selector():
    assert LIN_CHUNK == HALF_LANES == GLA_KEY_DIM
    r = jnp.arange(LIN_SUB * LANES)[:, None]
    c = jnp.arange(LANES)[None, :]
    same_slot = (r // LANES) == (c % LIN_SUB)
    same_half = ((r % LANES) < HALF_LANES) == (c < HALF_LANES)
    return (same_slot & same_half).astype(BF16)


def _pairwise_tiles(q, k, b, lhs_ref, row0):
    c = LIN_SUB
    b2 = b * LOG2E
    for blk in range(LIN_CHUNK // c):
        lo = blk * c
        qb, kb, bb = q[lo:lo + c], k[lo:lo + c], b2[lo:lo + c]
        for j in range(c):
            e = jnp.exp2(jnp.minimum(bb - bb[j:j + 1], 0.0))
            lhs_ref[row0 + lo:row0 + lo + c, j * LANES:(j + 1) * LANES] = (qb * kb[j:j + 1] * e).astype(BF16)


def _unit_chunk(q, k, b, vs, r, st_ref, masks):
    L, c = LIN_CHUNK, LIN_SUB
    lane_lo, col_mod, same_blk_causal = masks
    heads = [lane_lo, ~lane_lo] if len(vs) == 2 else [None]

    def pick(x, m):
        return x if m is None else jnp.where(m, x, jnp.zeros_like(x))

    g = b[L - 1:L, :]
    st = st_ref[...]
    st_b = st.astype(BF16)
    qx = (q * jnp.exp(b)).astype(BF16)
    outs = [_dot_nt(pick(qx, m), st_b) for m in heads]
    k_end = (k * jnp.exp(g - b)).astype(BF16)
    upd = [_dot_tn(v.astype(BF16), k_end) for v in vs]
    st_ref[...] = st * jnp.exp(g) + (upd[0] if len(vs) == 1 else jnp.where(lane_lo, upd[0], upd[1]))

    off_rows = [jnp.zeros((c, LANES), F32)]
    for blk in range(1, L // c):
        lo = blk * c
        ref = b[lo - 1:lo]
        q_in = (q[lo:lo + c] * jnp.exp(b[lo:lo + c] - ref)).astype(BF16)
        k_in = (k * jnp.exp(jnp.minimum(ref - b, 0.0))).astype(BF16)
        if len(vs) == 2:
            k_cat = jnp.concatenate([pick(k_in, heads[0]), pick(k_in, heads[1])], axis=0)
        else:
            k_cat = jnp.concatenate([k_in, jnp.zeros_like(k_in)], axis=0)
        off_rows.append(jnp.where(col_mod < lo, _dot_nt(q_in, k_cat), 0.0))
    a = jnp.where(same_blk_causal, r, jnp.concatenate(off_rows, axis=0)).astype(BF16)
    v_cat = jnp.concatenate([vs[0], vs[-1]], axis=0).astype(BF16)
    return [o + jnp.dot(pick(a, m), v_cat, preferred_element_type=F32) for o, m in zip(outs, heads)]


def _rms_gate(o, g, gate):
    o = o * lax.rsqrt(jnp.mean(o * o, axis=-1, keepdims=True) + RMS_EPS)
    return o * g * _silu(gate)


def _hgrn_gla_kernel(hq_ref, hf_ref, hi_ref, hg_ref, gq_ref, gk_ref, gv_ref, gg_ref, glr_ref,
                     lb_ref, wup_ref, bgk_ref, hng_ref, gng_ref, sel_ref, y_ref, lhs_ref, *st_refs):
    @pl.when(pl.program_id(1) == 0)
    def _():
        for r in st_refs:
            r[...] = jnp.zeros_like(r)

    L = LIN_CHUNK
    n_chunks = MIX_ROWS // L
    r_i = lax.broadcasted_iota(I32, (L, LANES), 0)
    c_i = lax.broadcasted_iota(I32, (L, LANES), 1)
    col_mod = jnp.where(c_i < HALF_LANES, c_i, c_i - HALF_LANES)
    same_blk_causal = (col_mod // LIN_SUB == r_i // LIN_SUB) & (col_mod <= r_i)
    c_sub = lax.broadcasted_iota(I32, (LIN_SUB, LANES), 1)
    col_mod_sub = jnp.where(c_sub < HALF_LANES, c_sub, c_sub - HALF_LANES)
    lane_lo = lax.broadcasted_iota(I32, (1, LANES), 1) < HALF_LANES
    masks = (lane_lo, col_mod_sub, same_blk_causal)
    tril = (lax.broadcasted_iota(I32, (L, L), 0) >= lax.broadcasted_iota(I32, (L, L), 1)).astype(BF16)
    lb = lb_ref[...]

    units = []
    for ck in range(n_chunks):
        rs = slice(ck * L, (ck + 1) * L)
        u = jnp.dot(glr_ref[rs, :].astype(BF16), wup_ref[...], preferred_element_type=F32) + bgk_ref[...]
        la_g = _log_sigmoid(u) * (1.0 / GLA_GATE_NORMALIZER)
        qs, ks, las = [], [], []
        for h in range(HGRN_HEADS):
            cs = slice(h * LANES, (h + 1) * LANES)
            z = hf_ref[rs, cs]
            lbh = lb[:, cs]
            qs.append(_silu(hq_ref[rs, cs]))
            ks.append((1.0 - lbh) * _sigmoid(-z))
            las.append(jnp.log(lbh + (1.0 - lbh) * _sigmoid(z)))
        for p in range(GLA_HEADS // 2):
            cs = slice(p * LANES, (p + 1) * LANES)
            qs.append(gq_ref[rs, cs] * (GLA_KEY_DIM ** -0.5))
            ks.append(gk_ref[rs, cs])
            las.append(la_g[:, cs])
        b_all = _tri_dot_left(tril, jnp.concatenate(las, axis=1))
        for un in range(N_UNITS):
            b = b_all[:, un * LANES:(un + 1) * LANES]
            _pairwise_tiles(qs[un], ks[un], b, lhs_ref, (ck * N_UNITS + un) * L)
            units.append((qs[un], ks[un], b))

    r_all = jnp.dot(lhs_ref[...], sel_ref[...], preferred_element_type=F32)

    for ck in range(n_chunks):
        rs = slice(ck * L, (ck + 1) * L)
        for un in range(N_UNITS):
            uc = ck * N_UNITS + un
            q, k, b = units[uc]
            r = r_all[uc * L:(uc + 1) * L]
            if un < HGRN_HEADS:
                cs = slice(un * LANES, (un + 1) * LANES)
                (o,) = _unit_chunk(q, k, b, [hi_ref[rs, cs]], r, st_refs[un], masks)
                y_ref[rs, cs] = _rms_gate(o, hng_ref[:, cs], hg_ref[rs, cs]).astype(y_ref.dtype)
            else:
                p = un - HGRN_HEADS
                vcs = [slice((2 * p + i) * GLA_VAL_DIM, (2 * p + i + 1) * GLA_VAL_DIM) for i in range(2)]
                outs = _unit_chunk(q, k, b, [gv_ref[rs, vc] for vc in vcs], r, st_refs[un], masks)
                for o, vc in zip(outs, vcs):
                    ys = slice(HGRN_WIDTH + vc.start, HGRN_WIDTH + vc.stop)
                    y_ref[rs, ys] = _rms_gate(o, gng_ref[:, vc], gg_ref[rs, vc]).astype(y_ref.dtype)


def _hgrn_gla_mixer(proj, batch, seq, lb, w_up, b_gk, hgrn_g, gla_g):
    steps = seq // MIX_ROWS
    R = MIX_ROWS

    def col_spec(width, block_idx):
        return pl.BlockSpec((R, width), lambda b, i, _j=block_idx: (b * steps + i, _j))

    def const_spec(shape):
        return pl.BlockSpec(shape, lambda b, i: (0,) * len(shape))

    W = HGRN_WIDTH
    in_specs = [col_spec(W, 0), col_spec(W, 1), col_spec(W, 2), col_spec(W, 3),
                col_spec(GLA_KEY_WIDTH, 4 * W // GLA_KEY_WIDTH),
                col_spec(GLA_KEY_WIDTH, 4 * W // GLA_KEY_WIDTH + 1),
                col_spec(GLA_VAL_WIDTH, (4 * W + 2 * GLA_KEY_WIDTH) // GLA_VAL_WIDTH),
                col_spec(GLA_VAL_WIDTH, (4 * W + 2 * GLA_KEY_WIDTH) // GLA_VAL_WIDTH + 1),
                col_spec(LANES, (AB_WIDTH - GLA_GATE_RANK) // LANES),
                const_spec((1, W)), const_spec((LANES, GLA_KEY_WIDTH)), const_spec((1, GLA_KEY_WIDTH)),
                const_spec((1, W)), const_spec((1, GLA_VAL_WIDTH)),
                const_spec((LIN_SUB * LANES, LANES))]
    n_uc = (R // LIN_CHUNK) * N_UNITS
    return pl.pallas_call(
        _hgrn_gla_kernel,
        grid=(batch, steps),
        in_specs=in_specs,
        out_specs=pl.BlockSpec((R, W + GLA_VAL_WIDTH), lambda b, i: (b * steps + i, 0)),
        out_shape=jax.ShapeDtypeStruct((batch * seq, W + GLA_VAL_WIDTH), BF16),
        scratch_shapes=([pltpu.VMEM((n_uc * LIN_CHUNK, LIN_SUB * LANES), BF16)]
                        + [pltpu.VMEM((GLA_VAL_DIM, LANES), F32)] * N_UNITS),
        compiler_params=_cparams("parallel", "arbitrary"),
        name="hgrn_gla",
    )(*([proj] * 9), lb, w_up, b_gk, hgrn_g, gla_g, _pair_selector())


def _mlstm_kernel(qk_ref, v_ref, og_ref, gt_ref, cw_ref, bif_ref, y_ref, ext_ref, *state_refs):
    L = MLSTM_CHUNK
    PAD = 8
    cn_refs, m_refs = state_refs[:MLSTM_HEADS], state_refs[MLSTM_HEADS:]

    @pl.when(pl.program_id(1) == 0)
    def _():
        ext_ref[0:PAD, :] = jnp.zeros((PAD, ext_ref.shape[1]), F32)
        for r in state_refs:
            r[...] = jnp.zeros_like(r)

    ext_ref[PAD:PAD + L, :] = qk_ref[...]
    conv = jnp.zeros((L, ext_ref.shape[1]), F32)
    for w in range(MLSTM_CONV):
        conv = conv + cw_ref[w:w + 1, :] * ext_ref[pl.ds(PAD - (MLSTM_CONV - 1) + w, L), :]
    ext_ref[0:PAD, :] = ext_ref[L:L + PAD, :]
    qk = _silu(conv)

    gt = gt_ref[...] + bif_ref[...]
    gt_t = gt.T
    r_i = lax.broadcasted_iota(I32, (L, L), 0)
    c_i = lax.broadcasted_iota(I32, (L, L), 1)
    causal = c_i <= r_i
    tril = causal.astype(BF16)
    triu = (r_i <= c_i).astype(BF16)
    b_cols = _tri_dot_left(tril, _log_sigmoid(gt))
    b_rows = _tri_dot_right(_log_sigmoid(gt_t), triu)
    ones_col = (lax.broadcasted_iota(I32, (L, LANES), 1) == 0).astype(F32)

    H = MLSTM_HEADS
    for h in range(H):
        q = qk[:, h * MLSTM_QK_DIM:(h + 1) * MLSTM_QK_DIM].astype(BF16)
        kf = qk[:, MLSTM_QK_WIDTH + h * MLSTM_QK_DIM:MLSTM_QK_WIDTH + (h + 1) * MLSTM_QK_DIM] * (MLSTM_QK_DIM ** -0.5)
        v_ext = jnp.concatenate([v_ref[:, h * MLSTM_V_DIM:(h + 1) * MLSTM_V_DIM], ones_col], axis=1).astype(BF16)
        bc, br = b_cols[:, H + h:H + h + 1], b_rows[H + h:H + h + 1, :]
        ic, ir = gt[:, h:h + 1], gt_t[h:h + 1, :]
        m_prev = m_refs[h][0:1, 0:1]
        g = bc[L - 1:L, :]
        dmat = jnp.where(causal, bc - br + ir, -jnp.inf)
        inter = bc + m_prev
        m_j = jnp.maximum(inter, jnp.max(dmat, axis=-1, keepdims=True))
        s = _dot_nt(q, kf.astype(BF16)) * jnp.exp(dmat - m_j)
        w_inter = jnp.exp(inter - m_j)
        cn = cn_refs[h][...]
        nd = (jnp.dot(s.astype(BF16), v_ext, preferred_element_type=F32)
              + w_inter * jnp.dot(q, cn.astype(BF16), preferred_element_type=F32))
        num, den = nd[:, :MLSTM_V_DIM], nd[:, MLSTM_V_DIM:MLSTM_V_DIM + 1]
        hid = num / jnp.maximum(jnp.abs(den), jnp.exp(-m_j))
        u = g - bc + ic
        m_new = jnp.maximum(g + m_prev, jnp.max(u, axis=0, keepdims=True))
        wk = (kf * jnp.exp(u - m_new)).astype(BF16)
        cn_refs[h][...] = jnp.exp(g + m_prev - m_new) * cn + _dot_tn(wk, v_ext)
        m_refs[h][...] = jnp.broadcast_to(m_new, m_refs[h].shape)
        vs = slice(h * MLSTM_V_DIM, (h + 1) * MLSTM_V_DIM)
        y_ref[:, vs] = (hid * _sigmoid(og_ref[:, vs])).astype(y_ref.dtype)


def _mlstm_mixer(proj, batch, seq, conv_w, b_if):
    steps = seq // MLSTM_CHUNK
    R = MLSTM_CHUNK
    QK2 = 2 * MLSTM_QK_WIDTH
    VW = MLSTM_V_WIDTH

    def col_spec(width, block_idx):
        return pl.BlockSpec((R, width), lambda b, i, _j=block_idx: (b * steps + i, _j))

    return pl.pallas_call(
        _mlstm_kernel,
        grid=(batch, steps),
        in_specs=[col_spec(QK2, 0), col_spec(VW, QK2 // VW), col_spec(VW, QK2 // VW + 1),
                  col_spec(LANES, (QK2 + 2 * VW) // LANES),
                  pl.BlockSpec((MLSTM_CONV, QK2), lambda b, i: (0, 0)),
                  pl.BlockSpec((1, LANES), lambda b, i: (0, 0))],
        out_specs=pl.BlockSpec((R, VW), lambda b, i: (b * steps + i, 0)),
        out_shape=jax.ShapeDtypeStruct((batch * seq, VW), BF16),
        scratch_shapes=([pltpu.VMEM((R + 8, QK2), F32)]
                        + [pltpu.VMEM((MLSTM_QK_DIM, MLSTM_V_DIM + LANES), F32)] * MLSTM_HEADS
                        + [pltpu.VMEM((8, LANES), F32)] * MLSTM_HEADS),
        compiler_params=_cparams("parallel", "arbitrary"),
        name="mlstm",
    )(proj, proj, proj, proj, conv_w, b_if)


def _router_kernel(h_ref, whi_ref, wlo_ref, bias_ref, eidx_ref, wgt_ref, rank_ref, cnt_ref, carry_ref):
    TT, E, G, GS = ROUTE_TOK, N_EXPERTS, N_GROUPS, GROUP_SIZE

    @pl.when(pl.program_id(0) == 0)
    def _():
        carry_ref[...] = jnp.zeros_like(carry_ref)

    h = h_ref[...]
    h_hi = h.astype(BF16)
    h_lo = (h - h_hi.astype(F32)).astype(BF16)
    logits = _dot_nt(whi_ref[...], h_hi) + _dot_nt(whi_ref[...], h_lo) + _dot_nt(wlo_ref[...], h_hi)
    scores = _sigmoid(logits)
    biased = scores + bias_ref[:, 0:1]
    neg = -jnp.inf

    io_g = lax.broadcasted_iota(I32, (GS, TT), 0)
    io8 = lax.broadcasted_iota(I32, (G, TT), 0)
    gs = jnp.zeros((G, TT), F32)
    for g in range(G):
        blk = biased[g * GS:(g + 1) * GS, :]
        m1 = jnp.max(blk, axis=0, keepdims=True)
        i1 = jnp.min(jnp.where(blk == m1, io_g, GS), axis=0, keepdims=True)
        m2 = jnp.max(jnp.where(io_g == i1, neg, blk), axis=0, keepdims=True)
        gs = jnp.where(io8 == g, m1 + m2, gs)
    gsel = jnp.zeros((G, TT), F32)
    for _ in range(TOPK_GROUPS):
        m = jnp.max(gs, axis=0, keepdims=True)
        idx = jnp.min(jnp.where(gs == m, io8, G), axis=0, keepdims=True)
        hit = io8 == idx
        gsel = jnp.where(hit, 1.0, gsel)
        gs = jnp.where(hit, neg, gs)
    sel = jnp.concatenate([jnp.broadcast_to(gsel[g:g + 1, :], (GS, TT)) for g in range(G)], axis=0)
    masked = jnp.where(sel > 0.0, biased, neg)

    io_e = lax.broadcasted_iota(I32, (E, TT), 0)
    eidx = jnp.zeros((TOP_K, TT), I32)
    wsel = jnp.zeros((TOP_K, TT), F32)
    chosen = jnp.zeros((E, TT), F32)
    for k in range(TOP_K):
        m = jnp.max(masked, axis=0, keepdims=True)
        idx = jnp.min(jnp.where(masked == m, io_e, E), axis=0, keepdims=True)
        hit = io_e == idx
        sc = jnp.sum(jnp.where(hit, scores, 0.0), axis=0, keepdims=True)
        eidx = jnp.where(io8 == k, idx, eidx)
        wsel = jnp.where(io8 == k, sc, wsel)
        chosen = jnp.where(hit, 1.0, chosen)
        masked = jnp.where(hit, neg, masked)
    wgt_ref[...] = wsel / jnp.sum(wsel, axis=0, keepdims=True) * ROUTED_SCALE
    eidx_ref[...] = eidx

    triu = (lax.broadcasted_iota(I32, (TT, TT), 0) <= lax.broadcasted_iota(I32, (TT, TT), 1)).astype(BF16)
    cum = jnp.dot(chosen.astype(BF16), triu, preferred_element_type=F32)
    carry = carry_ref[:, 0:1]
    before = cum - chosen + carry
    rank = jnp.zeros((TOP_K, TT), F32)
    for k in range(TOP_K):
        r = jnp.sum(jnp.where(io_e == eidx[k:k + 1, :], before, 0.0), axis=0, keepdims=True)
        rank = jnp.where(io8 == k, r, rank)
    rank_ref[...] = rank.astype(I32)
    total = carry + cum[:, TT - 1:TT]
    carry_ref[...] = jnp.broadcast_to(total, carry_ref.shape)
    cnt_ref[...] = jnp.broadcast_to(total, cnt_ref.shape)


def _router(h2, wt_hi, wt_lo, bias_col):
    n, d = h2.shape
    TT, E = ROUTE_TOK, N_EXPERTS
    tok_spec = pl.BlockSpec((TOP_K, TT), lambda i: (0, i))
    return pl.pallas_call(
        _router_kernel,
        grid=(n // TT,),
        in_specs=[pl.BlockSpec((TT, d), lambda i: (i, 0)),
                  pl.BlockSpec((E, d), lambda i: (0, 0)),
                  pl.BlockSpec((E, d), lambda i: (0, 0)),
                  pl.BlockSpec((E, LANES), lambda i: (0, 0))],
        out_specs=[tok_spec, tok_spec, tok_spec, pl.BlockSpec((E, LANES), lambda i: (0, 0))],
        out_shape=[jax.ShapeDtypeStruct((TOP_K, n), I32), jax.ShapeDtypeStruct((TOP_K, n), F32),
                   jax.ShapeDtypeStruct((TOP_K, n), I32), jax.ShapeDtypeStruct((E, LANES), F32)],
        scratch_shapes=[pltpu.VMEM((E, LANES), F32)],
        compiler_params=_cparams("arbitrary"),
        name="router",
    )(h2, wt_hi, wt_lo, bias_col)


def _dest_kernel(eidx_ref, rank_ref, offs_ref, dest_ref):
    TT, E = eidx_ref.shape[1], N_EXPERTS
    io_e = lax.broadcasted_iota(I32, (E, TT), 0)
    io8 = lax.broadcasted_iota(I32, (TOP_K, TT), 0)
    offs = offs_ref[:, 0:1]
    base = jnp.zeros((TOP_K, TT), F32)
    for k in range(TOP_K):
        r = jnp.sum(jnp.where(io_e == eidx_ref[k:k + 1, :], offs, 0.0), axis=0, keepdims=True)
        base = jnp.where(io8 == k, r, base)
    dest_ref[...] = base.astype(I32) + rank_ref[...]


def _dest_rows(eidx, rank, offs_col):
    n = eidx.shape[1]
    TT = 512
    spec = pl.BlockSpec((TOP_K, TT), lambda i: (0, i))
    return pl.pallas_call(
        _dest_kernel,
        grid=(n // TT,),
        in_specs=[spec, spec, pl.BlockSpec((N_EXPERTS, LANES), lambda i: (0, 0))],
        out_specs=spec,
        out_shape=jax.ShapeDtypeStruct((TOP_K, n), I32),
        compiler_params=_cparams("parallel"),
        name="dest_rows",
    )(eidx, rank, offs_col)


def _sc_mesh():
    return plsc.VectorSubcoreMesh(core_axis_name="c", subcore_axis_name="s")


def _sc_worker_id():
    return lax.axis_index("s") * SC_CORES + lax.axis_index("c")


def _dispatch(dest_chunks, x2):
    n, d = x2.shape
    n_chunks = n // SC_ROWS // SC_WORKERS

    @functools.partial(
        pl.kernel, mesh=_sc_mesh(),
        out_type=jax.ShapeDtypeStruct((n * TOP_K, d), x2.dtype),
        scratch_types=[pltpu.VMEM((2, TOP_K, SC_ROWS), I32), pltpu.VMEM((2, SC_ROWS, d), x2.dtype),
                       pltpu.SemaphoreType.DMA((2,))],
    )
    def scatter_rows(x_hbm, idx_hbm, out_hbm, idx_v, rows_v, wsem):
        base = _sc_worker_id() * n_chunks

        def row_scatter(b, k):
            return pltpu.make_async_copy(rows_v.at[b], out_hbm.at[idx_v.at[b, k]], wsem.at[b])

        def drain(b):
            for k in range(TOP_K):
                row_scatter(b, k).wait()

        @pl.loop(0, n_chunks, step=2)
        def _(i):
            for b in range(2):
                c = base + i + b

                @pl.when(i > 0)
                def _():
                    drain(b)

                pltpu.sync_copy(idx_hbm.at[c], idx_v.at[b])
                pltpu.sync_copy(x_hbm.at[pl.ds(c * SC_ROWS, SC_ROWS)], rows_v.at[b])
                for k in range(TOP_K):
                    row_scatter(b, k).start()

        for b in range(2):
            drain(b)

    return scatter_rows(x2, dest_chunks)


def _gather_rows(src, idx):
    d = src.shape[1]
    p = idx.shape[0]
    per_w = p // SC_WORKERS
    n_chunks = per_w // SC_ROWS

    @functools.partial(
        pl.kernel, mesh=_sc_mesh(),
        out_type=jax.ShapeDtypeStruct((p, d), src.dtype),
        scratch_types=[pltpu.VMEM((2, SC_ROWS), I32), pltpu.VMEM((2, SC_ROWS, d), src.dtype),
                       pltpu.SemaphoreType.DMA((2,)), pltpu.SemaphoreType.DMA((2,))],
    )
    def gather_rows(src_hbm, idx_hbm, out_hbm, idx_v, rows_v, gsem, wsem):
        base = _sc_worker_id() * per_w

        def row_gather(b):
            return pltpu.make_async_copy(src_hbm.at[idx_v.at[b]], rows_v.at[b], gsem.at[b])

        def write_back(b, off):
            return pltpu.make_async_copy(rows_v.at[b], out_hbm.at[pl.ds(off, SC_ROWS)], wsem.at[b])

        @pl.loop(0, n_chunks, step=2)
        def _(i):
            for b in range(2):
                off = base + (i + b) * SC_ROWS

                @pl.when(i > 0)
                def _():
                    write_back(b, off).wait()

                pltpu.sync_copy(idx_hbm.at[pl.ds(off, SC_ROWS)], idx_v.at[b])
                row_gather(b).start()
            for b in range(2):
                row_gather(b).wait()
                write_back(b, base + (i + b) * SC_ROWS).start()

        for b in range(2):
            write_back(b, base).wait()

    return gather_rows(src, idx)


def _gmm_kernel(tile_ref, grp_ref, lo_ref, hi_ref, first_ref, newgrp_ref, x_ref, win_ref, wdn_ref, o_ref,
                win_bf, wdn_bf):
    v = pl.program_id(0)
    lo, hi = lo_ref[v], hi_ref[v]

    @pl.when(newgrp_ref[v] == 1)
    def _():
        win_bf[...] = win_ref[...].astype(BF16)
        wdn_bf[...] = wdn_ref[...].astype(BF16)

    @pl.when(hi > lo)
    def _():
        half = x_ref.shape[1]
        x_lo, x_hi = _unpack_bf16_pair(x_ref[...])
        gu = (jnp.dot(x_lo, win_bf[:half, :], preferred_element_type=F32)
              + jnp.dot(x_hi, win_bf[half:, :], preferred_element_type=F32))
        act = (_silu(gu[:, :EXPERT_FF]) * gu[:, EXPERT_FF:]).astype(BF16)
        y = _pack_bf16_pair(jnp.dot(act, wdn_bf[...], preferred_element_type=F32))
        rows = lax.broadcasted_iota(I32, (GMM_ROWS, 1), 0)
        mine = (rows >= lo) & (rows < hi)

        @pl.when(first_ref[v] == 1)
        def _():
            o_ref[...] = jnp.where(mine, y, 0)

        @pl.when(first_ref[v] == 0)
        def _():
            o_ref[...] = jnp.where(mine, y, o_ref[...])


def _gmm(meta, xs, w_in_all, w_dn_all, layer):
    p, half = xs.shape
    d = 2 * half
    visits = meta[0].shape[0]
    TM = GMM_ROWS
    grid_spec = pltpu.PrefetchScalarGridSpec(
        num_scalar_prefetch=len(meta),
        grid=(visits,),
        in_specs=[pl.BlockSpec((TM, half), lambda v, tile, grp, *_: (tile[v], 0)),
                  pl.BlockSpec((None, None, d, 2 * EXPERT_FF), lambda v, tile, grp, *_: (layer, grp[v], 0, 0)),
                  pl.BlockSpec((None, None, EXPERT_FF, d), lambda v, tile, grp, *_: (layer, grp[v], 0, 0))],
        out_specs=pl.BlockSpec((TM, half), lambda v, tile, grp, *_: (tile[v], 0)),
        scratch_shapes=[pltpu.VMEM((d, 2 * EXPERT_FF), BF16), pltpu.VMEM((EXPERT_FF, d), BF16)],
    )
    return pl.pallas_call(
        _gmm_kernel,
        grid_spec=grid_spec,
        out_shape=jax.ShapeDtypeStruct((p, half), I32),
        compiler_params=_cparams("arbitrary"),
        name="gmm",
    )(*meta, xs, w_in_all, w_dn_all)


def _gmm_schedule(counts, n_rows):
    TM = GMM_ROWS
    E = counts.shape[0]
    max_visits = n_rows // TM + E - 1
    ends = jnp.cumsum(counts)
    starts = ends - counts
    first_tile = starts // TM
    n_vis = jnp.where(counts > 0, jnp.maximum(ends - 1, 0) // TM - first_tile + 1, 0)
    vis_end = jnp.cumsum(n_vis)
    vis_start = vis_end - n_vis
    n_real = vis_end[-1]
    v = jnp.minimum(jnp.arange(max_visits, dtype=I32), jnp.maximum(n_real - 1, 0))
    real = jnp.arange(max_visits, dtype=I32) < n_real
    grp = jnp.minimum(jnp.sum((vis_end[None, :] <= v[:, None]).astype(I32), axis=1), E - 1)
    onehot = grp[:, None] == jnp.arange(E, dtype=I32)[None, :]

    def take(table):
        return jnp.sum(jnp.where(onehot, table[None, :], 0), axis=1)

    tile = take(first_tile) + (v - take(vis_start))
    lo = jnp.where(real, jnp.maximum(take(starts), tile * TM) - tile * TM, 0)
    hi = jnp.where(real, jnp.minimum(take(ends), (tile + 1) * TM) - tile * TM, 0)
    prev = lambda a: jnp.concatenate([jnp.full((1,), -1, I32), a[:-1]])
    first = real & (tile != prev(tile))
    newgrp = real & (grp != prev(grp))
    return tuple(a.astype(I32) for a in (tile, grp, lo, hi, first, newgrp))


def _combine_kernel(wgt_ref, h_ref, yg_ref, wsi_ref, wsd_ref, g_ref, b_ref, o_ref):
    h = h_ref[...]
    gu = jnp.dot(h.astype(BF16), wsi_ref[...], preferred_element_type=F32)
    act = (_silu(gu[:, :SHARED_FF]) * gu[:, SHARED_FF:]).astype(BF16)
    acc = ALPHA * h + jnp.dot(act, wsd_ref[...], preferred_element_type=F32)
    wgt = wgt_ref[...]
    half = yg_ref.shape[2]
    r_lo = jnp.zeros((h.shape[0], half), F32)
    r_hi = jnp.zeros((h.shape[0], half), F32)
    for k in range(TOP_K):
        y_lo, y_hi = _unpack_bf16_pair(yg_ref[k])
        r_lo = r_lo + wgt[:, k:k + 1] * y_lo.astype(F32)
        r_hi = r_hi + wgt[:, k:k + 1] * y_hi.astype(F32)
    acc = acc + jnp.concatenate([r_lo, r_hi], axis=1)
    o_ref[...] = _layernorm(acc, g_ref[...], b_ref[...])


def _combine(wgt_tk, h2, yg, w_si, w_sd, g, b):
    n, d = h2.shape
    TT = COMBINE_TOK
    const2 = lambda shape: pl.BlockSpec(shape, lambda i: (0, 0))
    return pl.pallas_call(
        _combine_kernel,
        grid=(n // TT,),
        in_specs=[pl.BlockSpec((TT, TOP_K), lambda i: (i, 0)),
                  pl.BlockSpec((TT, d), lambda i: (i, 0)),
                  pl.BlockSpec((TOP_K, TT, d // 2), lambda i: (0, i, 0)),
                  const2((d, 2 * SHARED_FF)), const2((SHARED_FF, d)), const2((1, d)), const2((1, d))],
        out_specs=pl.BlockSpec((TT, d), lambda i: (i, 0)),
        out_shape=jax.ShapeDtypeStruct((n, d), F32),
        compiler_params=_cparams("parallel"),
        name="combine",
    )(wgt_tk, h2, yg, w_si, w_sd, g, b)


def _moe_ln(h2, h2_packed, w_router, r_bias, w_e_in_all, w_e_dn_all, layer, w_s_in, w_s_dn, ln_g, ln_b):
    n, d = h2.shape
    wt = w_router.T
    wt_hi = wt.astype(BF16)
    wt_lo = (wt - wt_hi.astype(F32)).astype(BF16)
    bias_col = jnp.broadcast_to(r_bias.astype(F32)[:, None], (N_EXPERTS, LANES))
    eidx, wgt, rank, cnt = _router(h2, wt_hi, wt_lo, bias_col)
    counts = cnt[:, 0].astype(I32)
    offs = jnp.cumsum(counts) - counts
    offs_col = jnp.broadcast_to(offs.astype(F32)[:, None], (N_EXPERTS, LANES))
    dest = _dest_rows(eidx, rank, offs_col)
    dest_chunks = dest.reshape(TOP_K, n // SC_ROWS, SC_ROWS).transpose(1, 0, 2)
    xs = _dispatch(dest_chunks, h2_packed)
    ys = _gmm(_gmm_schedule(counts, n * TOP_K), xs, w_e_in_all, w_e_dn_all, layer)
    yg = _gather_rows(ys, dest.reshape(-1)).reshape(TOP_K, n, d // 2)
    return _combine(wgt.T, h2, yg, w_s_in.astype(BF16), w_s_dn.astype(BF16), ln_g[None, :], ln_b[None, :])


def _pad_cols(w, width):
    return jnp.pad(w, ((0, 0), (0, width - w.shape[1])))


def kernel(x, w_in_ab, w_gla_gate_up, b_gla_gate, hgrn_norm_g, gla_norm_g, w_out_ab, hgrn_lb_logits, w_in_c, conv_c, b_if_c, w_out_c, w_router, router_bias, w_exp_in, w_exp_down, w_sh_in, w_sh_down, ln_mix_g, ln_mix_b, ln_ffn_g, ln_ffn_b):
    B, T, D = x.shape
    lower_bounds = jnp.cumsum(jax.nn.softmax(hgrn_lb_logits.astype(F32), axis=0), axis=0)
    h = x.reshape(B * T, D)
    for l in range(DEPTH):
        j = l // 2
        if l % 2 == 0:
            ab_pad = AB_WIDTH - GLA_GATE_RANK + LANES
            proj = _linear(h, _pad_cols(w_in_ab[j], ab_pad).astype(BF16))
            w_up = jnp.pad(w_gla_gate_up[j], ((0, LANES - GLA_GATE_RANK), (0, 0))).astype(BF16)
            y = _hgrn_gla_mixer(proj, B, T, lower_bounds[l][None, :], w_up, b_gla_gate[j][None, :],
                                hgrn_norm_g[j][None, :], gla_norm_g[j][None, :])
            w_out = w_out_ab[j]
        else:
            c_pad = C_WIDTH - 2 * MLSTM_HEADS + LANES
            proj = _linear(h, _pad_cols(w_in_c[j], c_pad).astype(BF16))
            b_if = jnp.pad(b_if_c[j].astype(F32), (0, LANES - 2 * MLSTM_HEADS))[None, :]
            y = _mlstm_mixer(proj, B, T, conv_c[j], b_if)
            w_out = w_out_c[j]
        h, h_packed = _outproj_ln(h, y, w_out.astype(BF16), ln_mix_g[l][None, :], ln_mix_b[l][None, :])
        h = _moe_ln(h, h_packed, w_router[l], router_bias[l], w_exp_in, w_exp_down, l, w_sh_in[l], w_sh_down[l],
                    ln_ffn_g[l], ln_ffn_b[l])
    return h.reshape(B, T, D)
```

```python
import functools

import jax
import jax.numpy as jnp
from jax import lax
from jax.experimental import pallas as pl
from jax.experimental.pallas import tpu as pltpu
from jax.experimental.pallas import tpu_sc as plsc

F32 = jnp.float32
BF16 = jnp.bfloat16
I32 = jnp.int32

D_MODEL = 1024
DEPTH = 2
HGRN_HEADS = 4
HGRN_HEAD_DIM = 128
HGRN_WIDTH = HGRN_HEADS * HGRN_HEAD_DIM
GLA_HEADS = 4
GLA_KEY_DIM = 64
GLA_VAL_DIM = 128
GLA_KEY_WIDTH = GLA_HEADS * GLA_KEY_DIM
GLA_VAL_WIDTH = GLA_HEADS * GLA_VAL_DIM
GLA_GATE_RANK = 16
GLA_GATE_NORMALIZER = 16.0
LIN_CHUNK = 64
LIN_SUB = 16
AB_WIDTH = 4 * HGRN_WIDTH + 2 * GLA_KEY_WIDTH + 2 * GLA_VAL_WIDTH + GLA_GATE_RANK
MLSTM_HEADS = 4
MLSTM_QK_DIM = 128
MLSTM_V_DIM = 256
MLSTM_QK_WIDTH = MLSTM_HEADS * MLSTM_QK_DIM
MLSTM_V_WIDTH = MLSTM_HEADS * MLSTM_V_DIM
MLSTM_CONV = 4
MLSTM_CHUNK = 128
C_WIDTH = 2 * MLSTM_QK_WIDTH + 2 * MLSTM_V_WIDTH + 2 * MLSTM_HEADS
N_EXPERTS = 256
TOP_K = 8
N_GROUPS = 8
TOPK_GROUPS = 4
GROUP_SIZE = N_EXPERTS // N_GROUPS
EXPERT_FF = 256
SHARED_FF = 256
ROUTED_SCALE = 2.5
ALPHA = (2 * DEPTH) ** 0.25
LN_EPS = 1e-5
RMS_EPS = 1e-6

LANES = 128
VMEM_LIMIT = 56 * 1024 * 1024

MIX_ROWS = 128
ROUTE_TOK = 128
COMBINE_TOK = 256
GMM_ROWS = 512
SC_CORES, SC_SUBCORES = 2, 16
SC_WORKERS = SC_CORES * SC_SUBCORES
SC_ROWS = 64


def _cparams(*sem):
    return pltpu.CompilerParams(dimension_semantics=sem, vmem_limit_bytes=VMEM_LIMIT)


def _sigmoid(x):
    return 1.0 / (1.0 + jnp.exp(-x))


def _silu(x):
    return x * _sigmoid(x)


def _log_sigmoid(x):
    return jnp.minimum(x, 0.0) - jnp.log(1.0 + jnp.exp(-jnp.abs(x)))


def _split3(x):
    hi = x.astype(BF16)
    r = x - hi.astype(F32)
    mid = r.astype(BF16)
    lo = (r - mid.astype(F32)).astype(BF16)
    return hi, mid, lo


def _tri_dot_left(tri, x):
    return sum(jnp.dot(tri, p, preferred_element_type=F32) for p in _split3(x))


def _tri_dot_right(x, tri):
    return sum(jnp.dot(p, tri, preferred_element_type=F32) for p in _split3(x))


def _dot_nt(a, b):
    return lax.dot_general(a, b, (((1,), (1,)), ((), ())), preferred_element_type=F32)


def _dot_tn(a, b):
    return lax.dot_general(a, b, (((0,), (0,)), ((), ())), preferred_element_type=F32)


HI16 = -65536


def _pack_bf16_pair(x):
    c = x.shape[1] // 2
    bits = lambda v: lax.bitcast_convert_type(v.astype(BF16).astype(F32), I32)
    return (bits(x[:, c:]) & HI16) | lax.shift_right_logical(bits(x[:, :c]), 16)


def _unpack_bf16_pair(w):
    lo = lax.bitcast_convert_type(lax.shift_left(w, 16), F32)
    hi = lax.bitcast_convert_type(w & HI16, F32)
    return lo.astype(BF16), hi.astype(BF16)


def _layernorm(x, g, b):
    mu = jnp.mean(x, axis=-1, keepdims=True)
    xc = x - mu
    var = jnp.mean(xc * xc, axis=-1, keepdims=True)
    return xc * lax.rsqrt(var + LN_EPS) * g + b


N_UNITS = HGRN_HEADS + GLA_HEADS // 2
HALF_LANES = LANES // 2
LOG2E = 1.4426950408889634


def _pair_selector():
    assert LIN_CHUNK == HALF_LANES == GLA_KEY_DIM
    r = jnp.arange(LIN_SUB * LANES)[:, None]
    c = jnp.arange(LANES)[None, :]
    same_slot = (r // LANES) == (c % LIN_SUB)
    same_half = ((r % LANES) < HALF_LANES) == (c < HALF_LANES)
    return (same_slot & same_half).astype(BF16)


def _pairwise_tiles(q, k, b, lhs_ref, row0):
    c = LIN_SUB
    b2 = b * LOG2E
    for blk in range(LIN_CHUNK // c):
        lo = blk * c
        qb, kb, bb = q[lo:lo + c], k[lo:lo + c], b2[lo:lo + c]
        for j in range(c):
            e = jnp.exp2(jnp.minimum(bb - bb[j:j + 1], 0.0))
            lhs_ref[row0 + lo:row0 + lo + c, j * LANES:(j + 1) * LANES] = (qb * kb[j:j + 1] * e).astype(BF16)


def _unit_chunk(q, k, b, vs, r, st_ref, masks):
    L, c = LIN_CHUNK, LIN_SUB
    lane_lo, col_mod, same_blk_causal = masks
    heads = [lane_lo, ~lane_lo] if len(vs) == 2 else [None]

    def pick(x, m):
        return x if m is None else jnp.where(m, x, jnp.zeros_like(x))

    g = b[L - 1:L, :]
    st = st_ref[...]
    st_b = st.astype(BF16)
    qx = (q * jnp.exp(b)).astype(BF16)
    outs = [_dot_nt(pick(qx, m), st_b) for m in heads]
    k_end = (k * jnp.exp(g - b)).astype(BF16)
    upd = [_dot_tn(v.astype(BF16), k_end) for v in vs]
    st_ref[...] = st * jnp.exp(g) + (upd[0] if len(vs) == 1 else jnp.where(lane_lo, upd[0], upd[1]))

    off_rows = [jnp.zeros((c, LANES), F32)]
    for blk in range(1, L // c):
        lo = blk * c
        ref = b[lo - 1:lo]
        q_in = (q[lo:lo + c] * jnp.exp(b[lo:lo + c] - ref)).astype(BF16)
        k_in = (k * jnp.exp(jnp.minimum(ref - b, 0.0))).astype(BF16)
        if len(vs) == 2:
            k_cat = jnp.concatenate([pick(k_in, heads[0]), pick(k_in, heads[1])], axis=0)
        else:
            k_cat = jnp.concatenate([k_in, jnp.zeros_like(k_in)], axis=0)
        off_rows.append(jnp.where(col_mod < lo, _dot_nt(q_in, k_cat), 0.0))
    a = jnp.where(same_blk_causal, r, jnp.concatenate(off_rows, axis=0)).astype(BF16)
    v_cat = jnp.concatenate([vs[0], vs[-1]], axis=0).astype(BF16)
    return [o + jnp.dot(pick(a, m), v_cat, preferred_element_type=F32) for o, m in zip(outs, heads)]


def _rms_gate(o, g, gate):
    o = o * lax.rsqrt(jnp.mean(o * o, axis=-1, keepdims=True) + RMS_EPS)
    return o * g * _silu(gate)


class _Cols:
    def __init__(self, ref, off, width):
        self.ref, self.off, self.width = ref, off, width

    def __getitem__(self, idx):
        rows, cols = (slice(None), slice(None)) if idx is Ellipsis else idx
        lo = self.off + (cols.start or 0)
        hi = self.off + (self.width if cols.stop is None else cols.stop)
        return self.ref[rows, lo:hi]


def _project_residual_ln(x_ref, y_ref, wout_ref, g_ref, b_ref, h_ref, hp_ref):
    mix = jnp.dot(y_ref[...], wout_ref[...], preferred_element_type=F32)
    h = _layernorm(ALPHA * x_ref[...] + mix, g_ref[...], b_ref[...])
    h_ref[...] = h
    hp_ref[...] = _pack_bf16_pair(h)


def _hgrn_gla_kernel(x_ref, win_ref, lb_ref, wup_ref, bgk_ref, hng_ref, gng_ref, sel_ref, wout_ref, lng_ref, lnb_ref,
                     h_ref, hp_ref, proj_ref, y_ref, lhs_ref, *st_refs):
    @pl.when(pl.program_id(1) == 0)
    def _():
        for r in st_refs:
            r[...] = jnp.zeros_like(r)

    proj_ref[...] = jnp.dot(x_ref[...].astype(BF16), win_ref[...], preferred_element_type=F32)
    W, KW, VW = HGRN_WIDTH, GLA_KEY_WIDTH, GLA_VAL_WIDTH
    hq_ref, hf_ref, hi_ref, hg_ref = (_Cols(proj_ref, i * W, W) for i in range(4))
    gq_ref, gk_ref = _Cols(proj_ref, 4 * W, KW), _Cols(proj_ref, 4 * W + KW, KW)
    gv_ref, gg_ref = _Cols(proj_ref, 4 * W + 2 * KW, VW), _Cols(proj_ref, 4 * W + 2 * KW + VW, VW)
    glr_ref = _Cols(proj_ref, AB_WIDTH - GLA_GATE_RANK, LANES)

    L = LIN_CHUNK
    n_chunks = MIX_ROWS // L
    r_i = lax.broadcasted_iota(I32, (L, LANES), 0)
    c_i = lax.broadcasted_iota(I32, (L, LANES), 1)
    col_mod = jnp.where(c_i < HALF_LANES, c_i, c_i - HALF_LANES)
    same_blk_causal = (col_mod // LIN_SUB == r_i // LIN_SUB) & (col_mod <= r_i)
    c_sub = lax.broadcasted_iota(I32, (LIN_SUB, LANES), 1)
    col_mod_sub = jnp.where(c_sub < HALF_LANES, c_sub, c_sub - HALF_LANES)
    lane_lo = lax.broadcasted_iota(I32, (1, LANES), 1) < HALF_LANES
    masks = (lane_lo, col_mod_sub, same_blk_causal)
    tril = (lax.broadcasted_iota(I32, (L, L), 0) >= lax.broadcasted_iota(I32, (L, L), 1)).astype(BF16)
    lb = lb_ref[...]

    units = []
    for ck in range(n_chunks):
        rs = slice(ck * L, (ck + 1) * L)
        u = jnp.dot(glr_ref[rs, :].astype(BF16), wup_ref[...], preferred_element_type=F32) + bgk_ref[...]
        la_g = _log_sigmoid(u) * (1.0 / GLA_GATE_NORMALIZER)
        qs, ks, las = [], [], []
        for h in range(HGRN_HEADS):
            cs = slice(h * LANES, (h + 1) * LANES)
            z = hf_ref[rs, cs]
            lbh = lb[:, cs]
            qs.append(_silu(hq_ref[rs, cs]))
            ks.append((1.0 - lbh) * _sigmoid(-z))
            las.append(jnp.log(lbh + (1.0 - lbh) * _sigmoid(z)))
        for p in range(GLA_HEADS // 2):
            cs = slice(p * LANES, (p + 1) * LANES)
            qs.append(gq_ref[rs, cs] * (GLA_KEY_DIM ** -0.5))
            ks.append(gk_ref[rs, cs])
            las.append(la_g[:, cs])
        b_all = _tri_dot_left(tril, jnp.concatenate(las, axis=1))
        for un in range(N_UNITS):
            b = b_all[:, un * LANES:(un + 1) * LANES]
            _pairwise_tiles(qs[un], ks[un], b, lhs_ref, (ck * N_UNITS + un) * L)
            units.append((qs[un], ks[un], b))

    r_all = jnp.dot(lhs_ref[...], sel_ref[...], preferred_element_type=F32)

    for ck in range(n_chunks):
        rs = slice(ck * L, (ck + 1) * L)
        for un in range(N_UNITS):
            uc = ck * N_UNITS + un
            q, k, b = units[uc]
            r = r_all[uc * L:(uc + 1) * L]
            if un < HGRN_HEADS:
                cs = slice(un * LANES, (un + 1) * LANES)
                (o,) = _unit_chunk(q, k, b, [hi_ref[rs, cs]], r, st_refs[un], masks)
                y_ref[rs, cs] = _rms_gate(o, hng_ref[:, cs], hg_ref[rs, cs]).astype(y_ref.dtype)
            else:
                p = un - HGRN_HEADS
                vcs = [slice((2 * p + i) * GLA_VAL_DIM, (2 * p + i + 1) * GLA_VAL_DIM) for i in range(2)]
                outs = _unit_chunk(q, k, b, [gv_ref[rs, vc] for vc in vcs], r, st_refs[un], masks)
                for o, vc in zip(outs, vcs):
                    ys = slice(HGRN_WIDTH + vc.start, HGRN_WIDTH + vc.stop)
                    y_ref[rs, ys] = _rms_gate(o, gng_ref[:, vc], gg_ref[rs, vc]).astype(y_ref.dtype)

    _project_residual_ln(x_ref, y_ref, wout_ref, lng_ref, lnb_ref, h_ref, hp_ref)


def _mixer_layer_call(kernel_fn, name, x2, batch, seq, rows, consts, scratch_shapes):
    n, d = x2.shape
    steps = seq // rows
    row_spec = lambda width: pl.BlockSpec((rows, width), lambda b, i: (b * steps + i, 0))
    const_spec = lambda a: pl.BlockSpec(a.shape, lambda b, i: (0,) * a.ndim)
    return pl.pallas_call(
        kernel_fn,
        grid=(batch, steps),
        in_specs=[row_spec(d)] + [const_spec(a) for a in consts],
        out_specs=[row_spec(d), row_spec(d // 2)],
        out_shape=[jax.ShapeDtypeStruct((n, d), F32), jax.ShapeDtypeStruct((n, d // 2), I32)],
        scratch_shapes=scratch_shapes,
        compiler_params=_cparams("parallel", "arbitrary"),
        name=name,
    )(x2, *consts)


def _hgrn_gla_layer(x2, batch, seq, w_in, lb, w_up, b_gk, hgrn_g, gla_g, w_out, ln_g, ln_b):
    R = MIX_ROWS
    n_uc = (R // LIN_CHUNK) * N_UNITS
    scratch = ([pltpu.VMEM((R, w_in.shape[1]), F32), pltpu.VMEM((R, w_out.shape[0]), BF16),
                pltpu.VMEM((n_uc * LIN_CHUNK, LIN_SUB * LANES), BF16)]
               + [pltpu.VMEM((GLA_VAL_DIM, LANES), F32)] * N_UNITS)
    consts = (w_in, lb, w_up, b_gk, hgrn_g, gla_g, _pair_selector(), w_out, ln_g, ln_b)
    return _mixer_layer_call(_hgrn_gla_kernel, "hgrn_gla", x2, batch, seq, R, consts, scratch)


def _mlstm_kernel(x_ref, win_ref, cw_ref, bif_ref, wout_ref, lng_ref, lnb_ref, h_ref, hp_ref,
                  proj_ref, y_ref, ext_ref, *state_refs):
    L = MLSTM_CHUNK
    PAD = 8
    cn_refs, m_refs = state_refs[:MLSTM_HEADS], state_refs[MLSTM_HEADS:]

    @pl.when(pl.program_id(1) == 0)
    def _():
        ext_ref[0:PAD, :] = jnp.zeros((PAD, ext_ref.shape[1]), F32)
        for r in state_refs:
            r[...] = jnp.zeros_like(r)

    proj_ref[...] = jnp.dot(x_ref[...].astype(BF16), win_ref[...], preferred_element_type=F32)
    QK2, VW = 2 * MLSTM_QK_WIDTH, MLSTM_V_WIDTH
    qk_ref, v_ref = _Cols(proj_ref, 0, QK2), _Cols(proj_ref, QK2, VW)
    og_ref, gt_ref = _Cols(proj_ref, QK2 + VW, VW), _Cols(proj_ref, QK2 + 2 * VW, LANES)

    ext_ref[PAD:PAD + L, :] = qk_ref[...]
    conv = jnp.zeros((L, ext_ref.shape[1]), F32)
    for w in range(MLSTM_CONV):
        conv = conv + cw_ref[w:w + 1, :] * ext_ref[pl.ds(PAD - (MLSTM_CONV - 1) + w, L), :]
    ext_ref[0:PAD, :] = ext_ref[L:L + PAD, :]
    qk = _silu(conv)

    gt = gt_ref[...] + bif_ref[...]
    gt_t = gt.T
    r_i = lax.broadcasted_iota(I32, (L, L), 0)
    c_i = lax.broadcasted_iota(I32, (L, L), 1)
    causal = c_i <= r_i
    tril = causal.astype(BF16)
    triu = (r_i <= c_i).astype(BF16)
    b_cols = _tri_dot_left(tril, _log_sigmoid(gt))
    b_rows = _tri_dot_right(_log_sigmoid(gt_t), triu)
    ones_col = (lax.broadcasted_iota(I32, (L, LANES), 1) == 0).astype(F32)

    H = MLSTM_HEADS
    for h in range(H):
        q = qk[:, h * MLSTM_QK_DIM:(h + 1) * MLSTM_QK_DIM].astype(BF16)
        kf = qk[:, MLSTM_QK_WIDTH + h * MLSTM_QK_DIM:MLSTM_QK_WIDTH + (h + 1) * MLSTM_QK_DIM] * (MLSTM_QK_DIM ** -0.5)
        vs = slice(h * MLSTM_V_DIM, (h + 1) * MLSTM_V_DIM)
        v_ext = jnp.concatenate([v_ref[:, vs], ones_col], axis=1).astype(BF16)
        bc, br = b_cols[:, H + h:H + h + 1], b_rows[H + h:H + h + 1, :]
        ic, ir = gt[:, h:h + 1], gt_t[h:h + 1, :]
        m_prev = m_refs[h][0:1, 0:1]
        g = bc[L - 1:L, :]
        dmat = jnp.where(causal, bc - br + ir, -jnp.inf)
        inter = bc + m_prev
        m_j = jnp.maximum(inter, jnp.max(dmat, axis=-1, keepdims=True))
        s = _dot_nt(q, kf.astype(BF16)) * jnp.exp(dmat - m_j)
        w_inter = jnp.exp(inter - m_j)
        cn = cn_refs[h][...]
        nd = (jnp.dot(s.astype(BF16), v_ext, preferred_element_type=F32)
              + w_inter * jnp.dot(q, cn.astype(BF16), preferred_element_type=F32))
        num, den = nd[:, :MLSTM_V_DIM], nd[:, MLSTM_V_DIM:MLSTM_V_DIM + 1]
        hid = num / jnp.maximum(jnp.abs(den), jnp.exp(-m_j))
        u = g - bc + ic
        m_new = jnp.maximum(g + m_prev, jnp.max(u, axis=0, keepdims=True))
        wk = (kf * jnp.exp(u - m_new)).astype(BF16)
        cn_refs[h][...] = jnp.exp(g + m_prev - m_new) * cn + _dot_tn(wk, v_ext)
        m_refs[h][...] = jnp.broadcast_to(m_new, m_refs[h].shape)
        y_ref[:, vs] = (hid * _sigmoid(og_ref[:, vs])).astype(y_ref.dtype)

    _project_residual_ln(x_ref, y_ref, wout_ref, lng_ref, lnb_ref, h_ref, hp_ref)


def _mlstm_layer(x2, batch, seq, w_in, conv_w, b_if, w_out, ln_g, ln_b):
    R = MLSTM_CHUNK
    scratch = ([pltpu.VMEM((R, w_in.shape[1]), F32), pltpu.VMEM((R, w_out.shape[0]), BF16),
                pltpu.VMEM((R + 8, 2 * MLSTM_QK_WIDTH), F32)]
               + [pltpu.VMEM((MLSTM_QK_DIM, MLSTM_V_DIM + LANES), F32)] * MLSTM_HEADS
               + [pltpu.VMEM((8, LANES), F32)] * MLSTM_HEADS)
    consts = (w_in, conv_w, b_if, w_out, ln_g, ln_b)
    return _mixer_layer_call(_mlstm_kernel, "mlstm", x2, batch, seq, R, consts, scratch)


def _router_kernel(h_ref, whi_ref, wlo_ref, bias_ref, eidx_ref, wgt_ref, rank_ref, cnt_ref, carry_ref):
    TT, E, G, GS = ROUTE_TOK, N_EXPERTS, N_GROUPS, GROUP_SIZE

    @pl.when(pl.program_id(0) == 0)
    def _():
        carry_ref[...] = jnp.zeros_like(carry_ref)

    h = h_ref[...]
    h_hi = h.astype(BF16)
    h_lo = (h - h_hi.astype(F32)).astype(BF16)
    logits = _dot_nt(whi_ref[...], h_hi) + _dot_nt(whi_ref[...], h_lo) + _dot_nt(wlo_ref[...], h_hi)
    scores = _sigmoid(logits)
    biased = scores + bias_ref[:, 0:1]
    neg = -jnp.inf

    io_g = lax.broadcasted_iota(I32, (GS, TT), 0)
    io8 = lax.broadcasted_iota(I32, (G, TT), 0)
    gs = jnp.zeros((G, TT), F32)
    for g in range(G):
        blk = biased[g * GS:(g + 1) * GS, :]
        m1 = jnp.max(blk, axis=0, keepdims=True)
        i1 = jnp.min(jnp.where(blk == m1, io_g, GS), axis=0, keepdims=True)
        m2 = jnp.max(jnp.where(io_g == i1, neg, blk), axis=0, keepdims=True)
        gs = jnp.where(io8 == g, m1 + m2, gs)
    gsel = jnp.zeros((G, TT), F32)
    for _ in range(TOPK_GROUPS):
        m = jnp.max(gs, axis=0, keepdims=True)
        idx = jnp.min(jnp.where(gs == m, io8, G), axis=0, keepdims=True)
        hit = io8 == idx
        gsel = jnp.where(hit, 1.0, gsel)
        gs = jnp.where(hit, neg, gs)
    sel = jnp.concatenate([jnp.broadcast_to(gsel[g:g + 1, :], (GS, TT)) for g in range(G)], axis=0)
    masked = jnp.where(sel > 0.0, biased, neg)

    io_e = lax.broadcasted_iota(I32, (E, TT), 0)
    eidx = jnp.zeros((TOP_K, TT), I32)
    wsel = jnp.zeros((TOP_K, TT), F32)
    chosen = jnp.zeros((E, TT), F32)
    for k in range(TOP_K):
        m = jnp.max(masked, axis=0, keepdims=True)
        idx = jnp.min(jnp.where(masked == m, io_e, E), axis=0, keepdims=True)
        hit = io_e == idx
        sc = jnp.sum(jnp.where(hit, scores, 0.0), axis=0, keepdims=True)
        eidx = jnp.where(io8 == k, idx, eidx)
        wsel = jnp.where(io8 == k, sc, wsel)
        chosen = jnp.where(hit, 1.0, chosen)
        masked = jnp.where(hit, neg, masked)
    wgt_ref[...] = wsel / jnp.sum(wsel, axis=0, keepdims=True) * ROUTED_SCALE
    eidx_ref[...] = eidx

    triu = (lax.broadcasted_iota(I32, (TT, TT), 0) <= lax.broadcasted_iota(I32, (TT, TT), 1)).astype(BF16)
    cum = jnp.dot(chosen.astype(BF16), triu, preferred_element_type=F32)
    carry = carry_ref[:, 0:1]
    before = cum - chosen + carry
    rank = jnp.zeros((TOP_K, TT), F32)
    for k in range(TOP_K):
        r = jnp.sum(jnp.where(io_e == eidx[k:k + 1, :], before, 0.0), axis=0, keepdims=True)
        rank = jnp.where(io8 == k, r, rank)
    rank_ref[...] = rank.astype(I32)
    total = carry + cum[:, TT - 1:TT]
    carry_ref[...] = jnp.broadcast_to(total, carry_ref.shape)
    cnt_ref[...] = jnp.broadcast_to(total, cnt_ref.shape)


def _router(h2, wt_hi, wt_lo, bias_col):
    n, d = h2.shape
    TT, E = ROUTE_TOK, N_EXPERTS
    tok_spec = pl.BlockSpec((TOP_K, TT), lambda i: (0, i))
    return pl.pallas_call(
        _router_kernel,
        grid=(n // TT,),
        in_specs=[pl.BlockSpec((TT, d), lambda i: (i, 0)),
                  pl.BlockSpec((E, d), lambda i: (0, 0)),
                  pl.BlockSpec((E, d), lambda i: (0, 0)),
                  pl.BlockSpec((E, LANES), lambda i: (0, 0))],
        out_specs=[tok_spec, tok_spec, tok_spec, pl.BlockSpec((E, LANES), lambda i: (0, 0))],
        out_shape=[jax.ShapeDtypeStruct((TOP_K, n), I32), jax.ShapeDtypeStruct((TOP_K, n), F32),
                   jax.ShapeDtypeStruct((TOP_K, n), I32), jax.ShapeDtypeStruct((E, LANES), F32)],
        scratch_shapes=[pltpu.VMEM((E, LANES), F32)],
        compiler_params=_cparams("arbitrary"),
        name="router",
    )(h2, wt_hi, wt_lo, bias_col)


def _dest_kernel(eidx_ref, rank_ref, offs_ref, dest_ref):
    TT, E = eidx_ref.shape[1], N_EXPERTS
    io_e = lax.broadcasted_iota(I32, (E, TT), 0)
    io8 = lax.broadcasted_iota(I32, (TOP_K, TT), 0)
    offs = offs_ref[:, 0:1]
    base = jnp.zeros((TOP_K, TT), F32)
    for k in range(TOP_K):
        r = jnp.sum(jnp.where(io_e == eidx_ref[k:k + 1, :], offs, 0.0), axis=0, keepdims=True)
        base = jnp.where(io8 == k, r, base)
    dest_ref[...] = base.astype(I32) + rank_ref[...]


def _dest_rows(eidx, rank, offs_col):
    n = eidx.shape[1]
    TT = 512
    spec = pl.BlockSpec((TOP_K, TT), lambda i: (0, i))
    return pl.pallas_call(
        _dest_kernel,
        grid=(n // TT,),
        in_specs=[spec, spec, pl.BlockSpec((N_EXPERTS, LANES), lambda i: (0, 0))],
        out_specs=spec,
        out_shape=jax.ShapeDtypeStruct((TOP_K, n), I32),
        compiler_params=_cparams("parallel"),
        name="dest_rows",
    )(eidx, rank, offs_col)


def _sc_mesh():
    return plsc.VectorSubcoreMesh(core_axis_name="c", subcore_axis_name="s")


def _sc_worker_id():
    return lax.axis_index("s") * SC_CORES + lax.axis_index("c")


def _dispatch(dest_chunks, x2):
    n, d = x2.shape
    n_chunks = n // SC_ROWS // SC_WORKERS

    @functools.partial(
        pl.kernel, mesh=_sc_mesh(),
        out_type=jax.ShapeDtypeStruct((n * TOP_K, d), x2.dtype),
        scratch_types=[pltpu.VMEM((2, TOP_K, SC_ROWS), I32), pltpu.VMEM((2, SC_ROWS, d), x2.dtype),
                       pltpu.SemaphoreType.DMA((2,))],
    )
    def scatter_rows(x_hbm, idx_hbm, out_hbm, idx_v, rows_v, wsem):
        base = _sc_worker_id() * n_chunks

        def row_scatter(b, k):
            return pltpu.make_async_copy(rows_v.at[b], out_hbm.at[idx_v.at[b, k]], wsem.at[b])

        def drain(b):
            for k in range(TOP_K):
                row_scatter(b, k).wait()

        @pl.loop(0, n_chunks, step=2)
        def _(i):
            for b in range(2):
                c = base + i + b

                @pl.when(i > 0)
                def _():
                    drain(b)

                pltpu.sync_copy(idx_hbm.at[c], idx_v.at[b])
                pltpu.sync_copy(x_hbm.at[pl.ds(c * SC_ROWS, SC_ROWS)], rows_v.at[b])
                for k in range(TOP_K):
                    row_scatter(b, k).start()

        for b in range(2):
            drain(b)

    return scatter_rows(x2, dest_chunks)


def _gather_rows(src, idx):
    d = src.shape[1]
    p = idx.shape[0]
    per_w = p // SC_WORKERS
    n_chunks = per_w // SC_ROWS

    @functools.partial(
        pl.kernel, mesh=_sc_mesh(),
        out_type=jax.ShapeDtypeStruct((p, d), src.dtype),
        scratch_types=[pltpu.VMEM((2, SC_ROWS), I32), pltpu.VMEM((2, SC_ROWS, d), src.dtype),
                       pltpu.SemaphoreType.DMA((2,)), pltpu.SemaphoreType.DMA((2,))],
    )
    def gather_rows(src_hbm, idx_hbm, out_hbm, idx_v, rows_v, gsem, wsem):
        base = _sc_worker_id() * per_w

        def row_gather(b):
            return pltpu.make_async_copy(src_hbm.at[idx_v.at[b]], rows_v.at[b], gsem.at[b])

        def write_back(b, off):
            return pltpu.make_async_copy(rows_v.at[b], out_hbm.at[pl.ds(off, SC_ROWS)], wsem.at[b])

        @pl.loop(0, n_chunks, step=2)
        def _(i):
            for b in range(2):
                off = base + (i + b) * SC_ROWS

                @pl.when(i > 0)
                def _():
                    write_back(b, off).wait()

                pltpu.sync_copy(idx_hbm.at[pl.ds(off, SC_ROWS)], idx_v.at[b])
                row_gather(b).start()
            for b in range(2):
                row_gather(b).wait()
                write_back(b, base + (i + b) * SC_ROWS).start()

        for b in range(2):
            write_back(b, base).wait()

    return gather_rows(src, idx)


def _gmm_kernel(tile_ref, grp_ref, lo_ref, hi_ref, first_ref, newgrp_ref, x_ref, win_ref, wdn_ref, o_ref,
                win_bf, wdn_bf):
    v = pl.program_id(0)
    lo, hi = lo_ref[v], hi_ref[v]

    @pl.when(newgrp_ref[v] == 1)
    def _():
        win_bf[...] = win_ref[...].astype(BF16)
        wdn_bf[...] = wdn_ref[...].astype(BF16)

    @pl.when(hi > lo)
    def _():
        half = x_ref.shape[1]
        x_lo, x_hi = _unpack_bf16_pair(x_ref[...])
        gu = (jnp.dot(x_lo, win_bf[:half, :], preferred_element_type=F32)
              + jnp.dot(x_hi, win_bf[half:, :], preferred_element_type=F32))
        act = (_silu(gu[:, :EXPERT_FF]) * gu[:, EXPERT_FF:]).astype(BF16)
        y = _pack_bf16_pair(jnp.dot(act, wdn_bf[...], preferred_element_type=F32))
        rows = lax.broadcasted_iota(I32, (GMM_ROWS, 1), 0)
        mine = (rows >= lo) & (rows < hi)

        @pl.when(first_ref[v] == 1)
        def _():
            o_ref[...] = jnp.where(mine, y, 0)

        @pl.when(first_ref[v] == 0)
        def _():
            o_ref[...] = jnp.where(mine, y, o_ref[...])


def _gmm(meta, xs, w_in_all, w_dn_all, layer):
    p, half = xs.shape
    d = 2 * half
    visits = meta[0].shape[0]
    TM = GMM_ROWS
    grid_spec = pltpu.PrefetchScalarGridSpec(
        num_scalar_prefetch=len(meta),
        grid=(visits,),
        in_specs=[pl.BlockSpec((TM, half), lambda v, tile, grp, *_: (tile[v], 0)),
                  pl.BlockSpec((None, None, d, 2 * EXPERT_FF), lambda v, tile, grp, *_: (layer, grp[v], 0, 0)),
                  pl.BlockSpec((None, None, EXPERT_FF, d), lambda v, tile, grp, *_: (layer, grp[v], 0, 0))],
        out_specs=pl.BlockSpec((TM, half), lambda v, tile, grp, *_: (tile[v], 0)),
        scratch_shapes=[pltpu.VMEM((d, 2 * EXPERT_FF), BF16), pltpu.VMEM((EXPERT_FF, d), BF16)],
    )
    return pl.pallas_call(
        _gmm_kernel,
        grid_spec=grid_spec,
        out_shape=jax.ShapeDtypeStruct((p, half), I32),
        compiler_params=_cparams("arbitrary"),
        name="gmm",
    )(*meta, xs, w_in_all, w_dn_all)


def _gmm_schedule(counts, n_rows):
    TM = GMM_ROWS
    E = counts.shape[0]
    max_visits = n_rows // TM + E - 1
    ends = jnp.cumsum(counts)
    starts = ends - counts
    first_tile = starts // TM
    n_vis = jnp.where(counts > 0, jnp.maximum(ends - 1, 0) // TM - first_tile + 1, 0)
    vis_end = jnp.cumsum(n_vis)
    vis_start = vis_end - n_vis
    n_real = vis_end[-1]
    v = jnp.minimum(jnp.arange(max_visits, dtype=I32), jnp.maximum(n_real - 1, 0))
    real = jnp.arange(max_visits, dtype=I32) < n_real
    grp = jnp.minimum(jnp.sum((vis_end[None, :] <= v[:, None]).astype(I32), axis=1), E - 1)
    onehot = grp[:, None] == jnp.arange(E, dtype=I32)[None, :]

    def take(table):
        return jnp.sum(jnp.where(onehot, table[None, :], 0), axis=1)

    tile = take(first_tile) + (v - take(vis_start))
    lo = jnp.where(real, jnp.maximum(take(starts), tile * TM) - tile * TM, 0)
    hi = jnp.where(real, jnp.minimum(take(ends), (tile + 1) * TM) - tile * TM, 0)
    prev = lambda a: jnp.concatenate([jnp.full((1,), -1, I32), a[:-1]])
    first = real & (tile != prev(tile))
    newgrp = real & (grp != prev(grp))
    return tuple(a.astype(I32) for a in (tile, grp, lo, hi, first, newgrp))


def _combine_kernel(wgt_ref, h_ref, yg_ref, wsi_ref, wsd_ref, g_ref, b_ref, o_ref):
    h = h_ref[...]
    gu = jnp.dot(h.astype(BF16), wsi_ref[...], preferred_element_type=F32)
    act = (_silu(gu[:, :SHARED_FF]) * gu[:, SHARED_FF:]).astype(BF16)
    acc = ALPHA * h + jnp.dot(act, wsd_ref[...], preferred_element_type=F32)
    wgt = wgt_ref[...]
    half = yg_ref.shape[2]
    r_lo = jnp.zeros((h.shape[0], half), F32)
    r_hi = jnp.zeros((h.shape[0], half), F32)
    for k in range(TOP_K):
        y_lo, y_hi = _unpack_bf16_pair(yg_ref[k])
        r_lo = r_lo + wgt[:, k:k + 1] * y_lo.astype(F32)
        r_hi = r_hi + wgt[:, k:k + 1] * y_hi.astype(F32)
    acc = acc + jnp.concatenate([r_lo, r_hi], axis=1)
    o_ref[...] = _layernorm(acc, g_ref[...], b_ref[...])


def _combine(wgt_tk, h2, yg, w_si, w_sd, g, b):
    n, d = h2.shape
    TT = COMBINE_TOK
    const2 = lambda shape: pl.BlockSpec(shape, lambda i: (0, 0))
    return pl.pallas_call(
        _combine_kernel,
        grid=(n // TT,),
        in_specs=[pl.BlockSpec((TT, TOP_K), lambda i: (i, 0)),
                  pl.BlockSpec((TT, d), lambda i: (i, 0)),
                  pl.BlockSpec((TOP_K, TT, d // 2), lambda i: (0, i, 0)),
                  const2((d, 2 * SHARED_FF)), const2((SHARED_FF, d)), const2((1, d)), const2((1, d))],
        out_specs=pl.BlockSpec((TT, d), lambda i: (i, 0)),
        out_shape=jax.ShapeDtypeStruct((n, d), F32),
        compiler_params=_cparams("parallel"),
        name="combine",
    )(wgt_tk, h2, yg, w_si, w_sd, g, b)


def _moe_ln(h2, h2_packed, w_router, r_bias, w_e_in_all, w_e_dn_all, layer, w_s_in, w_s_dn, ln_g, ln_b):
    n, d = h2.shape
    wt = w_router.T
    wt_hi = wt.astype(BF16)
    wt_lo = (wt - wt_hi.astype(F32)).astype(BF16)
    bias_col = jnp.broadcast_to(r_bias.astype(F32)[:, None], (N_EXPERTS, LANES))
    eidx, wgt, rank, cnt = _router(h2, wt_hi, wt_lo, bias_col)
    counts = cnt[:, 0].astype(I32)
    offs = jnp.cumsum(counts) - counts
    offs_col = jnp.broadcast_to(offs.astype(F32)[:, None], (N_EXPERTS, LANES))
    dest = _dest_rows(eidx, rank, offs_col)
    dest_chunks = dest.reshape(TOP_K, n // SC_ROWS, SC_ROWS).transpose(1, 0, 2)
    xs = _dispatch(dest_chunks, h2_packed)
    ys = _gmm(_gmm_schedule(counts, n * TOP_K), xs, w_e_in_all, w_e_dn_all, layer)
    yg = _gather_rows(ys, dest.reshape(-1)).reshape(TOP_K, n, d // 2)
    return _combine(wgt.T, h2, yg, w_s_in.astype(BF16), w_s_dn.astype(BF16), ln_g[None, :], ln_b[None, :])


def _pad_cols(w, width):
    return jnp.pad(w, ((0, 0), (0, width - w.shape[1])))


def kernel(x, w_in_ab, w_gla_gate_up, b_gla_gate, hgrn_norm_g, gla_norm_g, w_out_ab, hgrn_lb_logits, w_in_c, conv_c, b_if_c, w_out_c, w_router, router_bias, w_exp_in, w_exp_down, w_sh_in, w_sh_down, ln_mix_g, ln_mix_b, ln_ffn_g, ln_ffn_b):
    B, T, D = x.shape
    lower_bounds = jnp.cumsum(jax.nn.softmax(hgrn_lb_logits.astype(F32), axis=0), axis=0)
    h = x.reshape(B * T, D)
    for l in range(DEPTH):
        j = l // 2
        ln_g, ln_b = ln_mix_g[l][None, :], ln_mix_b[l][None, :]
        if l % 2 == 0:
            w_in = _pad_cols(w_in_ab[j], AB_WIDTH - GLA_GATE_RANK + LANES).astype(BF16)
            w_up = jnp.pad(w_gla_gate_up[j], ((0, LANES - GLA_GATE_RANK), (0, 0))).astype(BF16)
            h, h_packed = _hgrn_gla_layer(h, B, T, w_in, lower_bounds[l][None, :], w_up, b_gla_gate[j][None, :],
                                          hgrn_norm_g[j][None, :], gla_norm_g[j][None, :],
                                          w_out_ab[j].astype(BF16), ln_g, ln_b)
        else:
            w_in = _pad_cols(w_in_c[j], C_WIDTH - 2 * MLSTM_HEADS + LANES).astype(BF16)
            b_if = jnp.pad(b_if_c[j].astype(F32), (0, LANES - 2 * MLSTM_HEADS))[None, :]
            h, h_packed = _mlstm_layer(h, B, T, w_in, conv_c[j], b_if, w_out_c[j].astype(BF16), ln_g, ln_b)
        h = _moe_ln(h, h_packed, w_router[l], router_bias[l], w_exp_in, w_exp_down, l, w_sh_in[l], w_sh_down[l],
                    ln_ffn_g[l], ln_ffn_b[l])
    return h.reshape(B, T, D)
```

```python
import functools

import jax
import jax.numpy as jnp
from jax import lax
from jax.experimental import pallas as pl
from jax.experimental.pallas import tpu as pltpu
from jax.experimental.pallas import tpu_sc as plsc

F32 = jnp.float32
BF16 = jnp.bfloat16
I32 = jnp.int32

D_MODEL = 1024
DEPTH = 2
HGRN_HEADS = 4
HGRN_HEAD_DIM = 128
HGRN_WIDTH = HGRN_HEADS * HGRN_HEAD_DIM
GLA_HEADS = 4
GLA_KEY_DIM = 64
GLA_VAL_DIM = 128
GLA_KEY_WIDTH = GLA_HEADS * GLA_KEY_DIM
GLA_VAL_WIDTH = GLA_HEADS * GLA_VAL_DIM
GLA_GATE_RANK = 16
GLA_GATE_NORMALIZER = 16.0
LIN_CHUNK = 64
LIN_SUB = 16
AB_WIDTH = 4 * HGRN_WIDTH + 2 * GLA_KEY_WIDTH + 2 * GLA_VAL_WIDTH + GLA_GATE_RANK
MLSTM_HEADS = 4
MLSTM_QK_DIM = 128
MLSTM_V_DIM = 256
MLSTM_QK_WIDTH = MLSTM_HEADS * MLSTM_QK_DIM
MLSTM_V_WIDTH = MLSTM_HEADS * MLSTM_V_DIM
MLSTM_CONV = 4
MLSTM_CHUNK = 128
C_WIDTH = 2 * MLSTM_QK_WIDTH + 2 * MLSTM_V_WIDTH + 2 * MLSTM_HEADS
N_EXPERTS = 256
TOP_K = 8
N_GROUPS = 8
TOPK_GROUPS = 4
GROUP_SIZE = N_EXPERTS // N_GROUPS
EXPERT_FF = 256
SHARED_FF = 256
ROUTED_SCALE = 2.5
ALPHA = (2 * DEPTH) ** 0.25
LN_EPS = 1e-5
RMS_EPS = 1e-6

LANES = 128
VMEM_LIMIT = 56 * 1024 * 1024

MIX_ROWS = 128
ROUTE_TOK = 128
COMBINE_TOK = 256
GMM_ROWS = 256
SC_CORES, SC_SUBCORES = 2, 16
SC_WORKERS = SC_CORES * SC_SUBCORES
SC_ROWS = 64


def _cparams(*sem):
    return pltpu.CompilerParams(dimension_semantics=sem, vmem_limit_bytes=VMEM_LIMIT)


def _sigmoid(x):
    return 1.0 / (1.0 + jnp.exp(-x))


def _silu(x):
    return x * _sigmoid(x)


def _log_sigmoid(x):
    return jnp.minimum(x, 0.0) - jnp.log(1.0 + jnp.exp(-jnp.abs(x)))


def _split3(x):
    hi = x.astype(BF16)
    r = x - hi.astype(F32)
    mid = r.astype(BF16)
    lo = (r - mid.astype(F32)).astype(BF16)
    return hi, mid, lo


def _tri_dot_left(tri, x):
    return sum(jnp.dot(tri, p, preferred_element_type=F32) for p in _split3(x))


def _tri_dot_right(x, tri):
    return sum(jnp.dot(p, tri, preferred_element_type=F32) for p in _split3(x))


def _dot_nt(a, b):
    return lax.dot_general(a, b, (((1,), (1,)), ((), ())), preferred_element_type=F32)


def _dot_tn(a, b):
    return lax.dot_general(a, b, (((0,), (0,)), ((), ())), preferred_element_type=F32)


HI16 = -65536


def _pack_bf16_pair(x):
    c = x.shape[1] // 2
    bits = lambda v: lax.bitcast_convert_type(v.astype(BF16).astype(F32), I32)
    return (bits(x[:, c:]) & HI16) | lax.shift_right_logical(bits(x[:, :c]), 16)


def _unpack_bf16_pair(w):
    lo = lax.bitcast_convert_type(lax.shift_left(w, 16), F32)
    hi = lax.bitcast_convert_type(w & HI16, F32)
    return lo.astype(BF16), hi.astype(BF16)


def _layernorm(x, g, b):
    mu = jnp.mean(x, axis=-1, keepdims=True)
    xc = x - mu
    var = jnp.mean(xc * xc, axis=-1, keepdims=True)
    return xc * lax.rsqrt(var + LN_EPS) * g + b


N_UNITS = HGRN_HEADS + GLA_HEADS // 2
HALF_LANES = LANES // 2
LOG2E = 1.4426950408889634


def _pair_selector():
    assert LIN_CHUNK == HALF_LANES == GLA_KEY_DIM
    r = jnp.arange(LIN_SUB * LANES)[:, None]
    c = jnp.arange(LANES)[None, :]
    same_slot = (r // LANES) == (c % LIN_SUB)
    same_half = ((r % LANES) < HALF_LANES) == (c < HALF_LANES)
    return (same_slot & same_half).astype(BF16)


def _pairwise_tiles(q, k, b, lhs_ref, row0):
    c = LIN_SUB
    b2 = b * LOG2E
    for blk in range(LIN_CHUNK // c):
        lo = blk * c
        qb, kb, bb = q[lo:lo + c], k[lo:lo + c], b2[lo:lo + c]
        for j in range(c):
            e = jnp.exp2(jnp.minimum(bb - bb[j:j + 1], 0.0))
            lhs_ref[row0 + lo:row0 + lo + c, j * LANES:(j + 1) * LANES] = (qb * kb[j:j + 1] * e).astype(BF16)


def _unit_chunk(q, k, b, vs, r, st_ref, masks):
    L, c = LIN_CHUNK, LIN_SUB
    lane_lo, col_mod, same_blk_causal = masks
    heads = [lane_lo, ~lane_lo] if len(vs) == 2 else [None]

    def pick(x, m):
        return x if m is None else jnp.where(m, x, jnp.zeros_like(x))

    g = b[L - 1:L, :]
    st = st_ref[...]
    st_b = st.astype(BF16)
    qx = (q * jnp.exp(b)).astype(BF16)
    outs = [_dot_nt(pick(qx, m), st_b) for m in heads]
    k_end = (k * jnp.exp(g - b)).astype(BF16)
    upd = [_dot_tn(v.astype(BF16), k_end) for v in vs]
    st_ref[...] = st * jnp.exp(g) + (upd[0] if len(vs) == 1 else jnp.where(lane_lo, upd[0], upd[1]))

    off_rows = [jnp.zeros((c, LANES), F32)]
    for blk in range(1, L // c):
        lo = blk * c
        ref = b[lo - 1:lo]
        q_in = (q[lo:lo + c] * jnp.exp(b[lo:lo + c] - ref)).astype(BF16)
        k_in = (k * jnp.exp(jnp.minimum(ref - b, 0.0))).astype(BF16)
        if len(vs) == 2:
            k_cat = jnp.concatenate([pick(k_in, heads[0]), pick(k_in, heads[1])], axis=0)
        else:
            k_cat = jnp.concatenate([k_in, jnp.zeros_like(k_in)], axis=0)
        off_rows.append(jnp.where(col_mod < lo, _dot_nt(q_in, k_cat), 0.0))
    a = jnp.where(same_blk_causal, r, jnp.concatenate(off_rows, axis=0)).astype(BF16)
    v_cat = jnp.concatenate([vs[0], vs[-1]], axis=0).astype(BF16)
    return [o + jnp.dot(pick(a, m), v_cat, preferred_element_type=F32) for o, m in zip(outs, heads)]


def _rms_gate(o, g, gate):
    o = o * lax.rsqrt(jnp.mean(o * o, axis=-1, keepdims=True) + RMS_EPS)
    return o * g * _silu(gate)


class _Cols:
    def __init__(self, ref, off, width):
        self.ref, self.off, self.width = ref, off, width

    def __getitem__(self, idx):
        rows, cols = (slice(None), slice(None)) if idx is Ellipsis else idx
        lo = self.off + (cols.start or 0)
        hi = self.off + (self.width if cols.stop is None else cols.stop)
        return self.ref[rows, lo:hi]


def _project_residual_ln(x_ref, y_ref, wout_ref, g_ref, b_ref, h_ref, hp_ref):
    mix = jnp.dot(y_ref[...], wout_ref[...], preferred_element_type=F32)
    h = _layernorm(ALPHA * x_ref[...] + mix, g_ref[...], b_ref[...])
    h_ref[...] = h
    hp_ref[...] = _pack_bf16_pair(h)


def _hgrn_gla_kernel(x_ref, win_ref, lb_ref, wup_ref, bgk_ref, hng_ref, gng_ref, sel_ref, wout_ref, lng_ref, lnb_ref,
                     h_ref, hp_ref, proj_ref, y_ref, lhs_ref, *st_refs):
    @pl.when(pl.program_id(1) == 0)
    def _():
        for r in st_refs:
            r[...] = jnp.zeros_like(r)

    proj_ref[...] = jnp.dot(x_ref[...].astype(BF16), win_ref[...], preferred_element_type=F32)
    W, KW, VW = HGRN_WIDTH, GLA_KEY_WIDTH, GLA_VAL_WIDTH
    hq_ref, hf_ref, hi_ref, hg_ref = (_Cols(proj_ref, i * W, W) for i in range(4))
    gq_ref, gk_ref = _Cols(proj_ref, 4 * W, KW), _Cols(proj_ref, 4 * W + KW, KW)
    gv_ref, gg_ref = _Cols(proj_ref, 4 * W + 2 * KW, VW), _Cols(proj_ref, 4 * W + 2 * KW + VW, VW)
    glr_ref = _Cols(proj_ref, AB_WIDTH - GLA_GATE_RANK, LANES)

    L = LIN_CHUNK
    n_chunks = MIX_ROWS // L
    r_i = lax.broadcasted_iota(I32, (L, LANES), 0)
    c_i = lax.broadcasted_iota(I32, (L, LANES), 1)
    col_mod = jnp.where(c_i < HALF_LANES, c_i, c_i - HALF_LANES)
    same_blk_causal = (col_mod // LIN_SUB == r_i // LIN_SUB) & (col_mod <= r_i)
    c_sub = lax.broadcasted_iota(I32, (LIN_SUB, LANES), 1)
    col_mod_sub = jnp.where(c_sub < HALF_LANES, c_sub, c_sub - HALF_LANES)
    lane_lo = lax.broadcasted_iota(I32, (1, LANES), 1) < HALF_LANES
    masks = (lane_lo, col_mod_sub, same_blk_causal)
    tril = (lax.broadcasted_iota(I32, (L, L), 0) >= lax.broadcasted_iota(I32, (L, L), 1)).astype(BF16)
    lb = lb_ref[...]

    units = []
    for ck in range(n_chunks):
        rs = slice(ck * L, (ck + 1) * L)
        u = jnp.dot(glr_ref[rs, :].astype(BF16), wup_ref[...], preferred_element_type=F32) + bgk_ref[...]
        la_g = _log_sigmoid(u) * (1.0 / GLA_GATE_NORMALIZER)
        qs, ks, las = [], [], []
        for h in range(HGRN_HEADS):
            cs = slice(h * LANES, (h + 1) * LANES)
            z = hf_ref[rs, cs]
            lbh = lb[:, cs]
            qs.append(_silu(hq_ref[rs, cs]))
            ks.append((1.0 - lbh) * _sigmoid(-z))
            las.append(jnp.log(lbh + (1.0 - lbh) * _sigmoid(z)))
        for p in range(GLA_HEADS // 2):
            cs = slice(p * LANES, (p + 1) * LANES)
            qs.append(gq_ref[rs, cs] * (GLA_KEY_DIM ** -0.5))
            ks.append(gk_ref[rs, cs])
            las.append(la_g[:, cs])
        b_all = _tri_dot_left(tril, jnp.concatenate(las, axis=1))
        for un in range(N_UNITS):
            b = b_all[:, un * LANES:(un + 1) * LANES]
            _pairwise_tiles(qs[un], ks[un], b, lhs_ref, (ck * N_UNITS + un) * L)
            units.append((qs[un], ks[un], b))

    r_all = jnp.dot(lhs_ref[...], sel_ref[...], preferred_element_type=F32)

    for ck in range(n_chunks):
        rs = slice(ck * L, (ck + 1) * L)
        for un in range(N_UNITS):
            uc = ck * N_UNITS + un
            q, k, b = units[uc]
            r = r_all[uc * L:(uc + 1) * L]
            if un < HGRN_HEADS:
                cs = slice(un * LANES, (un + 1) * LANES)
                (o,) = _unit_chunk(q, k, b, [hi_ref[rs, cs]], r, st_refs[un], masks)
                y_ref[rs, cs] = _rms_gate(o, hng_ref[:, cs], hg_ref[rs, cs]).astype(y_ref.dtype)
            else:
                p = un - HGRN_HEADS
                vcs = [slice((2 * p + i) * GLA_VAL_DIM, (2 * p + i + 1) * GLA_VAL_DIM) for i in range(2)]
                outs = _unit_chunk(q, k, b, [gv_ref[rs, vc] for vc in vcs], r, st_refs[un], masks)
                for o, vc in zip(outs, vcs):
                    ys = slice(HGRN_WIDTH + vc.start, HGRN_WIDTH + vc.stop)
                    y_ref[rs, ys] = _rms_gate(o, gng_ref[:, vc], gg_ref[rs, vc]).astype(y_ref.dtype)

    _project_residual_ln(x_ref, y_ref, wout_ref, lng_ref, lnb_ref, h_ref, hp_ref)


def _mixer_layer_call(kernel_fn, name, x2, batch, seq, rows, consts, scratch_shapes):
    n, d = x2.shape
    steps = seq // rows
    row_spec = lambda width: pl.BlockSpec((rows, width), lambda b, i: (b * steps + i, 0))
    const_spec = lambda a: pl.BlockSpec(a.shape, lambda b, i: (0,) * a.ndim)
    return pl.pallas_call(
        kernel_fn,
        grid=(batch, steps),
        in_specs=[row_spec(d)] + [const_spec(a) for a in consts],
        out_specs=[row_spec(d), row_spec(d // 2)],
        out_shape=[jax.ShapeDtypeStruct((n, d), F32), jax.ShapeDtypeStruct((n, d // 2), I32)],
        scratch_shapes=scratch_shapes,
        compiler_params=_cparams("parallel", "arbitrary"),
        name=name,
    )(x2, *consts)


def _hgrn_gla_layer(x2, batch, seq, w_in, lb, w_up, b_gk, hgrn_g, gla_g, w_out, ln_g, ln_b):
    R = MIX_ROWS
    n_uc = (R // LIN_CHUNK) * N_UNITS
    scratch = ([pltpu.VMEM((R, w_in.shape[1]), F32), pltpu.VMEM((R, w_out.shape[0]), BF16),
                pltpu.VMEM((n_uc * LIN_CHUNK, LIN_SUB * LANES), BF16)]
               + [pltpu.VMEM((GLA_VAL_DIM, LANES), F32)] * N_UNITS)
    consts = (w_in, lb, w_up, b_gk, hgrn_g, gla_g, _pair_selector(), w_out, ln_g, ln_b)
    return _mixer_layer_call(_hgrn_gla_kernel, "hgrn_gla", x2, batch, seq, R, consts, scratch)


def _mlstm_kernel(x_ref, win_ref, cw_ref, bif_ref, wout_ref, lng_ref, lnb_ref, h_ref, hp_ref,
                  proj_ref, y_ref, ext_ref, *state_refs):
    L = MLSTM_CHUNK
    PAD = 8
    cn_refs, m_refs = state_refs[:MLSTM_HEADS], state_refs[MLSTM_HEADS:]

    @pl.when(pl.program_id(1) == 0)
    def _():
        ext_ref[0:PAD, :] = jnp.zeros((PAD, ext_ref.shape[1]), F32)
        for r in state_refs:
            r[...] = jnp.zeros_like(r)

    proj_ref[...] = jnp.dot(x_ref[...].astype(BF16), win_ref[...], preferred_element_type=F32)
    QK2, VW = 2 * MLSTM_QK_WIDTH, MLSTM_V_WIDTH
    qk_ref, v_ref = _Cols(proj_ref, 0, QK2), _Cols(proj_ref, QK2, VW)
    og_ref, gt_ref = _Cols(proj_ref, QK2 + VW, VW), _Cols(proj_ref, QK2 + 2 * VW, LANES)

    ext_ref[PAD:PAD + L, :] = qk_ref[...]
    conv = jnp.zeros((L, ext_ref.shape[1]), F32)
    for w in range(MLSTM_CONV):
        conv = conv + cw_ref[w:w + 1, :] * ext_ref[pl.ds(PAD - (MLSTM_CONV - 1) + w, L), :]
    ext_ref[0:PAD, :] = ext_ref[L:L + PAD, :]
    qk = _silu(conv)

    gt = gt_ref[...] + bif_ref[...]
    gt_t = gt.T
    r_i = lax.broadcasted_iota(I32, (L, L), 0)
    c_i = lax.broadcasted_iota(I32, (L, L), 1)
    causal = c_i <= r_i
    tril = causal.astype(BF16)
    triu = (r_i <= c_i).astype(BF16)
    b_cols = _tri_dot_left(tril, _log_sigmoid(gt))
    b_rows = _tri_dot_right(_log_sigmoid(gt_t), triu)
    ones_col = (lax.broadcasted_iota(I32, (L, LANES), 1) == 0).astype(F32)

    H = MLSTM_HEADS
    for h in range(H):
        q = qk[:, h * MLSTM_QK_DIM:(h + 1) * MLSTM_QK_DIM].astype(BF16)
        kf = qk[:, MLSTM_QK_WIDTH + h * MLSTM_QK_DIM:MLSTM_QK_WIDTH + (h + 1) * MLSTM_QK_DIM] * (MLSTM_QK_DIM ** -0.5)
        vs = slice(h * MLSTM_V_DIM, (h + 1) * MLSTM_V_DIM)
        v_ext = jnp.concatenate([v_ref[:, vs], ones_col], axis=1).astype(BF16)
        bc, br = b_cols[:, H + h:H + h + 1], b_rows[H + h:H + h + 1, :]
        ic, ir = gt[:, h:h + 1], gt_t[h:h + 1, :]
        m_prev = m_refs[h][0:1, 0:1]
        g = bc[L - 1:L, :]
        dmat = jnp.where(causal, bc - br + ir, -jnp.inf)
        inter = bc + m_prev
        m_j = jnp.maximum(inter, jnp.max(dmat, axis=-1, keepdims=True))
        s = _dot_nt(q, kf.astype(BF16)) * jnp.exp(dmat - m_j)
        w_inter = jnp.exp(inter - m_j)
        cn = cn_refs[h][...]
        nd = (jnp.dot(s.astype(BF16), v_ext, preferred_element_type=F32)
              + w_inter * jnp.dot(q, cn.astype(BF16), preferred_element_type=F32))
        num, den = nd[:, :MLSTM_V_DIM], nd[:, MLSTM_V_DIM:MLSTM_V_DIM + 1]
        hid = num / jnp.maximum(jnp.abs(den), jnp.exp(-m_j))
        u = g - bc + ic
        m_new = jnp.maximum(g + m_prev, jnp.max(u, axis=0, keepdims=True))
        wk = (kf * jnp.exp(u - m_new)).astype(BF16)
        cn_refs[h][...] = jnp.exp(g + m_prev - m_new) * cn + _dot_tn(wk, v_ext)
        m_refs[h][...] = jnp.broadcast_to(m_new, m_refs[h].shape)
        y_ref[:, vs] = (hid * _sigmoid(og_ref[:, vs])).astype(y_ref.dtype)

    _project_residual_ln(x_ref, y_ref, wout_ref, lng_ref, lnb_ref, h_ref, hp_ref)


def _mlstm_layer(x2, batch, seq, w_in, conv_w, b_if, w_out, ln_g, ln_b):
    R = MLSTM_CHUNK
    scratch = ([pltpu.VMEM((R, w_in.shape[1]), F32), pltpu.VMEM((R, w_out.shape[0]), BF16),
                pltpu.VMEM((R + 8, 2 * MLSTM_QK_WIDTH), F32)]
               + [pltpu.VMEM((MLSTM_QK_DIM, MLSTM_V_DIM + LANES), F32)] * MLSTM_HEADS
               + [pltpu.VMEM((8, LANES), F32)] * MLSTM_HEADS)
    consts = (w_in, conv_w, b_if, w_out, ln_g, ln_b)
    return _mixer_layer_call(_mlstm_kernel, "mlstm", x2, batch, seq, R, consts, scratch)


def _router_kernel(h_ref, whi_ref, wlo_ref, bias_ref, eidx_ref, wgt_ref, rank_ref, cnt_ref, carry_ref):
    TT, E, G, GS = ROUTE_TOK, N_EXPERTS, N_GROUPS, GROUP_SIZE

    @pl.when(pl.program_id(0) == 0)
    def _():
        carry_ref[...] = jnp.zeros_like(carry_ref)

    h = h_ref[...]
    h_hi = h.astype(BF16)
    h_lo = (h - h_hi.astype(F32)).astype(BF16)
    logits = _dot_nt(whi_ref[...], h_hi) + _dot_nt(whi_ref[...], h_lo) + _dot_nt(wlo_ref[...], h_hi)
    scores = _sigmoid(logits)
    biased = scores + bias_ref[:, 0:1]
    neg = -jnp.inf

    io_g = lax.broadcasted_iota(I32, (GS, TT), 0)
    io8 = lax.broadcasted_iota(I32, (G, TT), 0)
    gs = jnp.zeros((G, TT), F32)
    for g in range(G):
        blk = biased[g * GS:(g + 1) * GS, :]
        m1 = jnp.max(blk, axis=0, keepdims=True)
        i1 = jnp.min(jnp.where(blk == m1, io_g, GS), axis=0, keepdims=True)
        m2 = jnp.max(jnp.where(io_g == i1, neg, blk), axis=0, keepdims=True)
        gs = jnp.where(io8 == g, m1 + m2, gs)
    gsel = jnp.zeros((G, TT), F32)
    for _ in range(TOPK_GROUPS):
        m = jnp.max(gs, axis=0, keepdims=True)
        idx = jnp.min(jnp.where(gs == m, io8, G), axis=0, keepdims=True)
        hit = io8 == idx
        gsel = jnp.where(hit, 1.0, gsel)
        gs = jnp.where(hit, neg, gs)
    sel = jnp.concatenate([jnp.broadcast_to(gsel[g:g + 1, :], (GS, TT)) for g in range(G)], axis=0)
    masked = jnp.where(sel > 0.0, biased, neg)

    io_e = lax.broadcasted_iota(I32, (E, TT), 0)
    eidx = jnp.zeros((TOP_K, TT), I32)
    wsel = jnp.zeros((TOP_K, TT), F32)
    chosen = jnp.zeros((E, TT), F32)
    for k in range(TOP_K):
        m = jnp.max(masked, axis=0, keepdims=True)
        idx = jnp.min(jnp.where(masked == m, io_e, E), axis=0, keepdims=True)
        hit = io_e == idx
        sc = jnp.sum(jnp.where(hit, scores, 0.0), axis=0, keepdims=True)
        eidx = jnp.where(io8 == k, idx, eidx)
        wsel = jnp.where(io8 == k, sc, wsel)
        chosen = jnp.where(hit, 1.0, chosen)
        masked = jnp.where(hit, neg, masked)
    wgt_ref[...] = wsel / jnp.sum(wsel, axis=0, keepdims=True) * ROUTED_SCALE
    eidx_ref[...] = eidx

    triu = (lax.broadcasted_iota(I32, (TT, TT), 0) <= lax.broadcasted_iota(I32, (TT, TT), 1)).astype(BF16)
    cum = jnp.dot(chosen.astype(BF16), triu, preferred_element_type=F32)
    carry = carry_ref[:, 0:1]
    before = cum - chosen + carry
    rank = jnp.zeros((TOP_K, TT), F32)
    for k in range(TOP_K):
        r = jnp.sum(jnp.where(io_e == eidx[k:k + 1, :], before, 0.0), axis=0, keepdims=True)
        rank = jnp.where(io8 == k, r, rank)
    rank_ref[...] = rank.astype(I32)
    total = carry + cum[:, TT - 1:TT]
    carry_ref[...] = jnp.broadcast_to(total, carry_ref.shape)
    cnt_ref[...] = jnp.broadcast_to(total, cnt_ref.shape)


def _router(h2, wt_hi, wt_lo, bias_col):
    n, d = h2.shape
    TT, E = ROUTE_TOK, N_EXPERTS
    tok_spec = pl.BlockSpec((TOP_K, TT), lambda i: (0, i))
    return pl.pallas_call(
        _router_kernel,
        grid=(n // TT,),
        in_specs=[pl.BlockSpec((TT, d), lambda i: (i, 0)),
                  pl.BlockSpec((E, d), lambda i: (0, 0)),
                  pl.BlockSpec((E, d), lambda i: (0, 0)),
                  pl.BlockSpec((E, LANES), lambda i: (0, 0))],
        out_specs=[tok_spec, tok_spec, tok_spec, pl.BlockSpec((E, LANES), lambda i: (0, 0))],
        out_shape=[jax.ShapeDtypeStruct((TOP_K, n), I32), jax.ShapeDtypeStruct((TOP_K, n), F32),
                   jax.ShapeDtypeStruct((TOP_K, n), I32), jax.ShapeDtypeStruct((E, LANES), F32)],
        scratch_shapes=[pltpu.VMEM((E, LANES), F32)],
        compiler_params=_cparams("arbitrary"),
        name="router",
    )(h2, wt_hi, wt_lo, bias_col)


def _dest_kernel(eidx_ref, rank_ref, offs_ref, dest_ref):
    TT, E = eidx_ref.shape[1], N_EXPERTS
    io_e = lax.broadcasted_iota(I32, (E, TT), 0)
    io8 = lax.broadcasted_iota(I32, (TOP_K, TT), 0)
    offs = offs_ref[:, 0:1]
    base = jnp.zeros((TOP_K, TT), F32)
    for k in range(TOP_K):
        r = jnp.sum(jnp.where(io_e == eidx_ref[k:k + 1, :], offs, 0.0), axis=0, keepdims=True)
        base = jnp.where(io8 == k, r, base)
    dest_ref[...] = base.astype(I32) + rank_ref[...]


def _dest_rows(eidx, rank, offs_col):
    n = eidx.shape[1]
    TT = 512
    spec = pl.BlockSpec((TOP_K, TT), lambda i: (0, i))
    return pl.pallas_call(
        _dest_kernel,
        grid=(n // TT,),
        in_specs=[spec, spec, pl.BlockSpec((N_EXPERTS, LANES), lambda i: (0, 0))],
        out_specs=spec,
        out_shape=jax.ShapeDtypeStruct((TOP_K, n), I32),
        compiler_params=_cparams("parallel"),
        name="dest_rows",
    )(eidx, rank, offs_col)


def _sc_mesh():
    return plsc.VectorSubcoreMesh(core_axis_name="c", subcore_axis_name="s")


def _sc_worker_id():
    return lax.axis_index("s") * SC_CORES + lax.axis_index("c")


def _dispatch(dest_chunks, x2):
    n, d = x2.shape
    n_chunks = n // SC_ROWS // SC_WORKERS

    @functools.partial(
        pl.kernel, mesh=_sc_mesh(),
        out_type=jax.ShapeDtypeStruct((n * TOP_K, d), x2.dtype),
        scratch_types=[pltpu.VMEM((2, TOP_K, SC_ROWS), I32), pltpu.VMEM((2, SC_ROWS, d), x2.dtype),
                       pltpu.SemaphoreType.DMA((2,))],
    )
    def scatter_rows(x_hbm, idx_hbm, out_hbm, idx_v, rows_v, wsem):
        base = _sc_worker_id() * n_chunks

        def row_scatter(b, k):
            return pltpu.make_async_copy(rows_v.at[b], out_hbm.at[idx_v.at[b, k]], wsem.at[b])

        def drain(b):
            for k in range(TOP_K):
                row_scatter(b, k).wait()

        @pl.loop(0, n_chunks, step=2)
        def _(i):
            for b in range(2):
                c = base + i + b

                @pl.when(i > 0)
                def _():
                    drain(b)

                pltpu.sync_copy(idx_hbm.at[c], idx_v.at[b])
                pltpu.sync_copy(x_hbm.at[pl.ds(c * SC_ROWS, SC_ROWS)], rows_v.at[b])
                for k in range(TOP_K):
                    row_scatter(b, k).start()

        for b in range(2):
            drain(b)

    return scatter_rows(x2, dest_chunks)


def _gather_rows(src, idx):
    d = src.shape[1]
    p = idx.shape[0]
    per_w = p // SC_WORKERS
    n_chunks = per_w // SC_ROWS

    @functools.partial(
        pl.kernel, mesh=_sc_mesh(),
        out_type=jax.ShapeDtypeStruct((p, d), src.dtype),
        scratch_types=[pltpu.VMEM((2, SC_ROWS), I32), pltpu.VMEM((2, SC_ROWS, d), src.dtype),
                       pltpu.SemaphoreType.DMA((2,)), pltpu.SemaphoreType.DMA((2,))],
    )
    def gather_rows(src_hbm, idx_hbm, out_hbm, idx_v, rows_v, gsem, wsem):
        base = _sc_worker_id() * per_w

        def row_gather(b):
            return pltpu.make_async_copy(src_hbm.at[idx_v.at[b]], rows_v.at[b], gsem.at[b])

        def write_back(b, off):
            return pltpu.make_async_copy(rows_v.at[b], out_hbm.at[pl.ds(off, SC_ROWS)], wsem.at[b])

        @pl.loop(0, n_chunks, step=2)
        def _(i):
            for b in range(2):
                off = base + (i + b) * SC_ROWS

                @pl.when(i > 0)
                def _():
                    write_back(b, off).wait()

                pltpu.sync_copy(idx_hbm.at[pl.ds(off, SC_ROWS)], idx_v.at[b])
                row_gather(b).start()
            for b in range(2):
                row_gather(b).wait()
                write_back(b, base + (i + b) * SC_ROWS).start()

        for b in range(2):
            write_back(b, base).wait()

    return gather_rows(src, idx)


GMM_X_SLOTS = 3
GMM_O_SLOTS = 2


def _gmm_kernel(tile_ref, grp_ref, lo_ref, hi_ref, first_ref, last_ref, newgrp_ref, nextgrp_ref, nreal_ref,
                xs_hbm, win_hbm, wdn_hbm, ys_hbm,
                xbuf, obuf, winbuf, wdnbuf, win_bf, wdn_bf, xsem, osem, wsem, *, layer):
    TM = GMM_ROWS
    n_tiles = xs_hbm.shape[0] // TM
    half = xs_hbm.shape[1]

    def x_copy(t, slot):
        return pltpu.make_async_copy(xs_hbm.at[pl.ds(t * TM, TM)], xbuf.at[slot], xsem.at[slot])

    def o_copy(t, slot):
        return pltpu.make_async_copy(obuf.at[slot], ys_hbm.at[pl.ds(t * TM, TM)], osem.at[slot])

    def w_copies(e, slot):
        return (pltpu.make_async_copy(win_hbm.at[layer, e], winbuf.at[slot], wsem.at[0, slot]),
                pltpu.make_async_copy(wdn_hbm.at[layer, e], wdnbuf.at[slot], wsem.at[1, slot]))

    for t in range(GMM_X_SLOTS - 1):
        x_copy(t, t).start()
    for c in w_copies(grp_ref[0], 0):
        c.start()

    def visit(v, wslot):
        t = tile_ref[v]
        xslot = t % GMM_X_SLOTS
        oslot = t % GMM_O_SLOTS
        is_first = first_ref[v] == 1
        is_new = newgrp_ref[v] == 1
        wslot = jnp.where(is_new & (v > 0), 1 - wslot, wslot)

        @pl.when(is_first)
        def _():
            x_copy(t, xslot).wait()
            ahead = t + GMM_X_SLOTS - 1

            @pl.when(ahead < n_tiles)
            def _():
                x_copy(ahead, ahead % GMM_X_SLOTS).start()

            @pl.when(t >= GMM_O_SLOTS)
            def _():
                o_copy(t - GMM_O_SLOTS, oslot).wait()

        @pl.when(is_new)
        def _():
            for c in w_copies(grp_ref[v], wslot):
                c.wait()
            win_bf[...] = winbuf[wslot].astype(BF16)
            wdn_bf[...] = wdnbuf[wslot].astype(BF16)
            nxt = nextgrp_ref[v]

            @pl.when(nxt >= 0)
            def _():
                for c in w_copies(nxt, 1 - wslot):
                    c.start()

        x_lo, x_hi = _unpack_bf16_pair(xbuf[xslot])
        gu = (jnp.dot(x_lo, win_bf[:half, :], preferred_element_type=F32)
              + jnp.dot(x_hi, win_bf[half:, :], preferred_element_type=F32))
        act = (_silu(gu[:, :EXPERT_FF]) * gu[:, EXPERT_FF:]).astype(BF16)
        y = _pack_bf16_pair(jnp.dot(act, wdn_bf[...], preferred_element_type=F32))
        rows = lax.broadcasted_iota(I32, (TM, 1), 0)
        mine = (rows >= lo_ref[v]) & (rows < hi_ref[v])

        @pl.when(is_first)
        def _():
            obuf[oslot] = jnp.where(mine, y, 0)

        @pl.when(jnp.logical_not(is_first))
        def _():
            obuf[oslot] = jnp.where(mine, y, obuf[oslot])

        @pl.when(last_ref[v] == 1)
        def _():
            o_copy(t, oslot).start()

        return wslot

    lax.fori_loop(0, nreal_ref[0], visit, jnp.int32(0))
    for t in range(n_tiles - GMM_O_SLOTS, n_tiles):
        o_copy(t, t % GMM_O_SLOTS).wait()


def _gmm(meta, xs, w_in_all, w_dn_all, layer):
    p, half = xs.shape
    d = 2 * half
    TM = GMM_ROWS
    assert p % TM == 0 and p // TM >= GMM_X_SLOTS
    any_spec = pl.BlockSpec(memory_space=pl.ANY)
    grid_spec = pltpu.PrefetchScalarGridSpec(
        num_scalar_prefetch=len(meta),
        grid=(1,),
        in_specs=[any_spec, any_spec, any_spec],
        out_specs=any_spec,
        scratch_shapes=[pltpu.VMEM((GMM_X_SLOTS, TM, half), I32), pltpu.VMEM((GMM_O_SLOTS, TM, half), I32),
                        pltpu.VMEM((2, d, 2 * EXPERT_FF), F32), pltpu.VMEM((2, EXPERT_FF, d), F32),
                        pltpu.VMEM((d, 2 * EXPERT_FF), BF16), pltpu.VMEM((EXPERT_FF, d), BF16),
                        pltpu.SemaphoreType.DMA((GMM_X_SLOTS,)), pltpu.SemaphoreType.DMA((GMM_O_SLOTS,)),
                        pltpu.SemaphoreType.DMA((2, 2))],
    )
    return pl.pallas_call(
        functools.partial(_gmm_kernel, layer=layer),
        grid_spec=grid_spec,
        out_shape=jax.ShapeDtypeStruct((p, half), I32),
        compiler_params=_cparams("arbitrary"),
        name="gmm",
    )(*meta, xs, w_in_all, w_dn_all)


def _gmm_schedule(counts, n_rows):
    TM = GMM_ROWS
    E = counts.shape[0]
    max_visits = n_rows // TM + E - 1
    ends = jnp.cumsum(counts)
    starts = ends - counts
    first_tile = starts // TM
    n_vis = jnp.where(counts > 0, jnp.maximum(ends - 1, 0) // TM - first_tile + 1, 0)
    vis_end = jnp.cumsum(n_vis)
    vis_start = vis_end - n_vis
    n_real = vis_end[-1]
    idx = jnp.arange(max_visits, dtype=I32)
    v = jnp.minimum(idx, jnp.maximum(n_real - 1, 0))
    real = idx < n_real
    grp = jnp.minimum(jnp.sum((vis_end[None, :] <= v[:, None]).astype(I32), axis=1), E - 1)
    onehot = grp[:, None] == jnp.arange(E, dtype=I32)[None, :]

    def take(table):
        return jnp.sum(jnp.where(onehot, table[None, :], 0), axis=1)

    tile = take(first_tile) + (v - take(vis_start))
    lo = jnp.where(real, jnp.maximum(take(starts), tile * TM) - tile * TM, 0)
    hi = jnp.where(real, jnp.minimum(take(ends), (tile + 1) * TM) - tile * TM, 0)
    prev = lambda a: jnp.concatenate([jnp.full((1,), -1, I32), a[:-1]])
    first = real & (tile != prev(tile))
    newgrp = real & (grp != prev(grp))
    last = real & (jnp.concatenate([first[1:], jnp.ones((1,), bool)]) | (idx == n_real - 1))
    cand = jnp.where(counts > 0, jnp.arange(E, dtype=I32), E)
    later = jnp.concatenate([jnp.flip(lax.cummin(jnp.flip(cand)))[1:], jnp.full((1,), E, I32)])
    nextgrp = take(jnp.where(later < E, later, -1))
    meta = tuple(a.astype(I32) for a in (tile, grp, lo, hi, first, last, newgrp, nextgrp))
    return meta + (n_real.astype(I32).reshape(1),)


def _combine_kernel(wgt_ref, h_ref, yg_ref, wsi_ref, wsd_ref, g_ref, b_ref, o_ref):
    h = h_ref[...]
    gu = jnp.dot(h.astype(BF16), wsi_ref[...], preferred_element_type=F32)
    act = (_silu(gu[:, :SHARED_FF]) * gu[:, SHARED_FF:]).astype(BF16)
    acc = ALPHA * h + jnp.dot(act, wsd_ref[...], preferred_element_type=F32)
    wgt = wgt_ref[...]
    half = yg_ref.shape[2]
    r_lo = jnp.zeros((h.shape[0], half), F32)
    r_hi = jnp.zeros((h.shape[0], half), F32)
    for k in range(TOP_K):
        y_lo, y_hi = _unpack_bf16_pair(yg_ref[k])
        r_lo = r_lo + wgt[:, k:k + 1] * y_lo.astype(F32)
        r_hi = r_hi + wgt[:, k:k + 1] * y_hi.astype(F32)
    acc = acc + jnp.concatenate([r_lo, r_hi], axis=1)
    o_ref[...] = _layernorm(acc, g_ref[...], b_ref[...])


def _combine(wgt_tk, h2, yg, w_si, w_sd, g, b):
    n, d = h2.shape
    TT = COMBINE_TOK
    const2 = lambda shape: pl.BlockSpec(shape, lambda i: (0, 0))
    return pl.pallas_call(
        _combine_kernel,
        grid=(n // TT,),
        in_specs=[pl.BlockSpec((TT, TOP_K), lambda i: (i, 0)),
                  pl.BlockSpec((TT, d), lambda i: (i, 0)),
                  pl.BlockSpec((TOP_K, TT, d // 2), lambda i: (0, i, 0)),
                  const2((d, 2 * SHARED_FF)), const2((SHARED_FF, d)), const2((1, d)), const2((1, d))],
        out_specs=pl.BlockSpec((TT, d), lambda i: (i, 0)),
        out_shape=jax.ShapeDtypeStruct((n, d), F32),
        compiler_params=_cparams("parallel"),
        name="combine",
    )(wgt_tk, h2, yg, w_si, w_sd, g, b)


def _moe_ln(h2, h2_packed, w_router, r_bias, w_e_in_all, w_e_dn_all, layer, w_s_in, w_s_dn, ln_g, ln_b):
    n, d = h2.shape
    wt = w_router.T
    wt_hi = wt.astype(BF16)
    wt_lo = (wt - wt_hi.astype(F32)).astype(BF16)
    bias_col = jnp.broadcast_to(r_bias.astype(F32)[:, None], (N_EXPERTS, LANES))
    eidx, wgt, rank, cnt = _router(h2, wt_hi, wt_lo, bias_col)
    counts = cnt[:, 0].astype(I32)
    offs = jnp.cumsum(counts) - counts
    offs_col = jnp.broadcast_to(offs.astype(F32)[:, None], (N_EXPERTS, LANES))
    dest = _dest_rows(eidx, rank, offs_col)
    dest_chunks = dest.reshape(TOP_K, n // SC_ROWS, SC_ROWS).transpose(1, 0, 2)
    xs = _dispatch(dest_chunks, h2_packed)
    ys = _gmm(_gmm_schedule(counts, n * TOP_K), xs, w_e_in_all, w_e_dn_all, layer)
    yg = _gather_rows(ys, dest.reshape(-1)).reshape(TOP_K, n, d // 2)
    return _combine(wgt.T, h2, yg, w_s_in.astype(BF16), w_s_dn.astype(BF16), ln_g[None, :], ln_b[None, :])


def _pad_cols(w, width):
    return jnp.pad(w, ((0, 0), (0, width - w.shape[1])))


def kernel(x, w_in_ab, w_gla_gate_up, b_gla_gate, hgrn_norm_g, gla_norm_g, w_out_ab, hgrn_lb_logits, w_in_c, conv_c, b_if_c, w_out_c, w_router, router_bias, w_exp_in, w_exp_down, w_sh_in, w_sh_down, ln_mix_g, ln_mix_b, ln_ffn_g, ln_ffn_b):
    B, T, D = x.shape
    lower_bounds = jnp.cumsum(jax.nn.softmax(hgrn_lb_logits.astype(F32), axis=0), axis=0)
    h = x.reshape(B * T, D)
    for l in range(DEPTH):
        j = l // 2
        ln_g, ln_b = ln_mix_g[l][None, :], ln_mix_b[l][None, :]
        if l % 2 == 0:
            w_in = _pad_cols(w_in_ab[j], AB_WIDTH - GLA_GATE_RANK + LANES).astype(BF16)
            w_up = jnp.pad(w_gla_gate_up[j], ((0, LANES - GLA_GATE_RANK), (0, 0))).astype(BF16)
            h, h_packed = _hgrn_gla_layer(h, B, T, w_in, lower_bounds[l][None, :], w_up, b_gla_gate[j][None, :],
                                          hgrn_norm_g[j][None, :], gla_norm_g[j][None, :],
                                          w_out_ab[j].astype(BF16), ln_g, ln_b)
        else:
            w_in = _pad_cols(w_in_c[j], C_WIDTH - 2 * MLSTM_HEADS + LANES).astype(BF16)
            b_if = jnp.pad(b_if_c[j].astype(F32), (0, LANES - 2 * MLSTM_HEADS))[None, :]
            h, h_packed = _mlstm_layer(h, B, T, w_in, conv_c[j], b_if, w_out_c[j].astype(BF16), ln_g, ln_b)
        h = _moe_ln(h, h_packed, w_router[l], router_bias[l], w_exp_in, w_exp_down, l, w_sh_in[l], w_sh_down[l],
                    ln_ffn_g[l], ln_ffn_b[l])
    return h.reshape(B, T, D)
```

```python
import functools

import jax
import jax.numpy as jnp
from jax import lax
from jax.experimental import pallas as pl
from jax.experimental.pallas import tpu as pltpu
from jax.experimental.pallas import tpu_sc as plsc

F32 = jnp.float32
BF16 = jnp.bfloat16
I32 = jnp.int32

D_MODEL = 1024
DEPTH = 2
HGRN_HEADS = 4
HGRN_HEAD_DIM = 128
HGRN_WIDTH = HGRN_HEADS * HGRN_HEAD_DIM
GLA_HEADS = 4
GLA_KEY_DIM = 64
GLA_VAL_DIM = 128
GLA_KEY_WIDTH = GLA_HEADS * GLA_KEY_DIM
GLA_VAL_WIDTH = GLA_HEADS * GLA_VAL_DIM
GLA_GATE_RANK = 16
GLA_GATE_NORMALIZER = 16.0
LIN_CHUNK = 64
LIN_SUB = 16
AB_WIDTH = 4 * HGRN_WIDTH + 2 * GLA_KEY_WIDTH + 2 * GLA_VAL_WIDTH + GLA_GATE_RANK
MLSTM_HEADS = 4
MLSTM_QK_DIM = 128
MLSTM_V_DIM = 256
MLSTM_QK_WIDTH = MLSTM_HEADS * MLSTM_QK_DIM
MLSTM_V_WIDTH = MLSTM_HEADS * MLSTM_V_DIM
MLSTM_CONV = 4
MLSTM_CHUNK = 128
C_WIDTH = 2 * MLSTM_QK_WIDTH + 2 * MLSTM_V_WIDTH + 2 * MLSTM_HEADS
N_EXPERTS = 256
TOP_K = 8
N_GROUPS = 8
TOPK_GROUPS = 4
GROUP_SIZE = N_EXPERTS // N_GROUPS
EXPERT_FF = 256
SHARED_FF = 256
ROUTED_SCALE = 2.5
ALPHA = (2 * DEPTH) ** 0.25
LN_EPS = 1e-5
RMS_EPS = 1e-6

LANES = 128
VMEM_LIMIT = 56 * 1024 * 1024

MIX_ROWS = 128
ROUTE_TOK = 128
COMBINE_TOK = 256
COMBINE_PARTS = 4
GMM_ROWS = 256
SC_CORES, SC_SUBCORES = 2, 16
SC_WORKERS = SC_CORES * SC_SUBCORES
SC_ROWS = 64


def _cparams(*sem):
    return pltpu.CompilerParams(dimension_semantics=sem, vmem_limit_bytes=VMEM_LIMIT)


def _sigmoid(x):
    return 1.0 / (1.0 + jnp.exp(-x))


def _silu(x):
    return x * _sigmoid(x)


def _log_sigmoid(x):
    return jnp.minimum(x, 0.0) - jnp.log(1.0 + jnp.exp(-jnp.abs(x)))


def _split3(x):
    hi = x.astype(BF16)
    r = x - hi.astype(F32)
    mid = r.astype(BF16)
    lo = (r - mid.astype(F32)).astype(BF16)
    return hi, mid, lo


def _tri_dot_left(tri, x):
    return sum(jnp.dot(tri, p, preferred_element_type=F32) for p in _split3(x))


def _tri_dot_right(x, tri):
    return sum(jnp.dot(p, tri, preferred_element_type=F32) for p in _split3(x))


def _dot_nt(a, b):
    return lax.dot_general(a, b, (((1,), (1,)), ((), ())), preferred_element_type=F32)


def _dot_tn(a, b):
    return lax.dot_general(a, b, (((0,), (0,)), ((), ())), preferred_element_type=F32)


HI16 = -65536


def _pack_bf16_pair(x):
    c = x.shape[1] // 2
    bits = lambda v: lax.bitcast_convert_type(v.astype(BF16).astype(F32), I32)
    return (bits(x[:, c:]) & HI16) | lax.shift_right_logical(bits(x[:, :c]), 16)


def _unpack_bf16_pair(w):
    lo = lax.bitcast_convert_type(lax.shift_left(w, 16), F32)
    hi = lax.bitcast_convert_type(w & HI16, F32)
    return lo.astype(BF16), hi.astype(BF16)


def _layernorm(x, g, b):
    mu = jnp.mean(x, axis=-1, keepdims=True)
    xc = x - mu
    var = jnp.mean(xc * xc, axis=-1, keepdims=True)
    return xc * lax.rsqrt(var + LN_EPS) * g + b


N_UNITS = HGRN_HEADS + GLA_HEADS // 2
HALF_LANES = LANES // 2
LOG2E = 1.4426950408889634


def _pair_selector():
    assert LIN_CHUNK == HALF_LANES == GLA_KEY_DIM
    r = jnp.arange(LIN_SUB * LANES)[:, None]
    c = jnp.arange(LANES)[None, :]
    same_slot = (r // LANES) == (c % LIN_SUB)
    same_half = ((r % LANES) < HALF_LANES) == (c < HALF_LANES)
    return (same_slot & same_half).astype(BF16)


def _pairwise_tiles(q, k, b, lhs_ref, row0):
    c = LIN_SUB
    b2 = b * LOG2E
    for blk in range(LIN_CHUNK // c):
        lo = blk * c
        qb, kb, bb = q[lo:lo + c], k[lo:lo + c], b2[lo:lo + c]
        for j in range(c):
            e = jnp.exp2(jnp.minimum(bb - bb[j:j + 1], 0.0))
            lhs_ref[row0 + lo:row0 + lo + c, j * LANES:(j + 1) * LANES] = (qb * kb[j:j + 1] * e).astype(BF16)


def _unit_chunk(q, k, b, vs, r, st_ref, masks):
    L, c = LIN_CHUNK, LIN_SUB
    lane_lo, col_mod, same_blk_causal = masks
    heads = [lane_lo, ~lane_lo] if len(vs) == 2 else [None]

    def pick(x, m):
        return x if m is None else jnp.where(m, x, jnp.zeros_like(x))

    g = b[L - 1:L, :]
    st = st_ref[...]
    st_b = st.astype(BF16)
    qx = (q * jnp.exp(b)).astype(BF16)
    outs = [_dot_nt(pick(qx, m), st_b) for m in heads]
    k_end = (k * jnp.exp(g - b)).astype(BF16)
    upd = [_dot_tn(v.astype(BF16), k_end) for v in vs]
    st_ref[...] = st * jnp.exp(g) + (upd[0] if len(vs) == 1 else jnp.where(lane_lo, upd[0], upd[1]))

    off_rows = [jnp.zeros((c, LANES), F32)]
    for blk in range(1, L // c):
        lo = blk * c
        ref = b[lo - 1:lo]
        q_in = (q[lo:lo + c] * jnp.exp(b[lo:lo + c] - ref)).astype(BF16)
        k_in = (k * jnp.exp(jnp.minimum(ref - b, 0.0))).astype(BF16)
        if len(vs) == 2:
            k_cat = jnp.concatenate([pick(k_in, heads[0]), pick(k_in, heads[1])], axis=0)
        else:
            k_cat = jnp.concatenate([k_in, jnp.zeros_like(k_in)], axis=0)
        off_rows.append(jnp.where(col_mod < lo, _dot_nt(q_in, k_cat), 0.0))
    a = jnp.where(same_blk_causal, r, jnp.concatenate(off_rows, axis=0)).astype(BF16)
    v_cat = jnp.concatenate([vs[0], vs[-1]], axis=0).astype(BF16)
    return [o + jnp.dot(pick(a, m), v_cat, preferred_element_type=F32) for o, m in zip(outs, heads)]


def _rms_gate(o, g, gate):
    o = o * lax.rsqrt(jnp.mean(o * o, axis=-1, keepdims=True) + RMS_EPS)
    return o * g * _silu(gate)


class _Cols:
    def __init__(self, ref, off, width):
        self.ref, self.off, self.width = ref, off, width

    def __getitem__(self, idx):
        rows, cols = (slice(None), slice(None)) if idx is Ellipsis else idx
        lo = self.off + (cols.start or 0)
        hi = self.off + (self.width if cols.stop is None else cols.stop)
        return self.ref[rows, lo:hi]


def _project_residual_ln(x_ref, y_ref, wout_ref, g_ref, b_ref, h_ref, hp_ref):
    mix = jnp.dot(y_ref[...], wout_ref[...], preferred_element_type=F32)
    h = _layernorm(ALPHA * x_ref[...] + mix, g_ref[...], b_ref[...])
    h_ref[...] = h
    hp_ref[...] = _pack_bf16_pair(h)


def _ping_pong_projection(x_ref, xn_ref, win_ref, proj_refs, init_state, step):
    i = pl.program_id(1)
    project = lambda ref: jnp.dot(ref[...].astype(BF16), win_ref[...], preferred_element_type=F32)

    @pl.when(i == 0)
    def _():
        init_state()
        proj_refs[0][...] = project(x_ref)

    for parity in range(2):
        @pl.when(i % 2 == parity)
        def _():
            proj_refs[1 - parity][...] = project(xn_ref)
            step(proj_refs[parity])


def _hgrn_gla_kernel(x_ref, xn_ref, win_ref, lb_ref, wup_ref, bgk_ref, hng_ref, gng_ref, sel_ref, wout_ref,
                     lng_ref, lnb_ref, h_ref, hp_ref, proj_a, proj_b, y_ref, lhs_ref, *st_refs):
    def init_state():
        for r in st_refs:
            r[...] = jnp.zeros_like(r)

    def step(proj_ref):
        _hgrn_gla_step(proj_ref, x_ref, lb_ref, wup_ref, bgk_ref, hng_ref, gng_ref, sel_ref, wout_ref, lng_ref,
                       lnb_ref, h_ref, hp_ref, y_ref, lhs_ref, st_refs)

    _ping_pong_projection(x_ref, xn_ref, win_ref, (proj_a, proj_b), init_state, step)


def _hgrn_gla_step(proj_ref, x_ref, lb_ref, wup_ref, bgk_ref, hng_ref, gng_ref, sel_ref, wout_ref, lng_ref, lnb_ref,
                   h_ref, hp_ref, y_ref, lhs_ref, st_refs):
    W, KW, VW = HGRN_WIDTH, GLA_KEY_WIDTH, GLA_VAL_WIDTH
    hq_ref, hf_ref, hi_ref, hg_ref = (_Cols(proj_ref, i * W, W) for i in range(4))
    gq_ref, gk_ref = _Cols(proj_ref, 4 * W, KW), _Cols(proj_ref, 4 * W + KW, KW)
    gv_ref, gg_ref = _Cols(proj_ref, 4 * W + 2 * KW, VW), _Cols(proj_ref, 4 * W + 2 * KW + VW, VW)
    glr_ref = _Cols(proj_ref, AB_WIDTH - GLA_GATE_RANK, LANES)

    L = LIN_CHUNK
    n_chunks = MIX_ROWS // L
    r_i = lax.broadcasted_iota(I32, (L, LANES), 0)
    c_i = lax.broadcasted_iota(I32, (L, LANES), 1)
    col_mod = jnp.where(c_i < HALF_LANES, c_i, c_i - HALF_LANES)
    same_blk_causal = (col_mod // LIN_SUB == r_i // LIN_SUB) & (col_mod <= r_i)
    c_sub = lax.broadcasted_iota(I32, (LIN_SUB, LANES), 1)
    col_mod_sub = jnp.where(c_sub < HALF_LANES, c_sub, c_sub - HALF_LANES)
    lane_lo = lax.broadcasted_iota(I32, (1, LANES), 1) < HALF_LANES
    masks = (lane_lo, col_mod_sub, same_blk_causal)
    tril = (lax.broadcasted_iota(I32, (L, L), 0) >= lax.broadcasted_iota(I32, (L, L), 1)).astype(BF16)
    lb = lb_ref[...]

    units = []
    for ck in range(n_chunks):
        rs = slice(ck * L, (ck + 1) * L)
        u = jnp.dot(glr_ref[rs, :].astype(BF16), wup_ref[...], preferred_element_type=F32) + bgk_ref[...]
        la_g = _log_sigmoid(u) * (1.0 / GLA_GATE_NORMALIZER)
        qs, ks, las = [], [], []
        for h in range(HGRN_HEADS):
            cs = slice(h * LANES, (h + 1) * LANES)
            z = hf_ref[rs, cs]
            lbh = lb[:, cs]
            qs.append(_silu(hq_ref[rs, cs]))
            ks.append((1.0 - lbh) * _sigmoid(-z))
            las.append(jnp.log(lbh + (1.0 - lbh) * _sigmoid(z)))
        for p in range(GLA_HEADS // 2):
            cs = slice(p * LANES, (p + 1) * LANES)
            qs.append(gq_ref[rs, cs] * (GLA_KEY_DIM ** -0.5))
            ks.append(gk_ref[rs, cs])
            las.append(la_g[:, cs])
        b_all = _tri_dot_left(tril, jnp.concatenate(las, axis=1))
        for un in range(N_UNITS):
            b = b_all[:, un * LANES:(un + 1) * LANES]
            _pairwise_tiles(qs[un], ks[un], b, lhs_ref, (ck * N_UNITS + un) * L)
            units.append((qs[un], ks[un], b))

    r_all = jnp.dot(lhs_ref[...], sel_ref[...], preferred_element_type=F32)

    for ck in range(n_chunks):
        rs = slice(ck * L, (ck + 1) * L)
        for un in range(N_UNITS):
            uc = ck * N_UNITS + un
            q, k, b = units[uc]
            r = r_all[uc * L:(uc + 1) * L]
            if un < HGRN_HEADS:
                cs = slice(un * LANES, (un + 1) * LANES)
                (o,) = _unit_chunk(q, k, b, [hi_ref[rs, cs]], r, st_refs[un], masks)
                y_ref[rs, cs] = _rms_gate(o, hng_ref[:, cs], hg_ref[rs, cs]).astype(y_ref.dtype)
            else:
                p = un - HGRN_HEADS
                vcs = [slice((2 * p + i) * GLA_VAL_DIM, (2 * p + i + 1) * GLA_VAL_DIM) for i in range(2)]
                outs = _unit_chunk(q, k, b, [gv_ref[rs, vc] for vc in vcs], r, st_refs[un], masks)
                for o, vc in zip(outs, vcs):
                    ys = slice(HGRN_WIDTH + vc.start, HGRN_WIDTH + vc.stop)
                    y_ref[rs, ys] = _rms_gate(o, gng_ref[:, vc], gg_ref[rs, vc]).astype(y_ref.dtype)

    _project_residual_ln(x_ref, y_ref, wout_ref, lng_ref, lnb_ref, h_ref, hp_ref)


def _mixer_layer_call(kernel_fn, name, x2, batch, seq, rows, consts, scratch_shapes):
    n, d = x2.shape
    steps = seq // rows
    row_spec = lambda width: pl.BlockSpec((rows, width), lambda b, i: (b * steps + i, 0))
    next_row_spec = pl.BlockSpec((rows, d), lambda b, i: (b * steps + jnp.minimum(i + 1, steps - 1), 0))
    const_spec = lambda a: pl.BlockSpec(a.shape, lambda b, i: (0,) * a.ndim)
    return pl.pallas_call(
        kernel_fn,
        grid=(batch, steps),
        in_specs=[row_spec(d), next_row_spec] + [const_spec(a) for a in consts],
        out_specs=[row_spec(d), row_spec(d // 2)],
        out_shape=[jax.ShapeDtypeStruct((n, d), F32), jax.ShapeDtypeStruct((n, d // 2), I32)],
        scratch_shapes=scratch_shapes,
        compiler_params=_cparams("parallel", "arbitrary"),
        name=name,
    )(x2, x2, *consts)


def _hgrn_gla_layer(x2, batch, seq, w_in, lb, w_up, b_gk, hgrn_g, gla_g, w_out, ln_g, ln_b):
    R = MIX_ROWS
    n_uc = (R // LIN_CHUNK) * N_UNITS
    scratch = ([pltpu.VMEM((R, w_in.shape[1]), F32)] * 2
               + [pltpu.VMEM((R, w_out.shape[0]), BF16), pltpu.VMEM((n_uc * LIN_CHUNK, LIN_SUB * LANES), BF16)]
               + [pltpu.VMEM((GLA_VAL_DIM, LANES), F32)] * N_UNITS)
    consts = (w_in, lb, w_up, b_gk, hgrn_g, gla_g, _pair_selector(), w_out, ln_g, ln_b)
    return _mixer_layer_call(_hgrn_gla_kernel, "hgrn_gla", x2, batch, seq, R, consts, scratch)


MLSTM_CONV_PAD = 8


def _mlstm_kernel(x_ref, xn_ref, win_ref, cw_ref, bif_ref, wout_ref, lng_ref, lnb_ref, h_ref, hp_ref,
                  proj_a, proj_b, y_ref, ext_ref, *state_refs):
    def init_state():
        ext_ref[0:MLSTM_CONV_PAD, :] = jnp.zeros((MLSTM_CONV_PAD, ext_ref.shape[1]), F32)
        for r in state_refs:
            r[...] = jnp.zeros_like(r)

    def step(proj_ref):
        _mlstm_step(proj_ref, x_ref, cw_ref, bif_ref, wout_ref, lng_ref, lnb_ref, h_ref, hp_ref, y_ref, ext_ref,
                    state_refs)

    _ping_pong_projection(x_ref, xn_ref, win_ref, (proj_a, proj_b), init_state, step)


def _mlstm_step(proj_ref, x_ref, cw_ref, bif_ref, wout_ref, lng_ref, lnb_ref, h_ref, hp_ref, y_ref, ext_ref,
                state_refs):
    L = MLSTM_CHUNK
    PAD = MLSTM_CONV_PAD
    cn_refs, m_refs = state_refs[:MLSTM_HEADS], state_refs[MLSTM_HEADS:]
    QK2, VW = 2 * MLSTM_QK_WIDTH, MLSTM_V_WIDTH
    qk_ref, v_ref = _Cols(proj_ref, 0, QK2), _Cols(proj_ref, QK2, VW)
    og_ref, gt_ref = _Cols(proj_ref, QK2 + VW, VW), _Cols(proj_ref, QK2 + 2 * VW, LANES)

    ext_ref[PAD:PAD + L, :] = qk_ref[...]
    conv = jnp.zeros((L, ext_ref.shape[1]), F32)
    for w in range(MLSTM_CONV):
        conv = conv + cw_ref[w:w + 1, :] * ext_ref[pl.ds(PAD - (MLSTM_CONV - 1) + w, L), :]
    ext_ref[0:PAD, :] = ext_ref[L:L + PAD, :]
    qk = _silu(conv)

    gt = gt_ref[...] + bif_ref[...]
    gt_t = gt.T
    r_i = lax.broadcasted_iota(I32, (L, L), 0)
    c_i = lax.broadcasted_iota(I32, (L, L), 1)
    causal = c_i <= r_i
    tril = causal.astype(BF16)
    triu = (r_i <= c_i).astype(BF16)
    b_cols = _tri_dot_left(tril, _log_sigmoid(gt))
    b_rows = _tri_dot_right(_log_sigmoid(gt_t), triu)
    ones_col = (lax.broadcasted_iota(I32, (L, LANES), 1) == 0).astype(F32)

    H = MLSTM_HEADS
    for h in range(H):
        q = qk[:, h * MLSTM_QK_DIM:(h + 1) * MLSTM_QK_DIM].astype(BF16)
        kf = qk[:, MLSTM_QK_WIDTH + h * MLSTM_QK_DIM:MLSTM_QK_WIDTH + (h + 1) * MLSTM_QK_DIM] * (MLSTM_QK_DIM ** -0.5)
        vs = slice(h * MLSTM_V_DIM, (h + 1) * MLSTM_V_DIM)
        v_ext = jnp.concatenate([v_ref[:, vs], ones_col], axis=1).astype(BF16)
        bc, br = b_cols[:, H + h:H + h + 1], b_rows[H + h:H + h + 1, :]
        ic, ir = gt[:, h:h + 1], gt_t[h:h + 1, :]
        m_prev = m_refs[h][0:1, 0:1]
        g = bc[L - 1:L, :]
        dmat = jnp.where(causal, bc - br + ir, -jnp.inf)
        inter = bc + m_prev
        m_j = jnp.maximum(inter, jnp.max(dmat, axis=-1, keepdims=True))
        s = _dot_nt(q, kf.astype(BF16)) * jnp.exp(dmat - m_j)
        w_inter = jnp.exp(inter - m_j)
        cn = cn_refs[h][...]
        nd = (jnp.dot(s.astype(BF16), v_ext, preferred_element_type=F32)
              + w_inter * jnp.dot(q, cn.astype(BF16), preferred_element_type=F32))
        num, den = nd[:, :MLSTM_V_DIM], nd[:, MLSTM_V_DIM:MLSTM_V_DIM + 1]
        hid = num / jnp.maximum(jnp.abs(den), jnp.exp(-m_j))
        u = g - bc + ic
        m_new = jnp.maximum(g + m_prev, jnp.max(u, axis=0, keepdims=True))
        wk = (kf * jnp.exp(u - m_new)).astype(BF16)
        cn_refs[h][...] = jnp.exp(g + m_prev - m_new) * cn + _dot_tn(wk, v_ext)
        m_refs[h][...] = jnp.broadcast_to(m_new, m_refs[h].shape)
        y_ref[:, vs] = (hid * _sigmoid(og_ref[:, vs])).astype(y_ref.dtype)

    _project_residual_ln(x_ref, y_ref, wout_ref, lng_ref, lnb_ref, h_ref, hp_ref)


def _mlstm_layer(x2, batch, seq, w_in, conv_w, b_if, w_out, ln_g, ln_b):
    R = MLSTM_CHUNK
    scratch = ([pltpu.VMEM((R, w_in.shape[1]), F32)] * 2
               + [pltpu.VMEM((R, w_out.shape[0]), BF16), pltpu.VMEM((R + MLSTM_CONV_PAD, 2 * MLSTM_QK_WIDTH), F32)]
               + [pltpu.VMEM((MLSTM_QK_DIM, MLSTM_V_DIM + LANES), F32)] * MLSTM_HEADS
               + [pltpu.VMEM((8, LANES), F32)] * MLSTM_HEADS)
    consts = (w_in, conv_w, b_if, w_out, ln_g, ln_b)
    return _mixer_layer_call(_mlstm_kernel, "mlstm", x2, batch, seq, R, consts, scratch)


def _router_kernel(h_ref, whi_ref, wlo_ref, bias_ref, eidx_ref, wgt_ref, rank_ref, cnt_ref, carry_ref):
    TT, E, G, GS = ROUTE_TOK, N_EXPERTS, N_GROUPS, GROUP_SIZE

    @pl.when(pl.program_id(0) == 0)
    def _():
        carry_ref[...] = jnp.zeros_like(carry_ref)

    h = h_ref[...]
    h_hi = h.astype(BF16)
    h_lo = (h - h_hi.astype(F32)).astype(BF16)
    logits = _dot_nt(whi_ref[...], h_hi) + _dot_nt(whi_ref[...], h_lo) + _dot_nt(wlo_ref[...], h_hi)
    scores = _sigmoid(logits)
    biased = scores + bias_ref[:, 0:1]
    neg = -jnp.inf

    io_g = lax.broadcasted_iota(I32, (GS, TT), 0)
    io8 = lax.broadcasted_iota(I32, (G, TT), 0)
    gs = jnp.zeros((G, TT), F32)
    for g in range(G):
        blk = biased[g * GS:(g + 1) * GS, :]
        m1 = jnp.max(blk, axis=0, keepdims=True)
        i1 = jnp.min(jnp.where(blk == m1, io_g, GS), axis=0, keepdims=True)
        m2 = jnp.max(jnp.where(io_g == i1, neg, blk), axis=0, keepdims=True)
        gs = jnp.where(io8 == g, m1 + m2, gs)
    gsel = jnp.zeros((G, TT), F32)
    for _ in range(TOPK_GROUPS):
        m = jnp.max(gs, axis=0, keepdims=True)
        idx = jnp.min(jnp.where(gs == m, io8, G), axis=0, keepdims=True)
        hit = io8 == idx
        gsel = jnp.where(hit, 1.0, gsel)
        gs = jnp.where(hit, neg, gs)
    sel = jnp.concatenate([jnp.broadcast_to(gsel[g:g + 1, :], (GS, TT)) for g in range(G)], axis=0)
    masked = jnp.where(sel > 0.0, biased, neg)

    io_e = lax.broadcasted_iota(I32, (E, TT), 0)
    eidx = jnp.zeros((TOP_K, TT), I32)
    wsel = jnp.zeros((TOP_K, TT), F32)
    chosen = jnp.zeros((E, TT), F32)
    for k in range(TOP_K):
        m = jnp.max(masked, axis=0, keepdims=True)
        idx = jnp.min(jnp.where(masked == m, io_e, E), axis=0, keepdims=True)
        hit = io_e == idx
        sc = jnp.sum(jnp.where(hit, scores, 0.0), axis=0, keepdims=True)
        eidx = jnp.where(io8 == k, idx, eidx)
        wsel = jnp.where(io8 == k, sc, wsel)
        chosen = jnp.where(hit, 1.0, chosen)
        masked = jnp.where(hit, neg, masked)
    wgt_ref[...] = wsel / jnp.sum(wsel, axis=0, keepdims=True) * ROUTED_SCALE
    eidx_ref[...] = eidx

    triu = (lax.broadcasted_iota(I32, (TT, TT), 0) <= lax.broadcasted_iota(I32, (TT, TT), 1)).astype(BF16)
    cum = jnp.dot(chosen.astype(BF16), triu, preferred_element_type=F32)
    carry = carry_ref[:, 0:1]
    before = cum - chosen + carry
    rank = jnp.zeros((TOP_K, TT), F32)
    for k in range(TOP_K):
        r = jnp.sum(jnp.where(io_e == eidx[k:k + 1, :], before, 0.0), axis=0, keepdims=True)
        rank = jnp.where(io8 == k, r, rank)
    rank_ref[...] = rank.astype(I32)
    total = carry + cum[:, TT - 1:TT]
    carry_ref[...] = jnp.broadcast_to(total, carry_ref.shape)
    cnt_ref[...] = jnp.broadcast_to(total, cnt_ref.shape)


def _router(h2, wt_hi, wt_lo, bias_col):
    n, d = h2.shape
    TT, E = ROUTE_TOK, N_EXPERTS
    tok_spec = pl.BlockSpec((TOP_K, TT), lambda i: (0, i))
    return pl.pallas_call(
        _router_kernel,
        grid=(n // TT,),
        in_specs=[pl.BlockSpec((TT, d), lambda i: (i, 0)),
                  pl.BlockSpec((E, d), lambda i: (0, 0)),
                  pl.BlockSpec((E, d), lambda i: (0, 0)),
                  pl.BlockSpec((E, LANES), lambda i: (0, 0))],
        out_specs=[tok_spec, tok_spec, tok_spec, pl.BlockSpec((E, LANES), lambda i: (0, 0))],
        out_shape=[jax.ShapeDtypeStruct((TOP_K, n), I32), jax.ShapeDtypeStruct((TOP_K, n), F32),
                   jax.ShapeDtypeStruct((TOP_K, n), I32), jax.ShapeDtypeStruct((E, LANES), F32)],
        scratch_shapes=[pltpu.VMEM((E, LANES), F32)],
        compiler_params=_cparams("arbitrary"),
        name="router",
    )(h2, wt_hi, wt_lo, bias_col)


def _dest_kernel(eidx_ref, rank_ref, offs_ref, dest_ref):
    TT, E = eidx_ref.shape[1], N_EXPERTS
    io_e = lax.broadcasted_iota(I32, (E, TT), 0)
    io8 = lax.broadcasted_iota(I32, (TOP_K, TT), 0)
    offs = offs_ref[:, 0:1]
    base = jnp.zeros((TOP_K, TT), F32)
    for k in range(TOP_K):
        r = jnp.sum(jnp.where(io_e == eidx_ref[k:k + 1, :], offs, 0.0), axis=0, keepdims=True)
        base = jnp.where(io8 == k, r, base)
    dest_ref[...] = base.astype(I32) + rank_ref[...]


def _dest_rows(eidx, rank, offs_col):
    n = eidx.shape[1]
    TT = 512
    spec = pl.BlockSpec((TOP_K, TT), lambda i: (0, i))
    return pl.pallas_call(
        _dest_kernel,
        grid=(n // TT,),
        in_specs=[spec, spec, pl.BlockSpec((N_EXPERTS, LANES), lambda i: (0, 0))],
        out_specs=spec,
        out_shape=jax.ShapeDtypeStruct((TOP_K, n), I32),
        compiler_params=_cparams("parallel"),
        name="dest_rows",
    )(eidx, rank, offs_col)


def _sc_mesh():
    return plsc.VectorSubcoreMesh(core_axis_name="c", subcore_axis_name="s")


def _sc_worker_id():
    return lax.axis_index("s") * SC_CORES + lax.axis_index("c")


def _dispatch(dest_chunks, x2):
    n, d = x2.shape
    n_chunks = n // SC_ROWS // SC_WORKERS

    @functools.partial(
        pl.kernel, mesh=_sc_mesh(),
        out_type=jax.ShapeDtypeStruct((n * TOP_K, d), x2.dtype),
        scratch_types=[pltpu.VMEM((2, TOP_K, SC_ROWS), I32), pltpu.VMEM((2, SC_ROWS, d), x2.dtype),
                       pltpu.SemaphoreType.DMA((2,))],
    )
    def scatter_rows(x_hbm, idx_hbm, out_hbm, idx_v, rows_v, wsem):
        base = _sc_worker_id() * n_chunks

        def row_scatter(b, k):
            return pltpu.make_async_copy(rows_v.at[b], out_hbm.at[idx_v.at[b, k]], wsem.at[b])

        def drain(b):
            for k in range(TOP_K):
                row_scatter(b, k).wait()

        @pl.loop(0, n_chunks, step=2)
        def _(i):
            for b in range(2):
                c = base + i + b

                @pl.when(i > 0)
                def _():
                    drain(b)

                pltpu.sync_copy(idx_hbm.at[c], idx_v.at[b])
                pltpu.sync_copy(x_hbm.at[pl.ds(c * SC_ROWS, SC_ROWS)], rows_v.at[b])
                for k in range(TOP_K):
                    row_scatter(b, k).start()

        for b in range(2):
            drain(b)

    return scatter_rows(x2, dest_chunks)


def _gather_rows(src, idx):
    d = src.shape[1]
    p = idx.shape[0]
    per_w = p // SC_WORKERS
    n_chunks = per_w // SC_ROWS

    @functools.partial(
        pl.kernel, mesh=_sc_mesh(),
        out_type=jax.ShapeDtypeStruct((p, d), src.dtype),
        scratch_types=[pltpu.VMEM((2, SC_ROWS), I32), pltpu.VMEM((2, SC_ROWS, d), src.dtype),
                       pltpu.SemaphoreType.DMA((2,)), pltpu.SemaphoreType.DMA((2,))],
    )
    def gather_rows(src_hbm, idx_hbm, out_hbm, idx_v, rows_v, gsem, wsem):
        base = _sc_worker_id() * per_w

        def row_gather(b):
            return pltpu.make_async_copy(src_hbm.at[idx_v.at[b]], rows_v.at[b], gsem.at[b])

        def write_back(b, off):
            return pltpu.make_async_copy(rows_v.at[b], out_hbm.at[pl.ds(off, SC_ROWS)], wsem.at[b])

        @pl.loop(0, n_chunks, step=2)
        def _(i):
            for b in range(2):
                off = base + (i + b) * SC_ROWS

                @pl.when(i > 0)
                def _():
                    write_back(b, off).wait()

                pltpu.sync_copy(idx_hbm.at[pl.ds(off, SC_ROWS)], idx_v.at[b])
                row_gather(b).start()
            for b in range(2):
                row_gather(b).wait()
                write_back(b, base + (i + b) * SC_ROWS).start()

        for b in range(2):
            write_back(b, base).wait()

    return gather_rows(src, idx)


GMM_X_SLOTS = 3
GMM_O_SLOTS = 2


def _gmm_kernel(tile_ref, grp_ref, lo_ref, hi_ref, first_ref, last_ref, newgrp_ref, nextgrp_ref, nreal_ref,
                xs_hbm, win_hbm, wdn_hbm, ys_hbm,
                xbuf, obuf, winbuf, wdnbuf, win_bf, wdn_bf, xsem, osem, wsem, *, layer):
    TM = GMM_ROWS
    n_tiles = xs_hbm.shape[0] // TM
    half = xs_hbm.shape[1]

    def x_copy(t, slot):
        return pltpu.make_async_copy(xs_hbm.at[pl.ds(t * TM, TM)], xbuf.at[slot], xsem.at[slot])

    def o_copy(t, slot):
        return pltpu.make_async_copy(obuf.at[slot], ys_hbm.at[pl.ds(t * TM, TM)], osem.at[slot])

    def w_copies(e, slot):
        return (pltpu.make_async_copy(win_hbm.at[layer, e], winbuf.at[slot], wsem.at[0, slot]),
                pltpu.make_async_copy(wdn_hbm.at[layer, e], wdnbuf.at[slot], wsem.at[1, slot]))

    for t in range(GMM_X_SLOTS - 1):
        x_copy(t, t).start()
    for c in w_copies(grp_ref[0], 0):
        c.start()

    def visit(v, wslot):
        t = tile_ref[v]
        xslot = t % GMM_X_SLOTS
        oslot = t % GMM_O_SLOTS
        is_first = first_ref[v] == 1
        is_new = newgrp_ref[v] == 1
        wslot = jnp.where(is_new & (v > 0), 1 - wslot, wslot)

        @pl.when(is_first)
        def _():
            x_copy(t, xslot).wait()
            ahead = t + GMM_X_SLOTS - 1

            @pl.when(ahead < n_tiles)
            def _():
                x_copy(ahead, ahead % GMM_X_SLOTS).start()

            @pl.when(t >= GMM_O_SLOTS)
            def _():
                o_copy(t - GMM_O_SLOTS, oslot).wait()

        @pl.when(is_new)
        def _():
            for c in w_copies(grp_ref[v], wslot):
                c.wait()
            win_bf[...] = winbuf[wslot].astype(BF16)
            wdn_bf[...] = wdnbuf[wslot].astype(BF16)
            nxt = nextgrp_ref[v]

            @pl.when(nxt >= 0)
            def _():
                for c in w_copies(nxt, 1 - wslot):
                    c.start()

        x_lo, x_hi = _unpack_bf16_pair(xbuf[xslot])
        gu = (jnp.dot(x_lo, win_bf[:half, :], preferred_element_type=F32)
              + jnp.dot(x_hi, win_bf[half:, :], preferred_element_type=F32))
        act = (_silu(gu[:, :EXPERT_FF]) * gu[:, EXPERT_FF:]).astype(BF16)
        y = _pack_bf16_pair(jnp.dot(act, wdn_bf[...], preferred_element_type=F32))
        rows = lax.broadcasted_iota(I32, (TM, 1), 0)
        mine = (rows >= lo_ref[v]) & (rows < hi_ref[v])

        @pl.when(is_first)
        def _():
            obuf[oslot] = jnp.where(mine, y, 0)

        @pl.when(jnp.logical_not(is_first))
        def _():
            obuf[oslot] = jnp.where(mine, y, obuf[oslot])

        @pl.when(last_ref[v] == 1)
        def _():
            o_copy(t, oslot).start()

        return wslot

    lax.fori_loop(0, nreal_ref[0], visit, jnp.int32(0))
    for t in range(n_tiles - GMM_O_SLOTS, n_tiles):
        o_copy(t, t % GMM_O_SLOTS).wait()


def _gmm(meta, xs, w_in_all, w_dn_all, layer):
    p, half = xs.shape
    d = 2 * half
    TM = GMM_ROWS
    assert p % TM == 0 and p // TM >= GMM_X_SLOTS
    any_spec = pl.BlockSpec(memory_space=pl.ANY)
    grid_spec = pltpu.PrefetchScalarGridSpec(
        num_scalar_prefetch=len(meta),
        grid=(1,),
        in_specs=[any_spec, any_spec, any_spec],
        out_specs=any_spec,
        scratch_shapes=[pltpu.VMEM((GMM_X_SLOTS, TM, half), I32), pltpu.VMEM((GMM_O_SLOTS, TM, half), I32),
                        pltpu.VMEM((2, d, 2 * EXPERT_FF), F32), pltpu.VMEM((2, EXPERT_FF, d), F32),
                        pltpu.VMEM((d, 2 * EXPERT_FF), BF16), pltpu.VMEM((EXPERT_FF, d), BF16),
                        pltpu.SemaphoreType.DMA((GMM_X_SLOTS,)), pltpu.SemaphoreType.DMA((GMM_O_SLOTS,)),
                        pltpu.SemaphoreType.DMA((2, 2))],
    )
    return pl.pallas_call(
        functools.partial(_gmm_kernel, layer=layer),
        grid_spec=grid_spec,
        out_shape=jax.ShapeDtypeStruct((p, half), I32),
        compiler_params=_cparams("arbitrary"),
        name="gmm",
    )(*meta, xs, w_in_all, w_dn_all)


def _gmm_schedule(counts, n_rows):
    TM = GMM_ROWS
    E = counts.shape[0]
    max_visits = n_rows // TM + E - 1
    ends = jnp.cumsum(counts)
    starts = ends - counts
    first_tile = starts // TM
    n_vis = jnp.where(counts > 0, jnp.maximum(ends - 1, 0) // TM - first_tile + 1, 0)
    vis_end = jnp.cumsum(n_vis)
    vis_start = vis_end - n_vis
    n_real = vis_end[-1]
    idx = jnp.arange(max_visits, dtype=I32)
    v = jnp.minimum(idx, jnp.maximum(n_real - 1, 0))
    real = idx < n_real
    grp = jnp.minimum(jnp.sum((vis_end[None, :] <= v[:, None]).astype(I32), axis=1), E - 1)
    onehot = grp[:, None] == jnp.arange(E, dtype=I32)[None, :]

    def take(table):
        return jnp.sum(jnp.where(onehot, table[None, :], 0), axis=1)

    tile = take(first_tile) + (v - take(vis_start))
    lo = jnp.where(real, jnp.maximum(take(starts), tile * TM) - tile * TM, 0)
    hi = jnp.where(real, jnp.minimum(take(ends), (tile + 1) * TM) - tile * TM, 0)
    prev = lambda a: jnp.concatenate([jnp.full((1,), -1, I32), a[:-1]])
    first = real & (tile != prev(tile))
    newgrp = real & (grp != prev(grp))
    last = real & (jnp.concatenate([first[1:], jnp.ones((1,), bool)]) | (idx == n_real - 1))
    cand = jnp.where(counts > 0, jnp.arange(E, dtype=I32), E)
    later = jnp.concatenate([jnp.flip(lax.cummin(jnp.flip(cand)))[1:], jnp.full((1,), E, I32)])
    nextgrp = take(jnp.where(later < E, later, -1))
    meta = tuple(a.astype(I32) for a in (tile, grp, lo, hi, first, last, newgrp, nextgrp))
    return meta + (n_real.astype(I32).reshape(1),)


def _combine_kernel(wgt_ref, h_ref, yg_ref, wsi_ref, wsd_ref, g_ref, b_ref, *rest):
    o_ref = rest[-1]
    h = h_ref[...]
    gu = jnp.dot(h.astype(BF16), wsi_ref[...], preferred_element_type=F32)
    act = (_silu(gu[:, :SHARED_FF]) * gu[:, SHARED_FF:]).astype(BF16)
    acc = ALPHA * h + jnp.dot(act, wsd_ref[...], preferred_element_type=F32)
    wgt = wgt_ref[...]
    half = yg_ref.shape[2]
    r_lo = jnp.zeros((h.shape[0], half), F32)
    r_hi = jnp.zeros((h.shape[0], half), F32)
    for k in range(TOP_K):
        y_lo, y_hi = _unpack_bf16_pair(yg_ref[k])
        r_lo = r_lo + wgt[:, k:k + 1] * y_lo.astype(F32)
        r_hi = r_hi + wgt[:, k:k + 1] * y_hi.astype(F32)
    acc = acc + jnp.concatenate([r_lo, r_hi], axis=1)
    o_ref[...] = _layernorm(acc, g_ref[...], b_ref[...])


def _combine(wgt_tk, h2, yg_part, part, out_prev, w_si, w_sd, g, b):
    n, d = h2.shape
    TT = COMBINE_TOK
    steps = n // COMBINE_PARTS // TT
    first = part * steps
    const2 = lambda shape: pl.BlockSpec(shape, lambda i: (0, 0))
    in_specs = [pl.BlockSpec((TT, TOP_K), lambda i: (first + i, 0)),
                pl.BlockSpec((TT, d), lambda i: (first + i, 0)),
                pl.BlockSpec((TOP_K, TT, d // 2), lambda i: (0, i, 0)),
                const2((d, 2 * SHARED_FF)), const2((SHARED_FF, d)), const2((1, d)), const2((1, d))]
    args = [wgt_tk, h2, yg_part, w_si, w_sd, g, b]
    aliases = {}
    if out_prev is not None:
        in_specs.append(pl.BlockSpec(memory_space=pl.ANY))
        aliases = {len(args): 0}
        args.append(out_prev)
    return pl.pallas_call(
        _combine_kernel,
        grid=(steps,),
        in_specs=in_specs,
        out_specs=pl.BlockSpec((TT, d), lambda i: (first + i, 0)),
        out_shape=jax.ShapeDtypeStruct((n, d), F32),
        input_output_aliases=aliases,
        compiler_params=_cparams("parallel"),
        name="combine",
    )(*args)


def _moe_ln(h2, h2_packed, w_router, r_bias, w_e_in_all, w_e_dn_all, layer, w_s_in, w_s_dn, ln_g, ln_b):
    n, d = h2.shape
    wt = w_router.T
    wt_hi = wt.astype(BF16)
    wt_lo = (wt - wt_hi.astype(F32)).astype(BF16)
    bias_col = jnp.broadcast_to(r_bias.astype(F32)[:, None], (N_EXPERTS, LANES))
    eidx, wgt, rank, cnt = _router(h2, wt_hi, wt_lo, bias_col)
    counts = cnt[:, 0].astype(I32)
    offs = jnp.cumsum(counts) - counts
    offs_col = jnp.broadcast_to(offs.astype(F32)[:, None], (N_EXPERTS, LANES))
    dest = _dest_rows(eidx, rank, offs_col)
    dest_chunks = dest.reshape(TOP_K, n // SC_ROWS, SC_ROWS).transpose(1, 0, 2)
    xs = _dispatch(dest_chunks, h2_packed)
    ys = _gmm(_gmm_schedule(counts, n * TOP_K), xs, w_e_in_all, w_e_dn_all, layer)
    wgt_tk, w_si, w_sd = wgt.T, w_s_in.astype(BF16), w_s_dn.astype(BF16)
    n_part = n // COMBINE_PARTS
    out = None
    for part in range(COMBINE_PARTS):
        idx = dest[:, part * n_part:(part + 1) * n_part].reshape(-1)
        yg_part = _gather_rows(ys, idx).reshape(TOP_K, n_part, d // 2)
        out = _combine(wgt_tk, h2, yg_part, part, out, w_si, w_sd, ln_g[None, :], ln_b[None, :])
    return out


def _pad_cols(w, width):
    return jnp.pad(w, ((0, 0), (0, width - w.shape[1])))


def kernel(x, w_in_ab, w_gla_gate_up, b_gla_gate, hgrn_norm_g, gla_norm_g, w_out_ab, hgrn_lb_logits, w_in_c, conv_c, b_if_c, w_out_c, w_router, router_bias, w_exp_in, w_exp_down, w_sh_in, w_sh_down, ln_mix_g, ln_mix_b, ln_ffn_g, ln_ffn_b):
    B, T, D = x.shape
    lower_bounds = jnp.cumsum(jax.nn.softmax(hgrn_lb_logits.astype(F32), axis=0), axis=0)
    h = x.reshape(B * T, D)
    for l in range(DEPTH):
        j = l // 2
        ln_g, ln_b = ln_mix_g[l][None, :], ln_mix_b[l][None, :]
        if l % 2 == 0:
            w_in = _pad_cols(w_in_ab[j], AB_WIDTH - GLA_GATE_RANK + LANES).astype(BF16)
            w_up = jnp.pad(w_gla_gate_up[j], ((0, LANES - GLA_GATE_RANK), (0, 0))).astype(BF16)
            h, h_packed = _hgrn_gla_layer(h, B, T, w_in, lower_bounds[l][None, :], w_up, b_gla_gate[j][None, :],
                                          hgrn_norm_g[j][None, :], gla_norm_g[j][None, :],
                                          w_out_ab[j].astype(BF16), ln_g, ln_b)
        else:
            w_in = _pad_cols(w_in_c[j], C_WIDTH - 2 * MLSTM_HEADS + LANES).astype(BF16)
            b_if = jnp.pad(b_if_c[j].astype(F32), (0, LANES - 2 * MLSTM_HEADS))[None, :]
            h, h_packed = _mlstm_layer(h, B, T, w_in, conv_c[j], b_if, w_out_c[j].astype(BF16), ln_g, ln_b)
        h = _moe_ln(h, h_packed, w_router[l], router_bias[l], w_exp_in, w_exp_down, l, w_sh_in[l], w_sh_down[l],
                    ln_ffn_g[l], ln_ffn_b[l])
    return h.reshape(B, T, D)
```

```python
import functools

import jax
import jax.numpy as jnp
from jax import lax
from jax.experimental import pallas as pl
from jax.experimental.pallas import tpu as pltpu
from jax.experimental.pallas import tpu_sc as plsc

F32 = jnp.float32
BF16 = jnp.bfloat16
I32 = jnp.int32

D_MODEL = 1024
DEPTH = 2
HGRN_HEADS = 4
HGRN_HEAD_DIM = 128
HGRN_WIDTH = HGRN_HEADS * HGRN_HEAD_DIM
GLA_HEADS = 4
GLA_KEY_DIM = 64
GLA_VAL_DIM = 128
GLA_KEY_WIDTH = GLA_HEADS * GLA_KEY_DIM
GLA_VAL_WIDTH = GLA_HEADS * GLA_VAL_DIM
GLA_GATE_RANK = 16
GLA_GATE_NORMALIZER = 16.0
LIN_CHUNK = 64
LIN_SUB = 16
AB_WIDTH = 4 * HGRN_WIDTH + 2 * GLA_KEY_WIDTH + 2 * GLA_VAL_WIDTH + GLA_GATE_RANK
MLSTM_HEADS = 4
MLSTM_QK_DIM = 128
MLSTM_V_DIM = 256
MLSTM_QK_WIDTH = MLSTM_HEADS * MLSTM_QK_DIM
MLSTM_V_WIDTH = MLSTM_HEADS * MLSTM_V_DIM
MLSTM_CONV = 4
MLSTM_CHUNK = 128
C_WIDTH = 2 * MLSTM_QK_WIDTH + 2 * MLSTM_V_WIDTH + 2 * MLSTM_HEADS
N_EXPERTS = 256
TOP_K = 8
N_GROUPS = 8
TOPK_GROUPS = 4
GROUP_SIZE = N_EXPERTS // N_GROUPS
EXPERT_FF = 256
SHARED_FF = 256
ROUTED_SCALE = 2.5
ALPHA = (2 * DEPTH) ** 0.25
LN_EPS = 1e-5
RMS_EPS = 1e-6

LANES = 128
VMEM_LIMIT = 56 * 1024 * 1024

MIX_ROWS = 256
ROUTE_TOK = 256
COMBINE_TOK = 512
GMM_ROWS = 256
SC_CORES, SC_SUBCORES = 2, 16
SC_WORKERS = SC_CORES * SC_SUBCORES
SC_ROWS = 64


def _cparams(*sem):
    return pltpu.CompilerParams(dimension_semantics=sem, vmem_limit_bytes=VMEM_LIMIT)


def _sigmoid(x):
    return 1.0 / (1.0 + jnp.exp(-x))


def _silu(x):
    return x * _sigmoid(x)


def _log_sigmoid(x):
    return jnp.minimum(x, 0.0) - jnp.log(1.0 + jnp.exp(-jnp.abs(x)))


def _split3(x):
    hi = x.astype(BF16)
    r = x - hi.astype(F32)
    mid = r.astype(BF16)
    lo = (r - mid.astype(F32)).astype(BF16)
    return hi, mid, lo


def _tri_dot_left(tri, x):
    return sum(jnp.dot(tri, p, preferred_element_type=F32) for p in _split3(x))


def _tri_dot_right(x, tri):
    return sum(jnp.dot(p, tri, preferred_element_type=F32) for p in _split3(x))


def _dot_nt(a, b):
    return lax.dot_general(a, b, (((1,), (1,)), ((), ())), preferred_element_type=F32)


def _dot_tn(a, b):
    return lax.dot_general(a, b, (((0,), (0,)), ((), ())), preferred_element_type=F32)


HI16 = -65536


def _pack_bf16_pair(x):
    c = x.shape[1] // 2
    bits = lambda v: lax.bitcast_convert_type(v.astype(BF16).astype(F32), I32)
    return (bits(x[:, c:]) & HI16) | lax.shift_right_logical(bits(x[:, :c]), 16)


def _unpack_bf16_pair(w):
    lo = lax.bitcast_convert_type(lax.shift_left(w, 16), F32)
    hi = lax.bitcast_convert_type(w & HI16, F32)
    return lo.astype(BF16), hi.astype(BF16)


def _layernorm(x, g, b):
    mu = jnp.mean(x, axis=-1, keepdims=True)
    xc = x - mu
    var = jnp.mean(xc * xc, axis=-1, keepdims=True)
    return xc * lax.rsqrt(var + LN_EPS) * g + b


N_UNITS = HGRN_HEADS + GLA_HEADS // 2
HALF_LANES = LANES // 2
LOG2E = 1.4426950408889634


def _pair_selector():
    assert LIN_CHUNK == HALF_LANES == GLA_KEY_DIM
    r = jnp.arange(LIN_SUB * LANES)[:, None]
    c = jnp.arange(LANES)[None, :]
    same_slot = (r // LANES) == (c % LIN_SUB)
    same_half = ((r % LANES) < HALF_LANES) == (c < HALF_LANES)
    return (same_slot & same_half).astype(BF16)


def _pairwise_tiles(q, k, b, lhs_ref, row0):
    c = LIN_SUB
    b2 = b * LOG2E
    for blk in range(LIN_CHUNK // c):
        lo = blk * c
        qb, kb, bb = q[lo:lo + c], k[lo:lo + c], b2[lo:lo + c]
        for j in range(c):
            e = jnp.exp2(jnp.minimum(bb - bb[j:j + 1], 0.0))
            lhs_ref[row0 + lo:row0 + lo + c, j * LANES:(j + 1) * LANES] = (qb * kb[j:j + 1] * e).astype(BF16)


def _unit_chunk(q, k, b, vs, r, st_ref, masks):
    L, c = LIN_CHUNK, LIN_SUB
    lane_lo, col_mod, same_blk_causal = masks
    heads = [lane_lo, ~lane_lo] if len(vs) == 2 else [None]

    def pick(x, m):
        return x if m is None else jnp.where(m, x, jnp.zeros_like(x))

    g = b[L - 1:L, :]
    st = st_ref[...]
    st_b = st.astype(BF16)
    qx = (q * jnp.exp(b)).astype(BF16)
    outs = [_dot_nt(pick(qx, m), st_b) for m in heads]
    k_end = (k * jnp.exp(g - b)).astype(BF16)
    upd = [_dot_tn(v.astype(BF16), k_end) for v in vs]
    st_ref[...] = st * jnp.exp(g) + (upd[0] if len(vs) == 1 else jnp.where(lane_lo, upd[0], upd[1]))

    off_rows = [jnp.zeros((c, LANES), F32)]
    for blk in range(1, L // c):
        lo = blk * c
        ref = b[lo - 1:lo]
        q_in = (q[lo:lo + c] * jnp.exp(b[lo:lo + c] - ref)).astype(BF16)
        k_in = (k * jnp.exp(jnp.minimum(ref - b, 0.0))).astype(BF16)
        if len(vs) == 2:
            k_cat = jnp.concatenate([pick(k_in, heads[0]), pick(k_in, heads[1])], axis=0)
        else:
            k_cat = jnp.concatenate([k_in, jnp.zeros_like(k_in)], axis=0)
        off_rows.append(jnp.where(col_mod < lo, _dot_nt(q_in, k_cat), 0.0))
    a = jnp.where(same_blk_causal, r, jnp.concatenate(off_rows, axis=0)).astype(BF16)
    v_cat = jnp.concatenate([vs[0], vs[-1]], axis=0).astype(BF16)
    return [o + jnp.dot(pick(a, m), v_cat, preferred_element_type=F32) for o, m in zip(outs, heads)]


def _rms_gate(o, g, gate):
    o = o * lax.rsqrt(jnp.mean(o * o, axis=-1, keepdims=True) + RMS_EPS)
    return o * g * _silu(gate)


class _Cols:
    def __init__(self, ref, off, width):
        self.ref, self.off, self.width = ref, off, width

    def __getitem__(self, idx):
        rows, cols = (slice(None), slice(None)) if idx is Ellipsis else idx
        lo = self.off + (cols.start or 0)
        hi = self.off + (self.width if cols.stop is None else cols.stop)
        return self.ref[rows, lo:hi]


def _project_residual_ln(x_ref, y_ref, wout_ref, g_ref, b_ref, h_ref, hp_ref):
    mix = jnp.dot(y_ref[...], wout_ref[...], preferred_element_type=F32)
    h = _layernorm(ALPHA * x_ref[...] + mix, g_ref[...], b_ref[...])
    h_ref[...] = h
    hp_ref[...] = _pack_bf16_pair(h)


def _hgrn_gla_kernel(x_ref, win_ref, lb_ref, wup_ref, bgk_ref, hng_ref, gng_ref, sel_ref, wout_ref, lng_ref, lnb_ref,
                     h_ref, hp_ref, proj_ref, y_ref, lhs_ref, *st_refs):
    @pl.when(pl.program_id(1) == 0)
    def _():
        for r in st_refs:
            r[...] = jnp.zeros_like(r)

    proj_ref[...] = jnp.dot(x_ref[...].astype(BF16), win_ref[...], preferred_element_type=F32)
    W, KW, VW = HGRN_WIDTH, GLA_KEY_WIDTH, GLA_VAL_WIDTH
    hq_ref, hf_ref, hi_ref, hg_ref = (_Cols(proj_ref, i * W, W) for i in range(4))
    gq_ref, gk_ref = _Cols(proj_ref, 4 * W, KW), _Cols(proj_ref, 4 * W + KW, KW)
    gv_ref, gg_ref = _Cols(proj_ref, 4 * W + 2 * KW, VW), _Cols(proj_ref, 4 * W + 2 * KW + VW, VW)
    glr_ref = _Cols(proj_ref, AB_WIDTH - GLA_GATE_RANK, LANES)

    L = LIN_CHUNK
    n_chunks = MIX_ROWS // L
    r_i = lax.broadcasted_iota(I32, (L, LANES), 0)
    c_i = lax.broadcasted_iota(I32, (L, LANES), 1)
    col_mod = jnp.where(c_i < HALF_LANES, c_i, c_i - HALF_LANES)
    same_blk_causal = (col_mod // LIN_SUB == r_i // LIN_SUB) & (col_mod <= r_i)
    c_sub = lax.broadcasted_iota(I32, (LIN_SUB, LANES), 1)
    col_mod_sub = jnp.where(c_sub < HALF_LANES, c_sub, c_sub - HALF_LANES)
    lane_lo = lax.broadcasted_iota(I32, (1, LANES), 1) < HALF_LANES
    masks = (lane_lo, col_mod_sub, same_blk_causal)
    tril = (lax.broadcasted_iota(I32, (L, L), 0) >= lax.broadcasted_iota(I32, (L, L), 1)).astype(BF16)
    lb = lb_ref[...]

    units = []
    for ck in range(n_chunks):
        rs = slice(ck * L, (ck + 1) * L)
        u = jnp.dot(glr_ref[rs, :].astype(BF16), wup_ref[...], preferred_element_type=F32) + bgk_ref[...]
        la_g = _log_sigmoid(u) * (1.0 / GLA_GATE_NORMALIZER)
        qs, ks, las = [], [], []
        for h in range(HGRN_HEADS):
            cs = slice(h * LANES, (h + 1) * LANES)
            z = hf_ref[rs, cs]
            lbh = lb[:, cs]
            qs.append(_silu(hq_ref[rs, cs]))
            ks.append((1.0 - lbh) * _sigmoid(-z))
            las.append(jnp.log(lbh + (1.0 - lbh) * _sigmoid(z)))
        for p in range(GLA_HEADS // 2):
            cs = slice(p * LANES, (p + 1) * LANES)
            qs.append(gq_ref[rs, cs] * (GLA_KEY_DIM ** -0.5))
            ks.append(gk_ref[rs, cs])
            las.append(la_g[:, cs])
        b_all = _tri_dot_left(tril, jnp.concatenate(las, axis=1))
        for un in range(N_UNITS):
            b = b_all[:, un * LANES:(un + 1) * LANES]
            _pairwise_tiles(qs[un], ks[un], b, lhs_ref, (ck * N_UNITS + un) * L)
            units.append((qs[un], ks[un], b))

    r_all = jnp.dot(lhs_ref[...], sel_ref[...], preferred_element_type=F32)

    for ck in range(n_chunks):
        rs = slice(ck * L, (ck + 1) * L)
        for un in range(N_UNITS):
            uc = ck * N_UNITS + un
            q, k, b = units[uc]
            r = r_all[uc * L:(uc + 1) * L]
            if un < HGRN_HEADS:
                cs = slice(un * LANES, (un + 1) * LANES)
                (o,) = _unit_chunk(q, k, b, [hi_ref[rs, cs]], r, st_refs[un], masks)
                y_ref[rs, cs] = _rms_gate(o, hng_ref[:, cs], hg_ref[rs, cs]).astype(y_ref.dtype)
            else:
                p = un - HGRN_HEADS
                vcs = [slice((2 * p + i) * GLA_VAL_DIM, (2 * p + i + 1) * GLA_VAL_DIM) for i in range(2)]
                outs = _unit_chunk(q, k, b, [gv_ref[rs, vc] for vc in vcs], r, st_refs[un], masks)
                for o, vc in zip(outs, vcs):
                    ys = slice(HGRN_WIDTH + vc.start, HGRN_WIDTH + vc.stop)
                    y_ref[rs, ys] = _rms_gate(o, gng_ref[:, vc], gg_ref[rs, vc]).astype(y_ref.dtype)

    _project_residual_ln(x_ref, y_ref, wout_ref, lng_ref, lnb_ref, h_ref, hp_ref)


def _mixer_layer_call(kernel_fn, name, x2, batch, seq, consts, scratch_shapes):
    n, d = x2.shape
    rows = MIX_ROWS
    steps = seq // rows
    row_spec = lambda width: pl.BlockSpec((rows, width), lambda b, i: (b * steps + i, 0))
    const_spec = lambda a: pl.BlockSpec(a.shape, lambda b, i: (0,) * a.ndim)
    return pl.pallas_call(
        kernel_fn,
        grid=(batch, steps),
        in_specs=[row_spec(d)] + [const_spec(a) for a in consts],
        out_specs=[row_spec(d), row_spec(d // 2)],
        out_shape=[jax.ShapeDtypeStruct((n, d), F32), jax.ShapeDtypeStruct((n, d // 2), I32)],
        scratch_shapes=scratch_shapes,
        compiler_params=_cparams("parallel", "arbitrary"),
        name=name,
    )(x2, *consts)


def _hgrn_gla_layer(x2, batch, seq, w_in, lb, w_up, b_gk, hgrn_g, gla_g, w_out, ln_g, ln_b):
    R = MIX_ROWS
    n_uc = (R // LIN_CHUNK) * N_UNITS
    scratch = ([pltpu.VMEM((R, w_in.shape[1]), F32), pltpu.VMEM((R, w_out.shape[0]), BF16),
                pltpu.VMEM((n_uc * LIN_CHUNK, LIN_SUB * LANES), BF16)]
               + [pltpu.VMEM((GLA_VAL_DIM, LANES), F32)] * N_UNITS)
    consts = (w_in, lb, w_up, b_gk, hgrn_g, gla_g, _pair_selector(), w_out, ln_g, ln_b)
    return _mixer_layer_call(_hgrn_gla_kernel, "hgrn_gla", x2, batch, seq, consts, scratch)


MLSTM_CONV_PAD = 8


def _mlstm_kernel(x_ref, win_ref, cw_ref, bif_ref, wout_ref, lng_ref, lnb_ref, h_ref, hp_ref,
                  proj_ref, y_ref, ext_ref, *state_refs):
    L, R, PAD = MLSTM_CHUNK, MIX_ROWS, MLSTM_CONV_PAD
    cn_refs, m_refs = state_refs[:MLSTM_HEADS], state_refs[MLSTM_HEADS:]

    @pl.when(pl.program_id(1) == 0)
    def _():
        ext_ref[0:PAD, :] = jnp.zeros((PAD, ext_ref.shape[1]), F32)
        for r in state_refs:
            r[...] = jnp.zeros_like(r)

    proj_ref[...] = jnp.dot(x_ref[...].astype(BF16), win_ref[...], preferred_element_type=F32)
    QK2, VW = 2 * MLSTM_QK_WIDTH, MLSTM_V_WIDTH
    qk_ref, v_ref = _Cols(proj_ref, 0, QK2), _Cols(proj_ref, QK2, VW)
    og_ref, gt_ref = _Cols(proj_ref, QK2 + VW, VW), _Cols(proj_ref, QK2 + 2 * VW, LANES)

    ext_ref[PAD:PAD + R, :] = qk_ref[...]
    conv = jnp.zeros((R, ext_ref.shape[1]), F32)
    for w in range(MLSTM_CONV):
        conv = conv + cw_ref[w:w + 1, :] * ext_ref[pl.ds(PAD - (MLSTM_CONV - 1) + w, R), :]
    ext_ref[0:PAD, :] = ext_ref[R:R + PAD, :]
    qk_all = _silu(conv)

    r_i = lax.broadcasted_iota(I32, (L, L), 0)
    c_i = lax.broadcasted_iota(I32, (L, L), 1)
    causal = c_i <= r_i
    tril = causal.astype(BF16)
    triu = (r_i <= c_i).astype(BF16)
    ones_col = (lax.broadcasted_iota(I32, (L, LANES), 1) == 0).astype(F32)

    H = MLSTM_HEADS
    for ck in range(R // L):
        rs = slice(ck * L, (ck + 1) * L)
        qk = qk_all[rs]
        gt = gt_ref[rs, :] + bif_ref[...]
        gt_t = gt.T
        b_cols = _tri_dot_left(tril, _log_sigmoid(gt))
        b_rows = _tri_dot_right(_log_sigmoid(gt_t), triu)
        for h in range(H):
            q = qk[:, h * MLSTM_QK_DIM:(h + 1) * MLSTM_QK_DIM].astype(BF16)
            kf = qk[:, MLSTM_QK_WIDTH + h * MLSTM_QK_DIM:MLSTM_QK_WIDTH + (h + 1) * MLSTM_QK_DIM] * (MLSTM_QK_DIM ** -0.5)
            vs = slice(h * MLSTM_V_DIM, (h + 1) * MLSTM_V_DIM)
            v_ext = jnp.concatenate([v_ref[rs, vs], ones_col], axis=1).astype(BF16)
            bc, br = b_cols[:, H + h:H + h + 1], b_rows[H + h:H + h + 1, :]
            ic, ir = gt[:, h:h + 1], gt_t[h:h + 1, :]
            m_prev = m_refs[h][0:1, 0:1]
            g = bc[L - 1:L, :]
            dmat = jnp.where(causal, bc - br + ir, -jnp.inf)
            inter = bc + m_prev
            m_j = jnp.maximum(inter, jnp.max(dmat, axis=-1, keepdims=True))
            s = _dot_nt(q, kf.astype(BF16)) * jnp.exp(dmat - m_j)
            w_inter = jnp.exp(inter - m_j)
            cn = cn_refs[h][...]
            nd = (jnp.dot(s.astype(BF16), v_ext, preferred_element_type=F32)
                  + w_inter * jnp.dot(q, cn.astype(BF16), preferred_element_type=F32))
            num, den = nd[:, :MLSTM_V_DIM], nd[:, MLSTM_V_DIM:MLSTM_V_DIM + 1]
            hid = num / jnp.maximum(jnp.abs(den), jnp.exp(-m_j))
            u = g - bc + ic
            m_new = jnp.maximum(g + m_prev, jnp.max(u, axis=0, keepdims=True))
            wk = (kf * jnp.exp(u - m_new)).astype(BF16)
            cn_refs[h][...] = jnp.exp(g + m_prev - m_new) * cn + _dot_tn(wk, v_ext)
            m_refs[h][...] = jnp.broadcast_to(m_new, m_refs[h].shape)
            y_ref[rs, vs] = (hid * _sigmoid(og_ref[rs, vs])).astype(y_ref.dtype)

    _project_residual_ln(x_ref, y_ref, wout_ref, lng_ref, lnb_ref, h_ref, hp_ref)


def _mlstm_layer(x2, batch, seq, w_in, conv_w, b_if, w_out, ln_g, ln_b):
    R = MIX_ROWS
    assert R % MLSTM_CHUNK == 0
    scratch = ([pltpu.VMEM((R, w_in.shape[1]), F32), pltpu.VMEM((R, w_out.shape[0]), BF16),
                pltpu.VMEM((R + MLSTM_CONV_PAD, 2 * MLSTM_QK_WIDTH), F32)]
               + [pltpu.VMEM((MLSTM_QK_DIM, MLSTM_V_DIM + LANES), F32)] * MLSTM_HEADS
               + [pltpu.VMEM((8, LANES), F32)] * MLSTM_HEADS)
    consts = (w_in, conv_w, b_if, w_out, ln_g, ln_b)
    return _mixer_layer_call(_mlstm_kernel, "mlstm", x2, batch, seq, consts, scratch)


def _router_kernel(h_ref, whi_ref, wlo_ref, bias_ref, eidx_ref, wgt_ref, rank_ref, cnt_ref, carry_ref):
    TT, E, G, GS = ROUTE_TOK, N_EXPERTS, N_GROUPS, GROUP_SIZE

    @pl.when(pl.program_id(0) == 0)
    def _():
        carry_ref[...] = jnp.zeros_like(carry_ref)

    h = h_ref[...]
    h_hi = h.astype(BF16)
    h_lo = (h - h_hi.astype(F32)).astype(BF16)
    logits = _dot_nt(whi_ref[...], h_hi) + _dot_nt(whi_ref[...], h_lo) + _dot_nt(wlo_ref[...], h_hi)
    scores = _sigmoid(logits)
    biased = scores + bias_ref[:, 0:1]
    neg = -jnp.inf

    io_g = lax.broadcasted_iota(I32, (GS, TT), 0)
    io8 = lax.broadcasted_iota(I32, (G, TT), 0)
    gs = jnp.zeros((G, TT), F32)
    for g in range(G):
        blk = biased[g * GS:(g + 1) * GS, :]
        m1 = jnp.max(blk, axis=0, keepdims=True)
        i1 = jnp.min(jnp.where(blk == m1, io_g, GS), axis=0, keepdims=True)
        m2 = jnp.max(jnp.where(io_g == i1, neg, blk), axis=0, keepdims=True)
        gs = jnp.where(io8 == g, m1 + m2, gs)
    gsel = jnp.zeros((G, TT), F32)
    for _ in range(TOPK_GROUPS):
        m = jnp.max(gs, axis=0, keepdims=True)
        idx = jnp.min(jnp.where(gs == m, io8, G), axis=0, keepdims=True)
        hit = io8 == idx
        gsel = jnp.where(hit, 1.0, gsel)
        gs = jnp.where(hit, neg, gs)
    sel = jnp.concatenate([jnp.broadcast_to(gsel[g:g + 1, :], (GS, TT)) for g in range(G)], axis=0)
    masked = jnp.where(sel > 0.0, biased, neg)

    io_e = lax.broadcasted_iota(I32, (E, TT), 0)
    eidx = jnp.zeros((TOP_K, TT), I32)
    wsel = jnp.zeros((TOP_K, TT), F32)
    chosen = jnp.zeros((E, TT), F32)
    for k in range(TOP_K):
        m = jnp.max(masked, axis=0, keepdims=True)
        idx = jnp.min(jnp.where(masked == m, io_e, E), axis=0, keepdims=True)
        hit = io_e == idx
        sc = jnp.sum(jnp.where(hit, scores, 0.0), axis=0, keepdims=True)
        eidx = jnp.where(io8 == k, idx, eidx)
        wsel = jnp.where(io8 == k, sc, wsel)
        chosen = jnp.where(hit, 1.0, chosen)
        masked = jnp.where(hit, neg, masked)
    wgt_ref[...] = wsel / jnp.sum(wsel, axis=0, keepdims=True) * ROUTED_SCALE
    eidx_ref[...] = eidx

    triu = (lax.broadcasted_iota(I32, (TT, TT), 0) <= lax.broadcasted_iota(I32, (TT, TT), 1)).astype(BF16)
    cum = jnp.dot(chosen.astype(BF16), triu, preferred_element_type=F32)
    carry = carry_ref[:, 0:1]
    before = cum - chosen + carry
    rank = jnp.zeros((TOP_K, TT), F32)
    for k in range(TOP_K):
        r = jnp.sum(jnp.where(io_e == eidx[k:k + 1, :], before, 0.0), axis=0, keepdims=True)
        rank = jnp.where(io8 == k, r, rank)
    rank_ref[...] = rank.astype(I32)
    total = carry + cum[:, TT - 1:TT]
    carry_ref[...] = jnp.broadcast_to(total, carry_ref.shape)
    cnt_ref[...] = jnp.broadcast_to(total, cnt_ref.shape)


def _router(h2, wt_hi, wt_lo, bias_col):
    n, d = h2.shape
    TT, E = ROUTE_TOK, N_EXPERTS
    tok_spec = pl.BlockSpec((TOP_K, TT), lambda i: (0, i))
    return pl.pallas_call(
        _router_kernel,
        grid=(n // TT,),
        in_specs=[pl.BlockSpec((TT, d), lambda i: (i, 0)),
                  pl.BlockSpec((E, d), lambda i: (0, 0)),
                  pl.BlockSpec((E, d), lambda i: (0, 0)),
                  pl.BlockSpec((E, LANES), lambda i: (0, 0))],
        out_specs=[tok_spec, tok_spec, tok_spec, pl.BlockSpec((E, LANES), lambda i: (0, 0))],
        out_shape=[jax.ShapeDtypeStruct((TOP_K, n), I32), jax.ShapeDtypeStruct((TOP_K, n), F32),
                   jax.ShapeDtypeStruct((TOP_K, n), I32), jax.ShapeDtypeStruct((E, LANES), F32)],
        scratch_shapes=[pltpu.VMEM((E, LANES), F32)],
        compiler_params=_cparams("arbitrary"),
        name="router",
    )(h2, wt_hi, wt_lo, bias_col)


def _dest_kernel(eidx_ref, rank_ref, offs_ref, dest_ref):
    TT, E = eidx_ref.shape[1], N_EXPERTS
    io_e = lax.broadcasted_iota(I32, (E, TT), 0)
    io8 = lax.broadcasted_iota(I32, (TOP_K, TT), 0)
    offs = offs_ref[:, 0:1]
    base = jnp.zeros((TOP_K, TT), F32)
    for k in range(TOP_K):
        r = jnp.sum(jnp.where(io_e == eidx_ref[k:k + 1, :], offs, 0.0), axis=0, keepdims=True)
        base = jnp.where(io8 == k, r, base)
    dest_ref[...] = base.astype(I32) + rank_ref[...]


def _dest_rows(eidx, rank, offs_col):
    n = eidx.shape[1]
    TT = 512
    spec = pl.BlockSpec((TOP_K, TT), lambda i: (0, i))
    return pl.pallas_call(
        _dest_kernel,
        grid=(n // TT,),
        in_specs=[spec, spec, pl.BlockSpec((N_EXPERTS, LANES), lambda i: (0, 0))],
        out_specs=spec,
        out_shape=jax.ShapeDtypeStruct((TOP_K, n), I32),
        compiler_params=_cparams("parallel"),
        name="dest_rows",
    )(eidx, rank, offs_col)


def _sc_mesh():
    return plsc.VectorSubcoreMesh(core_axis_name="c", subcore_axis_name="s")


def _sc_worker_id():
    return lax.axis_index("s") * SC_CORES + lax.axis_index("c")


def _dispatch(dest_chunks, x2):
    n, d = x2.shape
    n_chunks = n // SC_ROWS // SC_WORKERS

    @functools.partial(
        pl.kernel, mesh=_sc_mesh(),
        out_type=jax.ShapeDtypeStruct((n * TOP_K, d), x2.dtype),
        scratch_types=[pltpu.VMEM((2, TOP_K, SC_ROWS), I32), pltpu.VMEM((2, SC_ROWS, d), x2.dtype),
                       pltpu.SemaphoreType.DMA((2,))],
    )
    def scatter_rows(x_hbm, idx_hbm, out_hbm, idx_v, rows_v, wsem):
        base = _sc_worker_id() * n_chunks

        def row_scatter(b, k):
            return pltpu.make_async_copy(rows_v.at[b], out_hbm.at[idx_v.at[b, k]], wsem.at[b])

        def drain(b):
            for k in range(TOP_K):
                row_scatter(b, k).wait()

        @pl.loop(0, n_chunks, step=2)
        def _(i):
            for b in range(2):
                c = base + i + b

                @pl.when(i > 0)
                def _():
                    drain(b)

                pltpu.sync_copy(idx_hbm.at[c], idx_v.at[b])
                pltpu.sync_copy(x_hbm.at[pl.ds(c * SC_ROWS, SC_ROWS)], rows_v.at[b])
                for k in range(TOP_K):
                    row_scatter(b, k).start()

        for b in range(2):
            drain(b)

    return scatter_rows(x2, dest_chunks)


def _gather_rows(src, idx):
    d = src.shape[1]
    p = idx.shape[0]
    per_w = p // SC_WORKERS
    n_chunks = per_w // SC_ROWS

    @functools.partial(
        pl.kernel, mesh=_sc_mesh(),
        out_type=jax.ShapeDtypeStruct((p, d), src.dtype),
        scratch_types=[pltpu.VMEM((2, SC_ROWS), I32), pltpu.VMEM((2, SC_ROWS, d), src.dtype),
                       pltpu.SemaphoreType.DMA((2,)), pltpu.SemaphoreType.DMA((2,))],
    )
    def gather_rows(src_hbm, idx_hbm, out_hbm, idx_v, rows_v, gsem, wsem):
        base = _sc_worker_id() * per_w

        def row_gather(b):
            return pltpu.make_async_copy(src_hbm.at[idx_v.at[b]], rows_v.at[b], gsem.at[b])

        def write_back(b, off):
            return pltpu.make_async_copy(rows_v.at[b], out_hbm.at[pl.ds(off, SC_ROWS)], wsem.at[b])

        @pl.loop(0, n_chunks, step=2)
        def _(i):
            for b in range(2):
                off = base + (i + b) * SC_ROWS

                @pl.when(i > 0)
                def _():
                    write_back(b, off).wait()

                pltpu.sync_copy(idx_hbm.at[pl.ds(off, SC_ROWS)], idx_v.at[b])
                row_gather(b).start()
            for b in range(2):
                row_gather(b).wait()
                write_back(b, base + (i + b) * SC_ROWS).start()

        for b in range(2):
            write_back(b, base).wait()

    return gather_rows(src, idx)


GMM_X_SLOTS = 3
GMM_O_SLOTS = 2


def _gmm_kernel(tile_ref, grp_ref, lo_ref, hi_ref, first_ref, last_ref, newgrp_ref, nextgrp_ref, nreal_ref,
                xs_hbm, win_hbm, wdn_hbm, ys_hbm,
                xbuf, obuf, winbuf, wdnbuf, win_bf, wdn_bf, xsem, osem, wsem, *, layer):
    TM = GMM_ROWS
    n_tiles = xs_hbm.shape[0] // TM
    half = xs_hbm.shape[1]

    def x_copy(t, slot):
        return pltpu.make_async_copy(xs_hbm.at[pl.ds(t * TM, TM)], xbuf.at[slot], xsem.at[slot])

    def o_copy(t, slot):
        return pltpu.make_async_copy(obuf.at[slot], ys_hbm.at[pl.ds(t * TM, TM)], osem.at[slot])

    def w_copies(e, slot):
        return (pltpu.make_async_copy(win_hbm.at[layer, e], winbuf.at[slot], wsem.at[0, slot]),
                pltpu.make_async_copy(wdn_hbm.at[layer, e], wdnbuf.at[slot], wsem.at[1, slot]))

    for t in range(GMM_X_SLOTS - 1):
        x_copy(t, t).start()
    for c in w_copies(grp_ref[0], 0):
        c.start()

    def visit(v, wslot):
        t = tile_ref[v]
        xslot = t % GMM_X_SLOTS
        oslot = t % GMM_O_SLOTS
        is_first = first_ref[v] == 1
        is_new = newgrp_ref[v] == 1
        wslot = jnp.where(is_new & (v > 0), 1 - wslot, wslot)

        @pl.when(is_first)
        def _():
            x_copy(t, xslot).wait()
            ahead = t + GMM_X_SLOTS - 1

            @pl.when(ahead < n_tiles)
            def _():
                x_copy(ahead, ahead % GMM_X_SLOTS).start()

            @pl.when(t >= GMM_O_SLOTS)
            def _():
                o_copy(t - GMM_O_SLOTS, oslot).wait()

        @pl.when(is_new)
        def _():
            for c in w_copies(grp_ref[v], wslot):
                c.wait()
            win_bf[...] = winbuf[wslot].astype(BF16)
            wdn_bf[...] = wdnbuf[wslot].astype(BF16)
            nxt = nextgrp_ref[v]

            @pl.when(nxt >= 0)
            def _():
                for c in w_copies(nxt, 1 - wslot):
                    c.start()

        x_lo, x_hi = _unpack_bf16_pair(xbuf[xslot])
        gu = (jnp.dot(x_lo, win_bf[:half, :], preferred_element_type=F32)
              + jnp.dot(x_hi, win_bf[half:, :], preferred_element_type=F32))
        act = (_silu(gu[:, :EXPERT_FF]) * gu[:, EXPERT_FF:]).astype(BF16)
        y = _pack_bf16_pair(jnp.dot(act, wdn_bf[...], preferred_element_type=F32))
        rows = lax.broadcasted_iota(I32, (TM, 1), 0)
        mine = (rows >= lo_ref[v]) & (rows < hi_ref[v])

        @pl.when(is_first)
        def _():
            obuf[oslot] = jnp.where(mine, y, 0)

        @pl.when(jnp.logical_not(is_first))
        def _():
            obuf[oslot] = jnp.where(mine, y, obuf[oslot])

        @pl.when(last_ref[v] == 1)
        def _():
            o_copy(t, oslot).start()

        return wslot

    lax.fori_loop(0, nreal_ref[0], visit, jnp.int32(0))
    for t in range(n_tiles - GMM_O_SLOTS, n_tiles):
        o_copy(t, t % GMM_O_SLOTS).wait()


def _gmm(meta, xs, w_in_all, w_dn_all, layer):
    p, half = xs.shape
    d = 2 * half
    TM = GMM_ROWS
    assert p % TM == 0 and p // TM >= GMM_X_SLOTS
    any_spec = pl.BlockSpec(memory_space=pl.ANY)
    grid_spec = pltpu.PrefetchScalarGridSpec(
        num_scalar_prefetch=len(meta),
        grid=(1,),
        in_specs=[any_spec, any_spec, any_spec],
        out_specs=any_spec,
        scratch_shapes=[pltpu.VMEM((GMM_X_SLOTS, TM, half), I32), pltpu.VMEM((GMM_O_SLOTS, TM, half), I32),
                        pltpu.VMEM((2, d, 2 * EXPERT_FF), F32), pltpu.VMEM((2, EXPERT_FF, d), F32),
                        pltpu.VMEM((d, 2 * EXPERT_FF), BF16), pltpu.VMEM((EXPERT_FF, d), BF16),
                        pltpu.SemaphoreType.DMA((GMM_X_SLOTS,)), pltpu.SemaphoreType.DMA((GMM_O_SLOTS,)),
                        pltpu.SemaphoreType.DMA((2, 2))],
    )
    return pl.pallas_call(
        functools.partial(_gmm_kernel, layer=layer),
        grid_spec=grid_spec,
        out_shape=jax.ShapeDtypeStruct((p, half), I32),
        compiler_params=_cparams("arbitrary"),
        name="gmm",
    )(*meta, xs, w_in_all, w_dn_all)


def _gmm_schedule(counts, n_rows):
    TM = GMM_ROWS
    E = counts.shape[0]
    max_visits = n_rows // TM + E - 1
    ends = jnp.cumsum(counts)
    starts = ends - counts
    first_tile = starts // TM
    n_vis = jnp.where(counts > 0, jnp.maximum(ends - 1, 0) // TM - first_tile + 1, 0)
    vis_end = jnp.cumsum(n_vis)
    vis_start = vis_end - n_vis
    n_real = vis_end[-1]
    idx = jnp.arange(max_visits, dtype=I32)
    v = jnp.minimum(idx, jnp.maximum(n_real - 1, 0))
    real = idx < n_real
    grp = jnp.minimum(jnp.sum((vis_end[None, :] <= v[:, None]).astype(I32), axis=1), E - 1)
    onehot = grp[:, None] == jnp.arange(E, dtype=I32)[None, :]

    def take(table):
        return jnp.sum(jnp.where(onehot, table[None, :], 0), axis=1)

    tile = take(first_tile) + (v - take(vis_start))
    lo = jnp.where(real, jnp.maximum(take(starts), tile * TM) - tile * TM, 0)
    hi = jnp.where(real, jnp.minimum(take(ends), (tile + 1) * TM) - tile * TM, 0)
    prev = lambda a: jnp.concatenate([jnp.full((1,), -1, I32), a[:-1]])
    first = real & (tile != prev(tile))
    newgrp = real & (grp != prev(grp))
    last = real & (jnp.concatenate([first[1:], jnp.ones((1,), bool)]) | (idx == n_real - 1))
    cand = jnp.where(counts > 0, jnp.arange(E, dtype=I32), E)
    later = jnp.concatenate([jnp.flip(lax.cummin(jnp.flip(cand)))[1:], jnp.full((1,), E, I32)])
    nextgrp = take(jnp.where(later < E, later, -1))
    meta = tuple(a.astype(I32) for a in (tile, grp, lo, hi, first, last, newgrp, nextgrp))
    return meta + (n_real.astype(I32).reshape(1),)


def _combine_kernel(wgt_ref, h_ref, yg_ref, wsi_ref, wsd_ref, g_ref, b_ref, o_ref):
    h = h_ref[...]
    gu = jnp.dot(h.astype(BF16), wsi_ref[...], preferred_element_type=F32)
    act = (_silu(gu[:, :SHARED_FF]) * gu[:, SHARED_FF:]).astype(BF16)
    acc = ALPHA * h + jnp.dot(act, wsd_ref[...], preferred_element_type=F32)
    wgt = wgt_ref[...]
    half = yg_ref.shape[2]
    r_lo = jnp.zeros((h.shape[0], half), F32)
    r_hi = jnp.zeros((h.shape[0], half), F32)
    for k in range(TOP_K):
        y_lo, y_hi = _unpack_bf16_pair(yg_ref[k])
        r_lo = r_lo + wgt[:, k:k + 1] * y_lo.astype(F32)
        r_hi = r_hi + wgt[:, k:k + 1] * y_hi.astype(F32)
    acc = acc + jnp.concatenate([r_lo, r_hi], axis=1)
    o_ref[...] = _layernorm(acc, g_ref[...], b_ref[...])


def _combine(wgt_tk, h2, yg, w_si, w_sd, g, b):
    n, d = h2.shape
    TT = COMBINE_TOK
    const2 = lambda shape: pl.BlockSpec(shape, lambda i: (0, 0))
    return pl.pallas_call(
        _combine_kernel,
        grid=(n // TT,),
        in_specs=[pl.BlockSpec((TT, TOP_K), lambda i: (i, 0)),
                  pl.BlockSpec((TT, d), lambda i: (i, 0)),
                  pl.BlockSpec((TOP_K, TT, d // 2), lambda i: (0, i, 0)),
                  const2((d, 2 * SHARED_FF)), const2((SHARED_FF, d)), const2((1, d)), const2((1, d))],
        out_specs=pl.BlockSpec((TT, d), lambda i: (i, 0)),
        out_shape=jax.ShapeDtypeStruct((n, d), F32),
        compiler_params=_cparams("parallel"),
        name="combine",
    )(wgt_tk, h2, yg, w_si, w_sd, g, b)


def _moe_ln(h2, h2_packed, w_router, r_bias, w_e_in_all, w_e_dn_all, layer, w_s_in, w_s_dn, ln_g, ln_b):
    n, d = h2.shape
    wt = w_router.T
    wt_hi = wt.astype(BF16)
    wt_lo = (wt - wt_hi.astype(F32)).astype(BF16)
    bias_col = jnp.broadcast_to(r_bias.astype(F32)[:, None], (N_EXPERTS, LANES))
    eidx, wgt, rank, cnt = _router(h2, wt_hi, wt_lo, bias_col)
    counts = cnt[:, 0].astype(I32)
    offs = jnp.cumsum(counts) - counts
    offs_col = jnp.broadcast_to(offs.astype(F32)[:, None], (N_EXPERTS, LANES))
    dest = _dest_rows(eidx, rank, offs_col)
    dest_chunks = dest.reshape(TOP_K, n // SC_ROWS, SC_ROWS).transpose(1, 0, 2)
    xs = _dispatch(dest_chunks, h2_packed)
    ys = _gmm(_gmm_schedule(counts, n * TOP_K), xs, w_e_in_all, w_e_dn_all, layer)
    yg = _gather_rows(ys, dest.reshape(-1)).reshape(TOP_K, n, d // 2)
    return _combine(wgt.T, h2, yg, w_s_in.astype(BF16), w_s_dn.astype(BF16), ln_g[None, :], ln_b[None, :])


def _pad_cols(w, width):
    return jnp.pad(w, ((0, 0), (0, width - w.shape[1])))


def kernel(x, w_in_ab, w_gla_gate_up, b_gla_gate, hgrn_norm_g, gla_norm_g, w_out_ab, hgrn_lb_logits, w_in_c, conv_c, b_if_c, w_out_c, w_router, router_bias, w_exp_in, w_exp_down, w_sh_in, w_sh_down, ln_mix_g, ln_mix_b, ln_ffn_g, ln_ffn_b):
    B, T, D = x.shape
    lower_bounds = jnp.cumsum(jax.nn.softmax(hgrn_lb_logits.astype(F32), axis=0), axis=0)
    h = x.reshape(B * T, D)
    for l in range(DEPTH):
        j = l // 2
        ln_g, ln_b = ln_mix_g[l][None, :], ln_mix_b[l][None, :]
        if l % 2 == 0:
            w_in = _pad_cols(w_in_ab[j], AB_WIDTH - GLA_GATE_RANK + LANES).astype(BF16)
            w_up = jnp.pad(w_gla_gate_up[j], ((0, LANES - GLA_GATE_RANK), (0, 0))).astype(BF16)
            h, h_packed = _hgrn_gla_layer(h, B, T, w_in, lower_bounds[l][None, :], w_up, b_gla_gate[j][None, :],
                                          hgrn_norm_g[j][None, :], gla_norm_g[j][None, :],
                                          w_out_ab[j].astype(BF16), ln_g, ln_b)
        else:
            w_in = _pad_cols(w_in_c[j], C_WIDTH - 2 * MLSTM_HEADS + LANES).astype(BF16)
            b_if = jnp.pad(b_if_c[j].astype(F32), (0, LANES - 2 * MLSTM_HEADS))[None, :]
            h, h_packed = _mlstm_layer(h, B, T, w_in, conv_c[j], b_if, w_out_c[j].astype(BF16), ln_g, ln_b)
        h = _moe_ln(h, h_packed, w_router[l], router_bias[l], w_exp_in, w_exp_down, l, w_sh_in[l], w_sh_down[l],
                    ln_ffn_g[l], ln_ffn_b[l])
    return h.reshape(B, T, D)
```

```python
import functools

import jax
import jax.numpy as jnp
from jax import lax
from jax.experimental import pallas as pl
from jax.experimental.pallas import tpu as pltpu
from jax.experimental.pallas import tpu_sc as plsc

F32 = jnp.float32
BF16 = jnp.bfloat16
I32 = jnp.int32

D_MODEL = 1024
DEPTH = 2
HGRN_HEADS = 4
HGRN_HEAD_DIM = 128
HGRN_WIDTH = HGRN_HEADS * HGRN_HEAD_DIM
GLA_HEADS = 4
GLA_KEY_DIM = 64
GLA_VAL_DIM = 128
GLA_KEY_WIDTH = GLA_HEADS * GLA_KEY_DIM
GLA_VAL_WIDTH = GLA_HEADS * GLA_VAL_DIM
GLA_GATE_RANK = 16
GLA_GATE_NORMALIZER = 16.0
LIN_CHUNK = 64
LIN_SUB = 16
AB_WIDTH = 4 * HGRN_WIDTH + 2 * GLA_KEY_WIDTH + 2 * GLA_VAL_WIDTH + GLA_GATE_RANK
MLSTM_HEADS = 4
MLSTM_QK_DIM = 128
MLSTM_V_DIM = 256
MLSTM_QK_WIDTH = MLSTM_HEADS * MLSTM_QK_DIM
MLSTM_V_WIDTH = MLSTM_HEADS * MLSTM_V_DIM
MLSTM_CONV = 4
MLSTM_CHUNK = 128
C_WIDTH = 2 * MLSTM_QK_WIDTH + 2 * MLSTM_V_WIDTH + 2 * MLSTM_HEADS
N_EXPERTS = 256
TOP_K = 8
N_GROUPS = 8
TOPK_GROUPS = 4
GROUP_SIZE = N_EXPERTS // N_GROUPS
EXPERT_FF = 256
SHARED_FF = 256
ROUTED_SCALE = 2.5
ALPHA = (2 * DEPTH) ** 0.25
LN_EPS = 1e-5
RMS_EPS = 1e-6

LANES = 128
VMEM_LIMIT = 56 * 1024 * 1024

MIX_ROWS = 256
MLSTM_ROWS = 512
ROUTE_TOK = 512
COMBINE_TOK = 512
GMM_ROWS = 256
SC_CORES, SC_SUBCORES = 2, 16
SC_WORKERS = SC_CORES * SC_SUBCORES
SC_ROWS = 64


def _cparams(*sem):
    return pltpu.CompilerParams(dimension_semantics=sem, vmem_limit_bytes=VMEM_LIMIT)


def _sigmoid(x):
    return 1.0 / (1.0 + jnp.exp(-x))


def _silu(x):
    return x * _sigmoid(x)


def _log_sigmoid(x):
    return jnp.minimum(x, 0.0) - jnp.log(1.0 + jnp.exp(-jnp.abs(x)))


def _split3(x):
    hi = x.astype(BF16)
    r = x - hi.astype(F32)
    mid = r.astype(BF16)
    lo = (r - mid.astype(F32)).astype(BF16)
    return hi, mid, lo


def _tri_dot_left(tri, x):
    return sum(jnp.dot(tri, p, preferred_element_type=F32) for p in _split3(x))


def _tri_dot_right(x, tri):
    return sum(jnp.dot(p, tri, preferred_element_type=F32) for p in _split3(x))


def _dot_nt(a, b):
    return lax.dot_general(a, b, (((1,), (1,)), ((), ())), preferred_element_type=F32)


def _dot_tn(a, b):
    return lax.dot_general(a, b, (((0,), (0,)), ((), ())), preferred_element_type=F32)


HI16 = -65536


def _pack_bf16_pair(x):
    c = x.shape[1] // 2
    bits = lambda v: lax.bitcast_convert_type(v.astype(BF16).astype(F32), I32)
    return (bits(x[:, c:]) & HI16) | lax.shift_right_logical(bits(x[:, :c]), 16)


def _unpack_bf16_pair(w):
    lo = lax.bitcast_convert_type(lax.shift_left(w, 16), F32)
    hi = lax.bitcast_convert_type(w & HI16, F32)
    return lo.astype(BF16), hi.astype(BF16)


def _layernorm(x, g, b):
    mu = jnp.mean(x, axis=-1, keepdims=True)
    xc = x - mu
    var = jnp.mean(xc * xc, axis=-1, keepdims=True)
    return xc * lax.rsqrt(var + LN_EPS) * g + b


N_UNITS = HGRN_HEADS + GLA_HEADS // 2
HALF_LANES = LANES // 2
LOG2E = 1.4426950408889634


def _pair_selector():
    assert LIN_CHUNK == HALF_LANES == GLA_KEY_DIM
    r = jnp.arange(LIN_SUB * LANES)[:, None]
    c = jnp.arange(LANES)[None, :]
    same_slot = (r // LANES) == (c % LIN_SUB)
    same_half = ((r % LANES) < HALF_LANES) == (c < HALF_LANES)
    return (same_slot & same_half).astype(BF16)


def _pairwise_tiles(q, k, b, lhs_ref, row0):
    c = LIN_SUB
    b2 = b * LOG2E
    for blk in range(LIN_CHUNK // c):
        lo = blk * c
        qb, kb, bb = q[lo:lo + c], k[lo:lo + c], b2[lo:lo + c]
        for j in range(c):
            e = jnp.exp2(jnp.minimum(bb - bb[j:j + 1], 0.0))
            lhs_ref[row0 + lo:row0 + lo + c, j * LANES:(j + 1) * LANES] = (qb * kb[j:j + 1] * e).astype(BF16)


def _unit_chunk(q, k, b, vs, r, st_ref, masks):
    L, c = LIN_CHUNK, LIN_SUB
    lane_lo, col_mod, same_blk_causal = masks
    heads = [lane_lo, ~lane_lo] if len(vs) == 2 else [None]

    def pick(x, m):
        return x if m is None else jnp.where(m, x, jnp.zeros_like(x))

    g = b[L - 1:L, :]
    st = st_ref[...]
    st_b = st.astype(BF16)
    qx = (q * jnp.exp(b)).astype(BF16)
    outs = [_dot_nt(pick(qx, m), st_b) for m in heads]
    k_end = (k * jnp.exp(g - b)).astype(BF16)
    upd = [_dot_tn(v.astype(BF16), k_end) for v in vs]
    st_ref[...] = st * jnp.exp(g) + (upd[0] if len(vs) == 1 else jnp.where(lane_lo, upd[0], upd[1]))

    off_rows = [jnp.zeros((c, LANES), F32)]
    for blk in range(1, L // c):
        lo = blk * c
        ref = b[lo - 1:lo]
        q_in = (q[lo:lo + c] * jnp.exp(b[lo:lo + c] - ref)).astype(BF16)
        k_in = (k * jnp.exp(jnp.minimum(ref - b, 0.0))).astype(BF16)
        if len(vs) == 2:
            k_cat = jnp.concatenate([pick(k_in, heads[0]), pick(k_in, heads[1])], axis=0)
        else:
            k_cat = jnp.concatenate([k_in, jnp.zeros_like(k_in)], axis=0)
        off_rows.append(jnp.where(col_mod < lo, _dot_nt(q_in, k_cat), 0.0))
    a = jnp.where(same_blk_causal, r, jnp.concatenate(off_rows, axis=0)).astype(BF16)
    v_cat = jnp.concatenate([vs[0], vs[-1]], axis=0).astype(BF16)
    return [o + jnp.dot(pick(a, m), v_cat, preferred_element_type=F32) for o, m in zip(outs, heads)]


def _rms_gate(o, g, gate):
    o = o * lax.rsqrt(jnp.mean(o * o, axis=-1, keepdims=True) + RMS_EPS)
    return o * g * _silu(gate)


class _Cols:
    def __init__(self, ref, off, width):
        self.ref, self.off, self.width = ref, off, width

    def __getitem__(self, idx):
        rows, cols = (slice(None), slice(None)) if idx is Ellipsis else idx
        lo = self.off + (cols.start or 0)
        hi = self.off + (self.width if cols.stop is None else cols.stop)
        return self.ref[rows, lo:hi]


def _project_residual_ln(x_ref, y_ref, wout_ref, g_ref, b_ref, h_ref, hp_ref):
    mix = jnp.dot(y_ref[...], wout_ref[...], preferred_element_type=F32)
    h = _layernorm(ALPHA * x_ref[...] + mix, g_ref[...], b_ref[...])
    h_ref[...] = h
    hp_ref[...] = _pack_bf16_pair(h)


def _hgrn_gla_kernel(x_ref, win_ref, lb_ref, wup_ref, bgk_ref, hng_ref, gng_ref, sel_ref, wout_ref, lng_ref, lnb_ref,
                     h_ref, hp_ref, proj_ref, y_ref, lhs_ref, *st_refs):
    @pl.when(pl.program_id(1) == 0)
    def _():
        for r in st_refs:
            r[...] = jnp.zeros_like(r)

    proj_ref[...] = jnp.dot(x_ref[...].astype(BF16), win_ref[...], preferred_element_type=F32)
    W, KW, VW = HGRN_WIDTH, GLA_KEY_WIDTH, GLA_VAL_WIDTH
    hq_ref, hf_ref, hi_ref, hg_ref = (_Cols(proj_ref, i * W, W) for i in range(4))
    gq_ref, gk_ref = _Cols(proj_ref, 4 * W, KW), _Cols(proj_ref, 4 * W + KW, KW)
    gv_ref, gg_ref = _Cols(proj_ref, 4 * W + 2 * KW, VW), _Cols(proj_ref, 4 * W + 2 * KW + VW, VW)
    glr_ref = _Cols(proj_ref, AB_WIDTH - GLA_GATE_RANK, LANES)

    L = LIN_CHUNK
    n_chunks = MIX_ROWS // L
    r_i = lax.broadcasted_iota(I32, (L, LANES), 0)
    c_i = lax.broadcasted_iota(I32, (L, LANES), 1)
    col_mod = jnp.where(c_i < HALF_LANES, c_i, c_i - HALF_LANES)
    same_blk_causal = (col_mod // LIN_SUB == r_i // LIN_SUB) & (col_mod <= r_i)
    c_sub = lax.broadcasted_iota(I32, (LIN_SUB, LANES), 1)
    col_mod_sub = jnp.where(c_sub < HALF_LANES, c_sub, c_sub - HALF_LANES)
    lane_lo = lax.broadcasted_iota(I32, (1, LANES), 1) < HALF_LANES
    masks = (lane_lo, col_mod_sub, same_blk_causal)
    tril = (lax.broadcasted_iota(I32, (L, L), 0) >= lax.broadcasted_iota(I32, (L, L), 1)).astype(BF16)
    lb = lb_ref[...]

    units = []
    for ck in range(n_chunks):
        rs = slice(ck * L, (ck + 1) * L)
        u = jnp.dot(glr_ref[rs, :].astype(BF16), wup_ref[...], preferred_element_type=F32) + bgk_ref[...]
        la_g = _log_sigmoid(u) * (1.0 / GLA_GATE_NORMALIZER)
        qs, ks, las = [], [], []
        for h in range(HGRN_HEADS):
            cs = slice(h * LANES, (h + 1) * LANES)
            z = hf_ref[rs, cs]
            lbh = lb[:, cs]
            qs.append(_silu(hq_ref[rs, cs]))
            ks.append((1.0 - lbh) * _sigmoid(-z))
            las.append(jnp.log(lbh + (1.0 - lbh) * _sigmoid(z)))
        for p in range(GLA_HEADS // 2):
            cs = slice(p * LANES, (p + 1) * LANES)
            qs.append(gq_ref[rs, cs] * (GLA_KEY_DIM ** -0.5))
            ks.append(gk_ref[rs, cs])
            las.append(la_g[:, cs])
        b_all = _tri_dot_left(tril, jnp.concatenate(las, axis=1))
        for un in range(N_UNITS):
            b = b_all[:, un * LANES:(un + 1) * LANES]
            _pairwise_tiles(qs[un], ks[un], b, lhs_ref, (ck * N_UNITS + un) * L)
            units.append((qs[un], ks[un], b))

    r_all = jnp.dot(lhs_ref[...], sel_ref[...], preferred_element_type=F32)

    for ck in range(n_chunks):
        rs = slice(ck * L, (ck + 1) * L)
        for un in range(N_UNITS):
            uc = ck * N_UNITS + un
            q, k, b = units[uc]
            r = r_all[uc * L:(uc + 1) * L]
            if un < HGRN_HEADS:
                cs = slice(un * LANES, (un + 1) * LANES)
                (o,) = _unit_chunk(q, k, b, [hi_ref[rs, cs]], r, st_refs[un], masks)
                y_ref[rs, cs] = _rms_gate(o, hng_ref[:, cs], hg_ref[rs, cs]).astype(y_ref.dtype)
            else:
                p = un - HGRN_HEADS
                vcs = [slice((2 * p + i) * GLA_VAL_DIM, (2 * p + i + 1) * GLA_VAL_DIM) for i in range(2)]
                outs = _unit_chunk(q, k, b, [gv_ref[rs, vc] for vc in vcs], r, st_refs[un], masks)
                for o, vc in zip(outs, vcs):
                    ys = slice(HGRN_WIDTH + vc.start, HGRN_WIDTH + vc.stop)
                    y_ref[rs, ys] = _rms_gate(o, gng_ref[:, vc], gg_ref[rs, vc]).astype(y_ref.dtype)

    _project_residual_ln(x_ref, y_ref, wout_ref, lng_ref, lnb_ref, h_ref, hp_ref)


def _mixer_layer_call(kernel_fn, name, x2, batch, seq, rows, consts, scratch_shapes):
    n, d = x2.shape
    steps = seq // rows
    row_spec = lambda width: pl.BlockSpec((rows, width), lambda b, i: (b * steps + i, 0))
    const_spec = lambda a: pl.BlockSpec(a.shape, lambda b, i: (0,) * a.ndim)
    return pl.pallas_call(
        kernel_fn,
        grid=(batch, steps),
        in_specs=[row_spec(d)] + [const_spec(a) for a in consts],
        out_specs=[row_spec(d), row_spec(d // 2)],
        out_shape=[jax.ShapeDtypeStruct((n, d), F32), jax.ShapeDtypeStruct((n, d // 2), I32)],
        scratch_shapes=scratch_shapes,
        compiler_params=_cparams("parallel", "arbitrary"),
        name=name,
    )(x2, *consts)


def _hgrn_gla_layer(x2, batch, seq, w_in, lb, w_up, b_gk, hgrn_g, gla_g, w_out, ln_g, ln_b):
    R = MIX_ROWS
    n_uc = (R // LIN_CHUNK) * N_UNITS
    scratch = ([pltpu.VMEM((R, w_in.shape[1]), F32), pltpu.VMEM((R, w_out.shape[0]), BF16),
                pltpu.VMEM((n_uc * LIN_CHUNK, LIN_SUB * LANES), BF16)]
               + [pltpu.VMEM((GLA_VAL_DIM, LANES), F32)] * N_UNITS)
    consts = (w_in, lb, w_up, b_gk, hgrn_g, gla_g, _pair_selector(), w_out, ln_g, ln_b)
    return _mixer_layer_call(_hgrn_gla_kernel, "hgrn_gla", x2, batch, seq, R, consts, scratch)


MLSTM_CONV_PAD = 8


def _mlstm_kernel(x_ref, win_ref, cw_ref, bif_ref, wout_ref, lng_ref, lnb_ref, h_ref, hp_ref,
                  proj_ref, y_ref, ext_ref, *state_refs):
    L, R, PAD = MLSTM_CHUNK, MLSTM_ROWS, MLSTM_CONV_PAD
    cn_refs, m_refs = state_refs[:MLSTM_HEADS], state_refs[MLSTM_HEADS:]

    @pl.when(pl.program_id(1) == 0)
    def _():
        ext_ref[0:PAD, :] = jnp.zeros((PAD, ext_ref.shape[1]), F32)
        for r in state_refs:
            r[...] = jnp.zeros_like(r)

    proj_ref[...] = jnp.dot(x_ref[...].astype(BF16), win_ref[...], preferred_element_type=F32)
    QK2, VW = 2 * MLSTM_QK_WIDTH, MLSTM_V_WIDTH
    qk_ref, v_ref = _Cols(proj_ref, 0, QK2), _Cols(proj_ref, QK2, VW)
    og_ref, gt_ref = _Cols(proj_ref, QK2 + VW, VW), _Cols(proj_ref, QK2 + 2 * VW, LANES)

    ext_ref[PAD:PAD + R, :] = qk_ref[...]
    conv = jnp.zeros((R, ext_ref.shape[1]), F32)
    for w in range(MLSTM_CONV):
        conv = conv + cw_ref[w:w + 1, :] * ext_ref[pl.ds(PAD - (MLSTM_CONV - 1) + w, R), :]
    ext_ref[0:PAD, :] = ext_ref[R:R + PAD, :]
    qk_all = _silu(conv)

    r_i = lax.broadcasted_iota(I32, (L, L), 0)
    c_i = lax.broadcasted_iota(I32, (L, L), 1)
    causal = c_i <= r_i
    tril = causal.astype(BF16)
    triu = (r_i <= c_i).astype(BF16)
    ones_col = (lax.broadcasted_iota(I32, (L, LANES), 1) == 0).astype(F32)

    H = MLSTM_HEADS
    for ck in range(R // L):
        rs = slice(ck * L, (ck + 1) * L)
        qk = qk_all[rs]
        gt = gt_ref[rs, :] + bif_ref[...]
        gt_t = gt.T
        b_cols = _tri_dot_left(tril, _log_sigmoid(gt))
        b_rows = _tri_dot_right(_log_sigmoid(gt_t), triu)
        for h in range(H):
            q = qk[:, h * MLSTM_QK_DIM:(h + 1) * MLSTM_QK_DIM].astype(BF16)
            kf = qk[:, MLSTM_QK_WIDTH + h * MLSTM_QK_DIM:MLSTM_QK_WIDTH + (h + 1) * MLSTM_QK_DIM] * (MLSTM_QK_DIM ** -0.5)
            vs = slice(h * MLSTM_V_DIM, (h + 1) * MLSTM_V_DIM)
            v_ext = jnp.concatenate([v_ref[rs, vs], ones_col], axis=1).astype(BF16)
            bc, br = b_cols[:, H + h:H + h + 1], b_rows[H + h:H + h + 1, :]
            ic, ir = gt[:, h:h + 1], gt_t[h:h + 1, :]
            m_prev = m_refs[h][0:1, 0:1]
            g = bc[L - 1:L, :]
            dmat = jnp.where(causal, bc - br + ir, -jnp.inf)
            inter = bc + m_prev
            m_j = jnp.maximum(inter, jnp.max(dmat, axis=-1, keepdims=True))
            s = _dot_nt(q, kf.astype(BF16)) * jnp.exp(dmat - m_j)
            w_inter = jnp.exp(inter - m_j)
            cn = cn_refs[h][...]
            nd = (jnp.dot(s.astype(BF16), v_ext, preferred_element_type=F32)
                  + w_inter * jnp.dot(q, cn.astype(BF16), preferred_element_type=F32))
            num, den = nd[:, :MLSTM_V_DIM], nd[:, MLSTM_V_DIM:MLSTM_V_DIM + 1]
            hid = num / jnp.maximum(jnp.abs(den), jnp.exp(-m_j))
            u = g - bc + ic
            m_new = jnp.maximum(g + m_prev, jnp.max(u, axis=0, keepdims=True))
            wk = (kf * jnp.exp(u - m_new)).astype(BF16)
            cn_refs[h][...] = jnp.exp(g + m_prev - m_new) * cn + _dot_tn(wk, v_ext)
            m_refs[h][...] = jnp.broadcast_to(m_new, m_refs[h].shape)
            y_ref[rs, vs] = (hid * _sigmoid(og_ref[rs, vs])).astype(y_ref.dtype)

    _project_residual_ln(x_ref, y_ref, wout_ref, lng_ref, lnb_ref, h_ref, hp_ref)


def _mlstm_layer(x2, batch, seq, w_in, conv_w, b_if, w_out, ln_g, ln_b):
    R = MLSTM_ROWS
    assert R % MLSTM_CHUNK == 0
    scratch = ([pltpu.VMEM((R, w_in.shape[1]), F32), pltpu.VMEM((R, w_out.shape[0]), BF16),
                pltpu.VMEM((R + MLSTM_CONV_PAD, 2 * MLSTM_QK_WIDTH), F32)]
               + [pltpu.VMEM((MLSTM_QK_DIM, MLSTM_V_DIM + LANES), F32)] * MLSTM_HEADS
               + [pltpu.VMEM((8, LANES), F32)] * MLSTM_HEADS)
    consts = (w_in, conv_w, b_if, w_out, ln_g, ln_b)
    return _mixer_layer_call(_mlstm_kernel, "mlstm", x2, batch, seq, R, consts, scratch)


def _router_kernel(h_ref, whi_ref, wlo_ref, bias_ref, eidx_ref, wgt_ref, rank_ref, cnt_ref, carry_ref):
    TT, E, G, GS = ROUTE_TOK, N_EXPERTS, N_GROUPS, GROUP_SIZE

    @pl.when(pl.program_id(0) == 0)
    def _():
        carry_ref[...] = jnp.zeros_like(carry_ref)

    h = h_ref[...]
    h_hi = h.astype(BF16)
    h_lo = (h - h_hi.astype(F32)).astype(BF16)
    logits = _dot_nt(whi_ref[...], h_hi) + _dot_nt(whi_ref[...], h_lo) + _dot_nt(wlo_ref[...], h_hi)
    scores = _sigmoid(logits)
    biased = scores + bias_ref[:, 0:1]
    neg = -jnp.inf

    io_g = lax.broadcasted_iota(I32, (GS, TT), 0)
    io8 = lax.broadcasted_iota(I32, (G, TT), 0)
    gs = jnp.zeros((G, TT), F32)
    for g in range(G):
        blk = biased[g * GS:(g + 1) * GS, :]
        m1 = jnp.max(blk, axis=0, keepdims=True)
        i1 = jnp.min(jnp.where(blk == m1, io_g, GS), axis=0, keepdims=True)
        m2 = jnp.max(jnp.where(io_g == i1, neg, blk), axis=0, keepdims=True)
        gs = jnp.where(io8 == g, m1 + m2, gs)
    gsel = jnp.zeros((G, TT), F32)
    for _ in range(TOPK_GROUPS):
        m = jnp.max(gs, axis=0, keepdims=True)
        idx = jnp.min(jnp.where(gs == m, io8, G), axis=0, keepdims=True)
        hit = io8 == idx
        gsel = jnp.where(hit, 1.0, gsel)
        gs = jnp.where(hit, neg, gs)
    sel = jnp.concatenate([jnp.broadcast_to(gsel[g:g + 1, :], (GS, TT)) for g in range(G)], axis=0)
    masked = jnp.where(sel > 0.0, biased, neg)

    io_e = lax.broadcasted_iota(I32, (E, TT), 0)
    eidx = jnp.zeros((TOP_K, TT), I32)
    wsel = jnp.zeros((TOP_K, TT), F32)
    chosen = jnp.zeros((E, TT), F32)
    for k in range(TOP_K):
        m = jnp.max(masked, axis=0, keepdims=True)
        idx = jnp.min(jnp.where(masked == m, io_e, E), axis=0, keepdims=True)
        hit = io_e == idx
        sc = jnp.sum(jnp.where(hit, scores, 0.0), axis=0, keepdims=True)
        eidx = jnp.where(io8 == k, idx, eidx)
        wsel = jnp.where(io8 == k, sc, wsel)
        chosen = jnp.where(hit, 1.0, chosen)
        masked = jnp.where(hit, neg, masked)
    wgt_ref[...] = wsel / jnp.sum(wsel, axis=0, keepdims=True) * ROUTED_SCALE
    eidx_ref[...] = eidx

    triu = (lax.broadcasted_iota(I32, (TT, TT), 0) <= lax.broadcasted_iota(I32, (TT, TT), 1)).astype(BF16)
    cum = jnp.dot(chosen.astype(BF16), triu, preferred_element_type=F32)
    carry = carry_ref[:, 0:1]
    before = cum - chosen + carry
    rank = jnp.zeros((TOP_K, TT), F32)
    for k in range(TOP_K):
        r = jnp.sum(jnp.where(io_e == eidx[k:k + 1, :], before, 0.0), axis=0, keepdims=True)
        rank = jnp.where(io8 == k, r, rank)
    rank_ref[...] = rank.astype(I32)
    total = carry + cum[:, TT - 1:TT]
    carry_ref[...] = jnp.broadcast_to(total, carry_ref.shape)
    cnt_ref[...] = jnp.broadcast_to(total, cnt_ref.shape)


def _router(h2, wt_hi, wt_lo, bias_col):
    n, d = h2.shape
    TT, E = ROUTE_TOK, N_EXPERTS
    tok_spec = pl.BlockSpec((TOP_K, TT), lambda i: (0, i))
    return pl.pallas_call(
        _router_kernel,
        grid=(n // TT,),
        in_specs=[pl.BlockSpec((TT, d), lambda i: (i, 0)),
                  pl.BlockSpec((E, d), lambda i: (0, 0)),
                  pl.BlockSpec((E, d), lambda i: (0, 0)),
                  pl.BlockSpec((E, LANES), lambda i: (0, 0))],
        out_specs=[tok_spec, tok_spec, tok_spec, pl.BlockSpec((E, LANES), lambda i: (0, 0))],
        out_shape=[jax.ShapeDtypeStruct((TOP_K, n), I32), jax.ShapeDtypeStruct((TOP_K, n), F32),
                   jax.ShapeDtypeStruct((TOP_K, n), I32), jax.ShapeDtypeStruct((E, LANES), F32)],
        scratch_shapes=[pltpu.VMEM((E, LANES), F32)],
        compiler_params=_cparams("arbitrary"),
        name="router",
    )(h2, wt_hi, wt_lo, bias_col)


def _dest_kernel(eidx_ref, rank_ref, offs_ref, dest_ref):
    TT, E = eidx_ref.shape[1], N_EXPERTS
    io_e = lax.broadcasted_iota(I32, (E, TT), 0)
    io8 = lax.broadcasted_iota(I32, (TOP_K, TT), 0)
    offs = offs_ref[:, 0:1]
    base = jnp.zeros((TOP_K, TT), F32)
    for k in range(TOP_K):
        r = jnp.sum(jnp.where(io_e == eidx_ref[k:k + 1, :], offs, 0.0), axis=0, keepdims=True)
        base = jnp.where(io8 == k, r, base)
    dest_ref[...] = base.astype(I32) + rank_ref[...]


def _dest_rows(eidx, rank, offs_col):
    n = eidx.shape[1]
    TT = 512
    spec = pl.BlockSpec((TOP_K, TT), lambda i: (0, i))
    return pl.pallas_call(
        _dest_kernel,
        grid=(n // TT,),
        in_specs=[spec, spec, pl.BlockSpec((N_EXPERTS, LANES), lambda i: (0, 0))],
        out_specs=spec,
        out_shape=jax.ShapeDtypeStruct((TOP_K, n), I32),
        compiler_params=_cparams("parallel"),
        name="dest_rows",
    )(eidx, rank, offs_col)


def _sc_mesh():
    return plsc.VectorSubcoreMesh(core_axis_name="c", subcore_axis_name="s")


def _sc_worker_id():
    return lax.axis_index("s") * SC_CORES + lax.axis_index("c")


def _dispatch(dest_chunks, x2):
    n, d = x2.shape
    n_chunks = n // SC_ROWS // SC_WORKERS

    @functools.partial(
        pl.kernel, mesh=_sc_mesh(),
        out_type=jax.ShapeDtypeStruct((n * TOP_K, d), x2.dtype),
        scratch_types=[pltpu.VMEM((2, TOP_K, SC_ROWS), I32), pltpu.VMEM((2, SC_ROWS, d), x2.dtype),
                       pltpu.SemaphoreType.DMA((2,))],
    )
    def scatter_rows(x_hbm, idx_hbm, out_hbm, idx_v, rows_v, wsem):
        base = _sc_worker_id() * n_chunks

        def row_scatter(b, k):
            return pltpu.make_async_copy(rows_v.at[b], out_hbm.at[idx_v.at[b, k]], wsem.at[b])

        def drain(b):
            for k in range(TOP_K):
                row_scatter(b, k).wait()

        @pl.loop(0, n_chunks, step=2)
        def _(i):
            for b in range(2):
                c = base + i + b

                @pl.when(i > 0)
                def _():
                    drain(b)

                pltpu.sync_copy(idx_hbm.at[c], idx_v.at[b])
                pltpu.sync_copy(x_hbm.at[pl.ds(c * SC_ROWS, SC_ROWS)], rows_v.at[b])
                for k in range(TOP_K):
                    row_scatter(b, k).start()

        for b in range(2):
            drain(b)

    return scatter_rows(x2, dest_chunks)


def _gather_rows(src, idx):
    d = src.shape[1]
    p = idx.shape[0]
    per_w = p // SC_WORKERS
    n_chunks = per_w // SC_ROWS

    @functools.partial(
        pl.kernel, mesh=_sc_mesh(),
        out_type=jax.ShapeDtypeStruct((p, d), src.dtype),
        scratch_types=[pltpu.VMEM((2, SC_ROWS), I32), pltpu.VMEM((2, SC_ROWS, d), src.dtype),
                       pltpu.SemaphoreType.DMA((2,)), pltpu.SemaphoreType.DMA((2,))],
    )
    def gather_rows(src_hbm, idx_hbm, out_hbm, idx_v, rows_v, gsem, wsem):
        base = _sc_worker_id() * per_w

        def row_gather(b):
            return pltpu.make_async_copy(src_hbm.at[idx_v.at[b]], rows_v.at[b], gsem.at[b])

        def write_back(b, off):
            return pltpu.make_async_copy(rows_v.at[b], out_hbm.at[pl.ds(off, SC_ROWS)], wsem.at[b])

        @pl.loop(0, n_chunks, step=2)
        def _(i):
            for b in range(2):
                off = base + (i + b) * SC_ROWS

                @pl.when(i > 0)
                def _():
                    write_back(b, off).wait()

                pltpu.sync_copy(idx_hbm.at[pl.ds(off, SC_ROWS)], idx_v.at[b])
                row_gather(b).start()
            for b in range(2):
                row_gather(b).wait()
                write_back(b, base + (i + b) * SC_ROWS).start()

        for b in range(2):
            write_back(b, base).wait()

    return gather_rows(src, idx)


GMM_X_SLOTS = 3
GMM_O_SLOTS = 2


def _gmm_kernel(tile_ref, grp_ref, lo_ref, hi_ref, first_ref, last_ref, newgrp_ref, nextgrp_ref, nreal_ref,
                xs_hbm, win_hbm, wdn_hbm, ys_hbm,
                xbuf, obuf, winbuf, wdnbuf, win_bf, wdn_bf, xsem, osem, wsem, *, layer):
    TM = GMM_ROWS
    n_tiles = xs_hbm.shape[0] // TM
    half = xs_hbm.shape[1]

    def x_copy(t, slot):
        return pltpu.make_async_copy(xs_hbm.at[pl.ds(t * TM, TM)], xbuf.at[slot], xsem.at[slot])

    def o_copy(t, slot):
        return pltpu.make_async_copy(obuf.at[slot], ys_hbm.at[pl.ds(t * TM, TM)], osem.at[slot])

    def w_copies(e, slot):
        return (pltpu.make_async_copy(win_hbm.at[layer, e], winbuf.at[slot], wsem.at[0, slot]),
                pltpu.make_async_copy(wdn_hbm.at[layer, e], wdnbuf.at[slot], wsem.at[1, slot]))

    for t in range(GMM_X_SLOTS - 1):
        x_copy(t, t).start()
    for c in w_copies(grp_ref[0], 0):
        c.start()

    def visit(v, wslot):
        t = tile_ref[v]
        xslot = t % GMM_X_SLOTS
        oslot = t % GMM_O_SLOTS
        is_first = first_ref[v] == 1
        is_new = newgrp_ref[v] == 1
        wslot = jnp.where(is_new & (v > 0), 1 - wslot, wslot)

        @pl.when(is_first)
        def _():
            x_copy(t, xslot).wait()
            ahead = t + GMM_X_SLOTS - 1

            @pl.when(ahead < n_tiles)
            def _():
                x_copy(ahead, ahead % GMM_X_SLOTS).start()

            @pl.when(t >= GMM_O_SLOTS)
            def _():
                o_copy(t - GMM_O_SLOTS, oslot).wait()

        @pl.when(is_new)
        def _():
            for c in w_copies(grp_ref[v], wslot):
                c.wait()
            win_bf[...] = winbuf[wslot].astype(BF16)
            wdn_bf[...] = wdnbuf[wslot].astype(BF16)
            nxt = nextgrp_ref[v]

            @pl.when(nxt >= 0)
            def _():
                for c in w_copies(nxt, 1 - wslot):
                    c.start()

        x_lo, x_hi = _unpack_bf16_pair(xbuf[xslot])
        gu = (jnp.dot(x_lo, win_bf[:half, :], preferred_element_type=F32)
              + jnp.dot(x_hi, win_bf[half:, :], preferred_element_type=F32))
        act = (_silu(gu[:, :EXPERT_FF]) * gu[:, EXPERT_FF:]).astype(BF16)
        y = _pack_bf16_pair(jnp.dot(act, wdn_bf[...], preferred_element_type=F32))
        rows = lax.broadcasted_iota(I32, (TM, 1), 0)
        mine = (rows >= lo_ref[v]) & (rows < hi_ref[v])

        @pl.when(is_first)
        def _():
            obuf[oslot] = jnp.where(mine, y, 0)

        @pl.when(jnp.logical_not(is_first))
        def _():
            obuf[oslot] = jnp.where(mine, y, obuf[oslot])

        @pl.when(last_ref[v] == 1)
        def _():
            o_copy(t, oslot).start()

        return wslot

    lax.fori_loop(0, nreal_ref[0], visit, jnp.int32(0))
    for t in range(n_tiles - GMM_O_SLOTS, n_tiles):
        o_copy(t, t % GMM_O_SLOTS).wait()


def _gmm(meta, xs, w_in_all, w_dn_all, layer):
    p, half = xs.shape
    d = 2 * half
    TM = GMM_ROWS
    assert p % TM == 0 and p // TM >= GMM_X_SLOTS
    any_spec = pl.BlockSpec(memory_space=pl.ANY)
    grid_spec = pltpu.PrefetchScalarGridSpec(
        num_scalar_prefetch=len(meta),
        grid=(1,),
        in_specs=[any_spec, any_spec, any_spec],
        out_specs=any_spec,
        scratch_shapes=[pltpu.VMEM((GMM_X_SLOTS, TM, half), I32), pltpu.VMEM((GMM_O_SLOTS, TM, half), I32),
                        pltpu.VMEM((2, d, 2 * EXPERT_FF), F32), pltpu.VMEM((2, EXPERT_FF, d), F32),
                        pltpu.VMEM((d, 2 * EXPERT_FF), BF16), pltpu.VMEM((EXPERT_FF, d), BF16),
                        pltpu.SemaphoreType.DMA((GMM_X_SLOTS,)), pltpu.SemaphoreType.DMA((GMM_O_SLOTS,)),
                        pltpu.SemaphoreType.DMA((2, 2))],
    )
    return pl.pallas_call(
        functools.partial(_gmm_kernel, layer=layer),
        grid_spec=grid_spec,
        out_shape=jax.ShapeDtypeStruct((p, half), I32),
        compiler_params=_cparams("arbitrary"),
        name="gmm",
    )(*meta, xs, w_in_all, w_dn_all)


def _gmm_schedule(counts, n_rows):
    TM = GMM_ROWS
    E = counts.shape[0]
    max_visits = n_rows // TM + E - 1
    ends = jnp.cumsum(counts)
    starts = ends - counts
    first_tile = starts // TM
    n_vis = jnp.where(counts > 0, jnp.maximum(ends - 1, 0) // TM - first_tile + 1, 0)
    vis_end = jnp.cumsum(n_vis)
    vis_start = vis_end - n_vis
    n_real = vis_end[-1]
    idx = jnp.arange(max_visits, dtype=I32)
    v = jnp.minimum(idx, jnp.maximum(n_real - 1, 0))
    real = idx < n_real
    grp = jnp.minimum(jnp.sum((vis_end[None, :] <= v[:, None]).astype(I32), axis=1), E - 1)
    onehot = grp[:, None] == jnp.arange(E, dtype=I32)[None, :]

    def take(table):
        return jnp.sum(jnp.where(onehot, table[None, :], 0), axis=1)

    tile = take(first_tile) + (v - take(vis_start))
    lo = jnp.where(real, jnp.maximum(take(starts), tile * TM) - tile * TM, 0)
    hi = jnp.where(real, jnp.minimum(take(ends), (tile + 1) * TM) - tile * TM, 0)
    prev = lambda a: jnp.concatenate([jnp.full((1,), -1, I32), a[:-1]])
    first = real & (tile != prev(tile))
    newgrp = real & (grp != prev(grp))
    last = real & (jnp.concatenate([first[1:], jnp.ones((1,), bool)]) | (idx == n_real - 1))
    cand = jnp.where(counts > 0, jnp.arange(E, dtype=I32), E)
    later = jnp.concatenate([jnp.flip(lax.cummin(jnp.flip(cand)))[1:], jnp.full((1,), E, I32)])
    nextgrp = take(jnp.where(later < E, later, -1))
    meta = tuple(a.astype(I32) for a in (tile, grp, lo, hi, first, last, newgrp, nextgrp))
    return meta + (n_real.astype(I32).reshape(1),)


def _combine_kernel(wgt_ref, h_ref, yg_ref, wsi_ref, wsd_ref, g_ref, b_ref, o_ref):
    h = h_ref[...]
    gu = jnp.dot(h.astype(BF16), wsi_ref[...], preferred_element_type=F32)
    act = (_silu(gu[:, :SHARED_FF]) * gu[:, SHARED_FF:]).astype(BF16)
    acc = ALPHA * h + jnp.dot(act, wsd_ref[...], preferred_element_type=F32)
    wgt = wgt_ref[...]
    half = yg_ref.shape[2]
    r_lo = jnp.zeros((h.shape[0], half), F32)
    r_hi = jnp.zeros((h.shape[0], half), F32)
    for k in range(TOP_K):
        y_lo, y_hi = _unpack_bf16_pair(yg_ref[k])
        r_lo = r_lo + wgt[:, k:k + 1] * y_lo.astype(F32)
        r_hi = r_hi + wgt[:, k:k + 1] * y_hi.astype(F32)
    acc = acc + jnp.concatenate([r_lo, r_hi], axis=1)
    o_ref[...] = _layernorm(acc, g_ref[...], b_ref[...])


def _combine(wgt_tk, h2, yg, w_si, w_sd, g, b):
    n, d = h2.shape
    TT = COMBINE_TOK
    const2 = lambda shape: pl.BlockSpec(shape, lambda i: (0, 0))
    return pl.pallas_call(
        _combine_kernel,
        grid=(n // TT,),
        in_specs=[pl.BlockSpec((TT, TOP_K), lambda i: (i, 0)),
                  pl.BlockSpec((TT, d), lambda i: (i, 0)),
                  pl.BlockSpec((TOP_K, TT, d // 2), lambda i: (0, i, 0)),
                  const2((d, 2 * SHARED_FF)), const2((SHARED_FF, d)), const2((1, d)), const2((1, d))],
        out_specs=pl.BlockSpec((TT, d), lambda i: (i, 0)),
        out_shape=jax.ShapeDtypeStruct((n, d), F32),
        compiler_params=_cparams("parallel"),
        name="combine",
    )(wgt_tk, h2, yg, w_si, w_sd, g, b)


def _moe_ln(h2, h2_packed, w_router, r_bias, w_e_in_all, w_e_dn_all, layer, w_s_in, w_s_dn, ln_g, ln_b):
    n, d = h2.shape
    wt = w_router.T
    wt_hi = wt.astype(BF16)
    wt_lo = (wt - wt_hi.astype(F32)).astype(BF16)
    bias_col = jnp.broadcast_to(r_bias.astype(F32)[:, None], (N_EXPERTS, LANES))
    eidx, wgt, rank, cnt = _router(h2, wt_hi, wt_lo, bias_col)
    counts = cnt[:, 0].astype(I32)
    offs = jnp.cumsum(counts) - counts
    offs_col = jnp.broadcast_to(offs.astype(F32)[:, None], (N_EXPERTS, LANES))
    dest = _dest_rows(eidx, rank, offs_col)
    dest_chunks = dest.reshape(TOP_K, n // SC_ROWS, SC_ROWS).transpose(1, 0, 2)
    xs = _dispatch(dest_chunks, h2_packed)
    ys = _gmm(_gmm_schedule(counts, n * TOP_K), xs, w_e_in_all, w_e_dn_all, layer)
    yg = _gather_rows(ys, dest.reshape(-1)).reshape(TOP_K, n, d // 2)
    return _combine(wgt.T, h2, yg, w_s_in.astype(BF16), w_s_dn.astype(BF16), ln_g[None, :], ln_b[None, :])


def _pad_cols(w, width):
    return jnp.pad(w, ((0, 0), (0, width - w.shape[1])))


def kernel(x, w_in_ab, w_gla_gate_up, b_gla_gate, hgrn_norm_g, gla_norm_g, w_out_ab, hgrn_lb_logits, w_in_c, conv_c, b_if_c, w_out_c, w_router, router_bias, w_exp_in, w_exp_down, w_sh_in, w_sh_down, ln_mix_g, ln_mix_b, ln_ffn_g, ln_ffn_b):
    B, T, D = x.shape
    lower_bounds = jnp.cumsum(jax.nn.softmax(hgrn_lb_logits.astype(F32), axis=0), axis=0)
    h = x.reshape(B * T, D)
    for l in range(DEPTH):
        j = l // 2
        ln_g, ln_b = ln_mix_g[l][None, :], ln_mix_b[l][None, :]
        if l % 2 == 0:
            w_in = _pad_cols(w_in_ab[j], AB_WIDTH - GLA_GATE_RANK + LANES).astype(BF16)
            w_up = jnp.pad(w_gla_gate_up[j], ((0, LANES - GLA_GATE_RANK), (0, 0))).astype(BF16)
            h, h_packed = _hgrn_gla_layer(h, B, T, w_in, lower_bounds[l][None, :], w_up, b_gla_gate[j][None, :],
                                          hgrn_norm_g[j][None, :], gla_norm_g[j][None, :],
                                          w_out_ab[j].astype(BF16), ln_g, ln_b)
        else:
            w_in = _pad_cols(w_in_c[j], C_WIDTH - 2 * MLSTM_HEADS + LANES).astype(BF16)
            b_if = jnp.pad(b_if_c[j].astype(F32), (0, LANES - 2 * MLSTM_HEADS))[None, :]
            h, h_packed = _mlstm_layer(h, B, T, w_in, conv_c[j], b_if, w_out_c[j].astype(BF16), ln_g, ln_b)
        h = _moe_ln(h, h_packed, w_router[l], router_bias[l], w_exp_in, w_exp_down, l, w_sh_in[l], w_sh_down[l],
                    ln_ffn_g[l], ln_ffn_b[l])
    return h.reshape(B, T, D)
```

```python
import functools

import jax
import jax.numpy as jnp
from jax import lax
from jax.experimental import pallas as pl
from jax.experimental.pallas import tpu as pltpu
from jax.experimental.pallas import tpu_sc as plsc

F32 = jnp.float32
BF16 = jnp.bfloat16
I32 = jnp.int32

D_MODEL = 1024
DEPTH = 2
HGRN_HEADS = 4
HGRN_HEAD_DIM = 128
HGRN_WIDTH = HGRN_HEADS * HGRN_HEAD_DIM
GLA_HEADS = 4
GLA_KEY_DIM = 64
GLA_VAL_DIM = 128
GLA_KEY_WIDTH = GLA_HEADS * GLA_KEY_DIM
GLA_VAL_WIDTH = GLA_HEADS * GLA_VAL_DIM
GLA_GATE_RANK = 16
GLA_GATE_NORMALIZER = 16.0
LIN_CHUNK = 64
LIN_SUB = 16
AB_WIDTH = 4 * HGRN_WIDTH + 2 * GLA_KEY_WIDTH + 2 * GLA_VAL_WIDTH + GLA_GATE_RANK
MLSTM_HEADS = 4
MLSTM_QK_DIM = 128
MLSTM_V_DIM = 256
MLSTM_QK_WIDTH = MLSTM_HEADS * MLSTM_QK_DIM
MLSTM_V_WIDTH = MLSTM_HEADS * MLSTM_V_DIM
MLSTM_CONV = 4
MLSTM_CHUNK = 128
C_WIDTH = 2 * MLSTM_QK_WIDTH + 2 * MLSTM_V_WIDTH + 2 * MLSTM_HEADS
N_EXPERTS = 256
TOP_K = 8
N_GROUPS = 8
TOPK_GROUPS = 4
GROUP_SIZE = N_EXPERTS // N_GROUPS
EXPERT_FF = 256
SHARED_FF = 256
ROUTED_SCALE = 2.5
ALPHA = (2 * DEPTH) ** 0.25
LN_EPS = 1e-5
RMS_EPS = 1e-6

LANES = 128
VMEM_LIMIT = 56 * 1024 * 1024

MIX_ROWS = 256
MLSTM_ROWS = 512
ROUTE_TOK = 512
COMBINE_TOK = 512
GMM_ROWS = 256
SC_CORES, SC_SUBCORES = 2, 16
SC_WORKERS = SC_CORES * SC_SUBCORES
SC_ROWS = 64


def _cparams(*sem):
    return pltpu.CompilerParams(dimension_semantics=sem, vmem_limit_bytes=VMEM_LIMIT)


def _sigmoid(x):
    return 1.0 / (1.0 + jnp.exp(-x))


def _silu(x):
    return x * _sigmoid(x)


def _log_sigmoid(x):
    return jnp.minimum(x, 0.0) - jnp.log(1.0 + jnp.exp(-jnp.abs(x)))


def _split3(x):
    hi = x.astype(BF16)
    r = x - hi.astype(F32)
    mid = r.astype(BF16)
    lo = (r - mid.astype(F32)).astype(BF16)
    return hi, mid, lo


def _tri_dot_left(tri, x):
    return sum(jnp.dot(tri, p, preferred_element_type=F32) for p in _split3(x))


def _tri_dot_right(x, tri):
    return sum(jnp.dot(p, tri, preferred_element_type=F32) for p in _split3(x))


def _dot_nt(a, b):
    return lax.dot_general(a, b, (((1,), (1,)), ((), ())), preferred_element_type=F32)


def _dot_tn(a, b):
    return lax.dot_general(a, b, (((0,), (0,)), ((), ())), preferred_element_type=F32)


HI16 = -65536


def _pack_bf16_pair(x):
    c = x.shape[1] // 2
    bits = lambda v: lax.bitcast_convert_type(v.astype(BF16).astype(F32), I32)
    return (bits(x[:, c:]) & HI16) | lax.shift_right_logical(bits(x[:, :c]), 16)


def _unpack_bf16_pair(w):
    lo = lax.bitcast_convert_type(lax.shift_left(w, 16), F32)
    hi = lax.bitcast_convert_type(w & HI16, F32)
    return lo.astype(BF16), hi.astype(BF16)


def _layernorm(x, g, b):
    mu = jnp.mean(x, axis=-1, keepdims=True)
    xc = x - mu
    var = jnp.mean(xc * xc, axis=-1, keepdims=True)
    return xc * lax.rsqrt(var + LN_EPS) * g + b


N_UNITS = HGRN_HEADS + GLA_HEADS // 2
HALF_LANES = LANES // 2
LOG2E = 1.4426950408889634


def _pair_selector():
    assert LIN_CHUNK == HALF_LANES == GLA_KEY_DIM
    r = jnp.arange(LIN_SUB * LANES)[:, None]
    c = jnp.arange(LANES)[None, :]
    same_slot = (r // LANES) == (c % LIN_SUB)
    same_half = ((r % LANES) < HALF_LANES) == (c < HALF_LANES)
    return (same_slot & same_half).astype(BF16)


def _pairwise_tiles(q, k, b, lhs_ref, row0):
    c = LIN_SUB
    b2 = b * LOG2E
    for blk in range(LIN_CHUNK // c):
        lo = blk * c
        qb, kb, bb = q[lo:lo + c], k[lo:lo + c], b2[lo:lo + c]
        for j in range(c):
            e = jnp.exp2(jnp.minimum(bb - bb[j:j + 1], 0.0))
            lhs_ref[row0 + lo:row0 + lo + c, j * LANES:(j + 1) * LANES] = (qb * kb[j:j + 1] * e).astype(BF16)


def _unit_chunk(q, k, b, vs, r, st_ref, masks):
    L, c = LIN_CHUNK, LIN_SUB
    lane_lo, col_mod, same_blk_causal = masks
    heads = [lane_lo, ~lane_lo] if len(vs) == 2 else [None]

    def pick(x, m):
        return x if m is None else jnp.where(m, x, jnp.zeros_like(x))

    g = b[L - 1:L, :]
    st = st_ref[...]
    st_b = st.astype(BF16)
    qx = (q * jnp.exp(b)).astype(BF16)
    outs = [_dot_nt(pick(qx, m), st_b) for m in heads]
    k_end = (k * jnp.exp(g - b)).astype(BF16)
    upd = [_dot_tn(v.astype(BF16), k_end) for v in vs]
    st_ref[...] = st * jnp.exp(g) + (upd[0] if len(vs) == 1 else jnp.where(lane_lo, upd[0], upd[1]))

    off_rows = [jnp.zeros((c, LANES), F32)]
    for blk in range(1, L // c):
        lo = blk * c
        ref = b[lo - 1:lo]
        q_in = (q[lo:lo + c] * jnp.exp(b[lo:lo + c] - ref)).astype(BF16)
        k_in = (k * jnp.exp(jnp.minimum(ref - b, 0.0))).astype(BF16)
        if len(vs) == 2:
            k_cat = jnp.concatenate([pick(k_in, heads[0]), pick(k_in, heads[1])], axis=0)
        else:
            k_cat = jnp.concatenate([k_in, jnp.zeros_like(k_in)], axis=0)
        off_rows.append(jnp.where(col_mod < lo, _dot_nt(q_in, k_cat), 0.0))
    a = jnp.where(same_blk_causal, r, jnp.concatenate(off_rows, axis=0)).astype(BF16)
    v_cat = jnp.concatenate([vs[0], vs[-1]], axis=0).astype(BF16)
    return [o + jnp.dot(pick(a, m), v_cat, preferred_element_type=F32) for o, m in zip(outs, heads)]


def _rms_gate(o, g, gate):
    o = o * lax.rsqrt(jnp.mean(o * o, axis=-1, keepdims=True) + RMS_EPS)
    return o * g * _silu(gate)


class _Cols:
    def __init__(self, ref, off, width):
        self.ref, self.off, self.width = ref, off, width

    def __getitem__(self, idx):
        rows, cols = (slice(None), slice(None)) if idx is Ellipsis else idx
        lo = self.off + (cols.start or 0)
        hi = self.off + (self.width if cols.stop is None else cols.stop)
        return self.ref[rows, lo:hi]


def _project_residual_ln(x_ref, y_ref, wout_ref, g_ref, b_ref, h_ref, hp_ref):
    mix = jnp.dot(y_ref[...], wout_ref[...], preferred_element_type=F32)
    h = _layernorm(ALPHA * x_ref[...] + mix, g_ref[...], b_ref[...])
    h_ref[...] = h
    hp_ref[...] = _pack_bf16_pair(h)


def _hgrn_gla_kernel(x_ref, win_ref, lb_ref, wup_ref, bgk_ref, hng_ref, gng_ref, sel_ref, wout_ref, lng_ref, lnb_ref,
                     h_ref, hp_ref, proj_ref, y_ref, lhs_ref, *st_refs):
    @pl.when(pl.program_id(1) == 0)
    def _():
        for r in st_refs:
            r[...] = jnp.zeros_like(r)

    proj_ref[...] = jnp.dot(x_ref[...].astype(BF16), win_ref[...], preferred_element_type=F32)
    W, KW, VW = HGRN_WIDTH, GLA_KEY_WIDTH, GLA_VAL_WIDTH
    hq_ref, hf_ref, hi_ref, hg_ref = (_Cols(proj_ref, i * W, W) for i in range(4))
    gq_ref, gk_ref = _Cols(proj_ref, 4 * W, KW), _Cols(proj_ref, 4 * W + KW, KW)
    gv_ref, gg_ref = _Cols(proj_ref, 4 * W + 2 * KW, VW), _Cols(proj_ref, 4 * W + 2 * KW + VW, VW)
    glr_ref = _Cols(proj_ref, AB_WIDTH - GLA_GATE_RANK, LANES)

    L = LIN_CHUNK
    n_chunks = MIX_ROWS // L
    r_i = lax.broadcasted_iota(I32, (L, LANES), 0)
    c_i = lax.broadcasted_iota(I32, (L, LANES), 1)
    col_mod = jnp.where(c_i < HALF_LANES, c_i, c_i - HALF_LANES)
    same_blk_causal = (col_mod // LIN_SUB == r_i // LIN_SUB) & (col_mod <= r_i)
    c_sub = lax.broadcasted_iota(I32, (LIN_SUB, LANES), 1)
    col_mod_sub = jnp.where(c_sub < HALF_LANES, c_sub, c_sub - HALF_LANES)
    lane_lo = lax.broadcasted_iota(I32, (1, LANES), 1) < HALF_LANES
    masks = (lane_lo, col_mod_sub, same_blk_causal)
    tril = (lax.broadcasted_iota(I32, (L, L), 0) >= lax.broadcasted_iota(I32, (L, L), 1)).astype(BF16)
    lb = lb_ref[...]

    units = []
    for ck in range(n_chunks):
        rs = slice(ck * L, (ck + 1) * L)
        u = jnp.dot(glr_ref[rs, :].astype(BF16), wup_ref[...], preferred_element_type=F32) + bgk_ref[...]
        la_g = _log_sigmoid(u) * (1.0 / GLA_GATE_NORMALIZER)
        qs, ks, las = [], [], []
        for h in range(HGRN_HEADS):
            cs = slice(h * LANES, (h + 1) * LANES)
            z = hf_ref[rs, cs]
            lbh = lb[:, cs]
            qs.append(_silu(hq_ref[rs, cs]))
            ks.append((1.0 - lbh) * _sigmoid(-z))
            las.append(jnp.log(lbh + (1.0 - lbh) * _sigmoid(z)))
        for p in range(GLA_HEADS // 2):
            cs = slice(p * LANES, (p + 1) * LANES)
            qs.append(gq_ref[rs, cs] * (GLA_KEY_DIM ** -0.5))
            ks.append(gk_ref[rs, cs])
            las.append(la_g[:, cs])
        b_all = _tri_dot_left(tril, jnp.concatenate(las, axis=1))
        for un in range(N_UNITS):
            b = b_all[:, un * LANES:(un + 1) * LANES]
            _pairwise_tiles(qs[un], ks[un], b, lhs_ref, (ck * N_UNITS + un) * L)
            units.append((qs[un], ks[un], b))

    r_all = jnp.dot(lhs_ref[...], sel_ref[...], preferred_element_type=F32)

    for ck in range(n_chunks):
        rs = slice(ck * L, (ck + 1) * L)
        for un in range(N_UNITS):
            uc = ck * N_UNITS + un
            q, k, b = units[uc]
            r = r_all[uc * L:(uc + 1) * L]
            if un < HGRN_HEADS:
                cs = slice(un * LANES, (un + 1) * LANES)
                (o,) = _unit_chunk(q, k, b, [hi_ref[rs, cs]], r, st_refs[un], masks)
                y_ref[rs, cs] = _rms_gate(o, hng_ref[:, cs], hg_ref[rs, cs]).astype(y_ref.dtype)
            else:
                p = un - HGRN_HEADS
                vcs = [slice((2 * p + i) * GLA_VAL_DIM, (2 * p + i + 1) * GLA_VAL_DIM) for i in range(2)]
                outs = _unit_chunk(q, k, b, [gv_ref[rs, vc] for vc in vcs], r, st_refs[un], masks)
                for o, vc in zip(outs, vcs):
                    ys = slice(HGRN_WIDTH + vc.start, HGRN_WIDTH + vc.stop)
                    y_ref[rs, ys] = _rms_gate(o, gng_ref[:, vc], gg_ref[rs, vc]).astype(y_ref.dtype)

    _project_residual_ln(x_ref, y_ref, wout_ref, lng_ref, lnb_ref, h_ref, hp_ref)


def _mixer_layer_call(kernel_fn, name, x2, batch, seq, rows, consts, scratch_shapes):
    n, d = x2.shape
    steps = seq // rows
    row_spec = lambda width: pl.BlockSpec((rows, width), lambda b, i: (b * steps + i, 0))
    const_spec = lambda a: pl.BlockSpec(a.shape, lambda b, i: (0,) * a.ndim)
    return pl.pallas_call(
        kernel_fn,
        grid=(batch, steps),
        in_specs=[row_spec(d)] + [const_spec(a) for a in consts],
        out_specs=[row_spec(d), row_spec(d // 2)],
        out_shape=[jax.ShapeDtypeStruct((n, d), F32), jax.ShapeDtypeStruct((n, d // 2), I32)],
        scratch_shapes=scratch_shapes,
        compiler_params=_cparams("parallel", "arbitrary"),
        name=name,
    )(x2, *consts)


def _hgrn_gla_layer(x2, batch, seq, w_in, lb, w_up, b_gk, hgrn_g, gla_g, w_out, ln_g, ln_b):
    R = MIX_ROWS
    n_uc = (R // LIN_CHUNK) * N_UNITS
    scratch = ([pltpu.VMEM((R, w_in.shape[1]), F32), pltpu.VMEM((R, w_out.shape[0]), BF16),
                pltpu.VMEM((n_uc * LIN_CHUNK, LIN_SUB * LANES), BF16)]
               + [pltpu.VMEM((GLA_VAL_DIM, LANES), F32)] * N_UNITS)
    consts = (w_in, lb, w_up, b_gk, hgrn_g, gla_g, _pair_selector(), w_out, ln_g, ln_b)
    return _mixer_layer_call(_hgrn_gla_kernel, "hgrn_gla", x2, batch, seq, R, consts, scratch)


MLSTM_CONV_PAD = 8


def _mlstm_kernel(x_ref, win_ref, cw_ref, bif_ref, wout_ref, lng_ref, lnb_ref, h_ref, hp_ref,
                  proj_ref, y_ref, ext_ref, *state_refs):
    L, R, PAD = MLSTM_CHUNK, MLSTM_ROWS, MLSTM_CONV_PAD
    cn_refs, m_refs = state_refs[:MLSTM_HEADS], state_refs[MLSTM_HEADS:]

    @pl.when(pl.program_id(1) == 0)
    def _():
        ext_ref[0:PAD, :] = jnp.zeros((PAD, ext_ref.shape[1]), F32)
        for r in state_refs:
            r[...] = jnp.zeros_like(r)

    proj_ref[...] = jnp.dot(x_ref[...].astype(BF16), win_ref[...], preferred_element_type=F32)
    QK2, VW = 2 * MLSTM_QK_WIDTH, MLSTM_V_WIDTH
    qk_ref, v_ref = _Cols(proj_ref, 0, QK2), _Cols(proj_ref, QK2, VW)
    og_ref, gt_ref = _Cols(proj_ref, QK2 + VW, VW), _Cols(proj_ref, QK2 + 2 * VW, LANES)

    ext_ref[PAD:PAD + R, :] = qk_ref[...]
    conv = jnp.zeros((R, ext_ref.shape[1]), F32)
    for w in range(MLSTM_CONV):
        conv = conv + cw_ref[w:w + 1, :] * ext_ref[pl.ds(PAD - (MLSTM_CONV - 1) + w, R), :]
    ext_ref[0:PAD, :] = ext_ref[R:R + PAD, :]
    qk_all = _silu(conv)

    r_i = lax.broadcasted_iota(I32, (L, L), 0)
    c_i = lax.broadcasted_iota(I32, (L, L), 1)
    causal = c_i <= r_i
    tril = causal.astype(BF16)
    triu = (r_i <= c_i).astype(BF16)
    ones_col = (lax.broadcasted_iota(I32, (L, LANES), 1) == 0).astype(F32)

    H = MLSTM_HEADS
    for ck in range(R // L):
        rs = slice(ck * L, (ck + 1) * L)
        qk = qk_all[rs]
        gt = gt_ref[rs, :] + bif_ref[...]
        gt_t = gt.T
        b_cols = _tri_dot_left(tril, _log_sigmoid(gt))
        b_rows = _tri_dot_right(_log_sigmoid(gt_t), triu)
        for h in range(H):
            q = qk[:, h * MLSTM_QK_DIM:(h + 1) * MLSTM_QK_DIM].astype(BF16)
            kf = qk[:, MLSTM_QK_WIDTH + h * MLSTM_QK_DIM:MLSTM_QK_WIDTH + (h + 1) * MLSTM_QK_DIM] * (MLSTM_QK_DIM ** -0.5)
            vs = slice(h * MLSTM_V_DIM, (h + 1) * MLSTM_V_DIM)
            v_ext = jnp.concatenate([v_ref[rs, vs], ones_col], axis=1).astype(BF16)
            bc, br = b_cols[:, H + h:H + h + 1], b_rows[H + h:H + h + 1, :]
            ic, ir = gt[:, h:h + 1], gt_t[h:h + 1, :]
            m_prev = m_refs[h][0:1, 0:1]
            g = bc[L - 1:L, :]
            dmat = jnp.where(causal, bc - br + ir, -jnp.inf)
            inter = bc + m_prev
            m_j = jnp.maximum(inter, jnp.max(dmat, axis=-1, keepdims=True))
            s = _dot_nt(q, kf.astype(BF16)) * jnp.exp(dmat - m_j)
            w_inter = jnp.exp(inter - m_j)
            cn = cn_refs[h][...]
            nd = (jnp.dot(s.astype(BF16), v_ext, preferred_element_type=F32)
                  + w_inter * jnp.dot(q, cn.astype(BF16), preferred_element_type=F32))
            num, den = nd[:, :MLSTM_V_DIM], nd[:, MLSTM_V_DIM:MLSTM_V_DIM + 1]
            hid = num / jnp.maximum(jnp.abs(den), jnp.exp(-m_j))
            u = g - bc + ic
            m_new = jnp.maximum(g + m_prev, jnp.max(u, axis=0, keepdims=True))
            wk = (kf * jnp.exp(u - m_new)).astype(BF16)
            cn_refs[h][...] = jnp.exp(g + m_prev - m_new) * cn + _dot_tn(wk, v_ext)
            m_refs[h][...] = jnp.broadcast_to(m_new, m_refs[h].shape)
            y_ref[rs, vs] = (hid * _sigmoid(og_ref[rs, vs])).astype(y_ref.dtype)

    _project_residual_ln(x_ref, y_ref, wout_ref, lng_ref, lnb_ref, h_ref, hp_ref)


def _mlstm_layer(x2, batch, seq, w_in, conv_w, b_if, w_out, ln_g, ln_b):
    R = MLSTM_ROWS
    assert R % MLSTM_CHUNK == 0
    scratch = ([pltpu.VMEM((R, w_in.shape[1]), F32), pltpu.VMEM((R, w_out.shape[0]), BF16),
                pltpu.VMEM((R + MLSTM_CONV_PAD, 2 * MLSTM_QK_WIDTH), F32)]
               + [pltpu.VMEM((MLSTM_QK_DIM, MLSTM_V_DIM + LANES), F32)] * MLSTM_HEADS
               + [pltpu.VMEM((8, LANES), F32)] * MLSTM_HEADS)
    consts = (w_in, conv_w, b_if, w_out, ln_g, ln_b)
    return _mixer_layer_call(_mlstm_kernel, "mlstm", x2, batch, seq, R, consts, scratch)


def _router_kernel(h_ref, whi_ref, wlo_ref, bias_ref, eidx_ref, wgt_ref, rank_ref, cnt_ref, carry_ref):
    TT, E, G, GS = ROUTE_TOK, N_EXPERTS, N_GROUPS, GROUP_SIZE

    @pl.when(pl.program_id(0) == 0)
    def _():
        carry_ref[...] = jnp.zeros_like(carry_ref)

    h = h_ref[...]
    h_hi = h.astype(BF16)
    h_lo = (h - h_hi.astype(F32)).astype(BF16)
    logits = _dot_nt(whi_ref[...], h_hi) + _dot_nt(whi_ref[...], h_lo) + _dot_nt(wlo_ref[...], h_hi)
    scores = _sigmoid(logits)
    biased = scores + bias_ref[:, 0:1]
    neg = -jnp.inf

    io_g = lax.broadcasted_iota(I32, (GS, TT), 0)
    io8 = lax.broadcasted_iota(I32, (G, TT), 0)
    gs = jnp.zeros((G, TT), F32)
    for g in range(G):
        blk = biased[g * GS:(g + 1) * GS, :]
        m1 = jnp.max(blk, axis=0, keepdims=True)
        i1 = jnp.min(jnp.where(blk == m1, io_g, GS), axis=0, keepdims=True)
        m2 = jnp.max(jnp.where(io_g == i1, neg, blk), axis=0, keepdims=True)
        gs = jnp.where(io8 == g, m1 + m2, gs)
    gsel = jnp.zeros((G, TT), F32)
    for _ in range(TOPK_GROUPS):
        m = jnp.max(gs, axis=0, keepdims=True)
        idx = jnp.min(jnp.where(gs == m, io8, G), axis=0, keepdims=True)
        hit = io8 == idx
        gsel = jnp.where(hit, 1.0, gsel)
        gs = jnp.where(hit, neg, gs)
    sel = jnp.concatenate([jnp.broadcast_to(gsel[g:g + 1, :], (GS, TT)) for g in range(G)], axis=0)
    masked = jnp.where(sel > 0.0, biased, neg)

    io_e = lax.broadcasted_iota(I32, (E, TT), 0)
    eidx = jnp.zeros((TOP_K, TT), I32)
    wsel = jnp.zeros((TOP_K, TT), F32)
    chosen = jnp.zeros((E, TT), F32)
    for k in range(TOP_K):
        m = jnp.max(masked, axis=0, keepdims=True)
        idx = jnp.min(jnp.where(masked == m, io_e, E), axis=0, keepdims=True)
        hit = io_e == idx
        sc = jnp.sum(jnp.where(hit, scores, 0.0), axis=0, keepdims=True)
        eidx = jnp.where(io8 == k, idx, eidx)
        wsel = jnp.where(io8 == k, sc, wsel)
        chosen = jnp.where(hit, 1.0, chosen)
        masked = jnp.where(hit, neg, masked)
    wgt_ref[...] = wsel / jnp.sum(wsel, axis=0, keepdims=True) * ROUTED_SCALE
    eidx_ref[...] = eidx

    triu = (lax.broadcasted_iota(I32, (TT, TT), 0) <= lax.broadcasted_iota(I32, (TT, TT), 1)).astype(BF16)
    cum = jnp.dot(chosen.astype(BF16), triu, preferred_element_type=F32)
    carry = carry_ref[:, 0:1]
    before = cum - chosen + carry
    rank = jnp.zeros((TOP_K, TT), F32)
    for k in range(TOP_K):
        r = jnp.sum(jnp.where(io_e == eidx[k:k + 1, :], before, 0.0), axis=0, keepdims=True)
        rank = jnp.where(io8 == k, r, rank)
    rank_ref[...] = rank.astype(I32)
    total = carry + cum[:, TT - 1:TT]
    carry_ref[...] = jnp.broadcast_to(total, carry_ref.shape)
    cnt_ref[...] = jnp.broadcast_to(total, cnt_ref.shape)


def _router(h2, wt_hi, wt_lo, bias_col):
    n, d = h2.shape
    TT, E = ROUTE_TOK, N_EXPERTS
    tok_spec = pl.BlockSpec((TOP_K, TT), lambda i: (0, i))
    return pl.pallas_call(
        _router_kernel,
        grid=(n // TT,),
        in_specs=[pl.BlockSpec((TT, d), lambda i: (i, 0)),
                  pl.BlockSpec((E, d), lambda i: (0, 0)),
                  pl.BlockSpec((E, d), lambda i: (0, 0)),
                  pl.BlockSpec((E, LANES), lambda i: (0, 0))],
        out_specs=[tok_spec, tok_spec, tok_spec, pl.BlockSpec((E, LANES), lambda i: (0, 0))],
        out_shape=[jax.ShapeDtypeStruct((TOP_K, n), I32), jax.ShapeDtypeStruct((TOP_K, n), F32),
                   jax.ShapeDtypeStruct((TOP_K, n), I32), jax.ShapeDtypeStruct((E, LANES), F32)],
        scratch_shapes=[pltpu.VMEM((E, LANES), F32)],
        compiler_params=_cparams("arbitrary"),
        name="router",
    )(h2, wt_hi, wt_lo, bias_col)


def _dest_kernel(eidx_ref, rank_ref, offs_ref, dest_ref):
    TT, E = eidx_ref.shape[1], N_EXPERTS
    io_e = lax.broadcasted_iota(I32, (E, TT), 0)
    io8 = lax.broadcasted_iota(I32, (TOP_K, TT), 0)
    offs = offs_ref[:, 0:1]
    base = jnp.zeros((TOP_K, TT), F32)
    for k in range(TOP_K):
        r = jnp.sum(jnp.where(io_e == eidx_ref[k:k + 1, :], offs, 0.0), axis=0, keepdims=True)
        base = jnp.where(io8 == k, r, base)
    dest_ref[...] = base.astype(I32) + rank_ref[...]


def _dest_rows(eidx, rank, offs_col):
    n = eidx.shape[1]
    TT = 512
    spec = pl.BlockSpec((TOP_K, TT), lambda i: (0, i))
    return pl.pallas_call(
        _dest_kernel,
        grid=(n // TT,),
        in_specs=[spec, spec, pl.BlockSpec((N_EXPERTS, LANES), lambda i: (0, 0))],
        out_specs=spec,
        out_shape=jax.ShapeDtypeStruct((TOP_K, n), I32),
        compiler_params=_cparams("parallel"),
        name="dest_rows",
    )(eidx, rank, offs_col)


def _sc_mesh():
    return plsc.VectorSubcoreMesh(core_axis_name="c", subcore_axis_name="s")


def _sc_worker_id():
    return lax.axis_index("s") * SC_CORES + lax.axis_index("c")


def _dispatch(dest_chunks, x2):
    n, d = x2.shape
    n_chunks = n // SC_ROWS // SC_WORKERS

    @functools.partial(
        pl.kernel, mesh=_sc_mesh(),
        out_type=jax.ShapeDtypeStruct((n * TOP_K, d), x2.dtype),
        scratch_types=[pltpu.VMEM((2, TOP_K, SC_ROWS), I32), pltpu.VMEM((2, SC_ROWS, d), x2.dtype),
                       pltpu.SemaphoreType.DMA((2,))],
    )
    def scatter_rows(x_hbm, idx_hbm, out_hbm, idx_v, rows_v, wsem):
        base = _sc_worker_id() * n_chunks

        def row_scatter(b, k):
            return pltpu.make_async_copy(rows_v.at[b], out_hbm.at[idx_v.at[b, k]], wsem.at[b])

        def drain(b):
            for k in range(TOP_K):
                row_scatter(b, k).wait()

        @pl.loop(0, n_chunks, step=2)
        def _(i):
            for b in range(2):
                c = base + i + b

                @pl.when(i > 0)
                def _():
                    drain(b)

                pltpu.sync_copy(idx_hbm.at[c], idx_v.at[b])
                pltpu.sync_copy(x_hbm.at[pl.ds(c * SC_ROWS, SC_ROWS)], rows_v.at[b])
                for k in range(TOP_K):
                    row_scatter(b, k).start()

        for b in range(2):
            drain(b)

    return scatter_rows(x2, dest_chunks)


def _gather_rows(src, idx):
    d = src.shape[1]
    p = idx.shape[0]
    per_w = p // SC_WORKERS
    n_chunks = per_w // SC_ROWS

    @functools.partial(
        pl.kernel, mesh=_sc_mesh(),
        out_type=jax.ShapeDtypeStruct((p, d), src.dtype),
        scratch_types=[pltpu.VMEM((2, SC_ROWS), I32), pltpu.VMEM((2, SC_ROWS, d), src.dtype),
                       pltpu.SemaphoreType.DMA((2,)), pltpu.SemaphoreType.DMA((2,))],
    )
    def gather_rows(src_hbm, idx_hbm, out_hbm, idx_v, rows_v, gsem, wsem):
        base = _sc_worker_id() * per_w

        def row_gather(b):
            return pltpu.make_async_copy(src_hbm.at[idx_v.at[b]], rows_v.at[b], gsem.at[b])

        def write_back(b, off):
            return pltpu.make_async_copy(rows_v.at[b], out_hbm.at[pl.ds(off, SC_ROWS)], wsem.at[b])

        @pl.loop(0, n_chunks, step=2)
        def _(i):
            for b in range(2):
                off = base + (i + b) * SC_ROWS

                @pl.when(i > 0)
                def _():
                    write_back(b, off).wait()

                pltpu.sync_copy(idx_hbm.at[pl.ds(off, SC_ROWS)], idx_v.at[b])
                row_gather(b).start()
            for b in range(2):
                row_gather(b).wait()
                write_back(b, base + (i + b) * SC_ROWS).start()

        for b in range(2):
            write_back(b, base).wait()

    return gather_rows(src, idx)


GMM_X_SLOTS = 4
GMM_O_SLOTS = 3


def _gmm_kernel(tile_ref, grp_ref, lo_ref, hi_ref, first_ref, last_ref, newgrp_ref, nextgrp_ref, nreal_ref,
                xs_hbm, win_hbm, wdn_hbm, ys_hbm,
                xbuf, obuf, winbuf, wdnbuf, win_bf, wdn_bf, xsem, osem, wsem, *, layer):
    TM = GMM_ROWS
    n_tiles = xs_hbm.shape[0] // TM
    half = xs_hbm.shape[1]

    def x_copy(t, slot):
        return pltpu.make_async_copy(xs_hbm.at[pl.ds(t * TM, TM)], xbuf.at[slot], xsem.at[slot])

    def o_copy(t, slot):
        return pltpu.make_async_copy(obuf.at[slot], ys_hbm.at[pl.ds(t * TM, TM)], osem.at[slot])

    def w_copies(e, slot):
        return (pltpu.make_async_copy(win_hbm.at[layer, e], winbuf.at[slot], wsem.at[0, slot]),
                pltpu.make_async_copy(wdn_hbm.at[layer, e], wdnbuf.at[slot], wsem.at[1, slot]))

    for t in range(GMM_X_SLOTS - 1):
        x_copy(t, t).start()
    for c in w_copies(grp_ref[0], 0):
        c.start()

    def visit(v, wslot):
        t = tile_ref[v]
        xslot = t % GMM_X_SLOTS
        oslot = t % GMM_O_SLOTS
        is_first = first_ref[v] == 1
        is_new = newgrp_ref[v] == 1
        wslot = jnp.where(is_new & (v > 0), 1 - wslot, wslot)

        @pl.when(is_first)
        def _():
            x_copy(t, xslot).wait()
            ahead = t + GMM_X_SLOTS - 1

            @pl.when(ahead < n_tiles)
            def _():
                x_copy(ahead, ahead % GMM_X_SLOTS).start()

            @pl.when(t >= GMM_O_SLOTS)
            def _():
                o_copy(t - GMM_O_SLOTS, oslot).wait()

        @pl.when(is_new)
        def _():
            for c in w_copies(grp_ref[v], wslot):
                c.wait()
            win_bf[...] = winbuf[wslot].astype(BF16)
            wdn_bf[...] = wdnbuf[wslot].astype(BF16)
            nxt = nextgrp_ref[v]

            @pl.when(nxt >= 0)
            def _():
                for c in w_copies(nxt, 1 - wslot):
                    c.start()

        x_lo, x_hi = _unpack_bf16_pair(xbuf[xslot])
        gu = (jnp.dot(x_lo, win_bf[:half, :], preferred_element_type=F32)
              + jnp.dot(x_hi, win_bf[half:, :], preferred_element_type=F32))
        act = (_silu(gu[:, :EXPERT_FF]) * gu[:, EXPERT_FF:]).astype(BF16)
        y = _pack_bf16_pair(jnp.dot(act, wdn_bf[...], preferred_element_type=F32))
        rows = lax.broadcasted_iota(I32, (TM, 1), 0)
        mine = (rows >= lo_ref[v]) & (rows < hi_ref[v])

        @pl.when(is_first)
        def _():
            obuf[oslot] = jnp.where(mine, y, 0)

        @pl.when(jnp.logical_not(is_first))
        def _():
            obuf[oslot] = jnp.where(mine, y, obuf[oslot])

        @pl.when(last_ref[v] == 1)
        def _():
            o_copy(t, oslot).start()

        return wslot

    lax.fori_loop(0, nreal_ref[0], visit, jnp.int32(0))
    for t in range(n_tiles - GMM_O_SLOTS, n_tiles):
        o_copy(t, t % GMM_O_SLOTS).wait()


def _gmm(meta, xs, w_in_all, w_dn_all, layer):
    p, half = xs.shape
    d = 2 * half
    TM = GMM_ROWS
    assert p % TM == 0 and p // TM >= GMM_X_SLOTS
    any_spec = pl.BlockSpec(memory_space=pl.ANY)
    grid_spec = pltpu.PrefetchScalarGridSpec(
        num_scalar_prefetch=len(meta),
        grid=(1,),
        in_specs=[any_spec, any_spec, any_spec],
        out_specs=any_spec,
        scratch_shapes=[pltpu.VMEM((GMM_X_SLOTS, TM, half), I32), pltpu.VMEM((GMM_O_SLOTS, TM, half), I32),
                        pltpu.VMEM((2, d, 2 * EXPERT_FF), F32), pltpu.VMEM((2, EXPERT_FF, d), F32),
                        pltpu.VMEM((d, 2 * EXPERT_FF), BF16), pltpu.VMEM((EXPERT_FF, d), BF16),
                        pltpu.SemaphoreType.DMA((GMM_X_SLOTS,)), pltpu.SemaphoreType.DMA((GMM_O_SLOTS,)),
                        pltpu.SemaphoreType.DMA((2, 2))],
    )
    return pl.pallas_call(
        functools.partial(_gmm_kernel, layer=layer),
        grid_spec=grid_spec,
        out_shape=jax.ShapeDtypeStruct((p, half), I32),
        compiler_params=_cparams("arbitrary"),
        name="gmm",
    )(*meta, xs, w_in_all, w_dn_all)


def _gmm_schedule(counts, n_rows):
    TM = GMM_ROWS
    E = counts.shape[0]
    max_visits = n_rows // TM + E - 1
    ends = jnp.cumsum(counts)
    starts = ends - counts
    first_tile = starts // TM
    n_vis = jnp.where(counts > 0, jnp.maximum(ends - 1, 0) // TM - first_tile + 1, 0)
    vis_end = jnp.cumsum(n_vis)
    vis_start = vis_end - n_vis
    n_real = vis_end[-1]
    idx = jnp.arange(max_visits, dtype=I32)
    v = jnp.minimum(idx, jnp.maximum(n_real - 1, 0))
    real = idx < n_real
    grp = jnp.minimum(jnp.sum((vis_end[None, :] <= v[:, None]).astype(I32), axis=1), E - 1)
    onehot = grp[:, None] == jnp.arange(E, dtype=I32)[None, :]

    def take(table):
        return jnp.sum(jnp.where(onehot, table[None, :], 0), axis=1)

    tile = take(first_tile) + (v - take(vis_start))
    lo = jnp.where(real, jnp.maximum(take(starts), tile * TM) - tile * TM, 0)
    hi = jnp.where(real, jnp.minimum(take(ends), (tile + 1) * TM) - tile * TM, 0)
    prev = lambda a: jnp.concatenate([jnp.full((1,), -1, I32), a[:-1]])
    first = real & (tile != prev(tile))
    newgrp = real & (grp != prev(grp))
    last = real & (jnp.concatenate([first[1:], jnp.ones((1,), bool)]) | (idx == n_real - 1))
    cand = jnp.where(counts > 0, jnp.arange(E, dtype=I32), E)
    later = jnp.concatenate([jnp.flip(lax.cummin(jnp.flip(cand)))[1:], jnp.full((1,), E, I32)])
    nextgrp = take(jnp.where(later < E, later, -1))
    meta = tuple(a.astype(I32) for a in (tile, grp, lo, hi, first, last, newgrp, nextgrp))
    return meta + (n_real.astype(I32).reshape(1),)


def _combine_kernel(wgt_ref, h_ref, yg_ref, wsi_ref, wsd_ref, g_ref, b_ref, o_ref):
    h = h_ref[...]
    gu = jnp.dot(h.astype(BF16), wsi_ref[...], preferred_element_type=F32)
    act = (_silu(gu[:, :SHARED_FF]) * gu[:, SHARED_FF:]).astype(BF16)
    acc = ALPHA * h + jnp.dot(act, wsd_ref[...], preferred_element_type=F32)
    wgt = wgt_ref[...]
    half = yg_ref.shape[2]
    r_lo = jnp.zeros((h.shape[0], half), F32)
    r_hi = jnp.zeros((h.shape[0], half), F32)
    for k in range(TOP_K):
        y_lo, y_hi = _unpack_bf16_pair(yg_ref[k])
        r_lo = r_lo + wgt[:, k:k + 1] * y_lo.astype(F32)
        r_hi = r_hi + wgt[:, k:k + 1] * y_hi.astype(F32)
    acc = acc + jnp.concatenate([r_lo, r_hi], axis=1)
    o_ref[...] = _layernorm(acc, g_ref[...], b_ref[...])


def _combine(wgt_tk, h2, yg, w_si, w_sd, g, b):
    n, d = h2.shape
    TT = COMBINE_TOK
    const2 = lambda shape: pl.BlockSpec(shape, lambda i: (0, 0))
    return pl.pallas_call(
        _combine_kernel,
        grid=(n // TT,),
        in_specs=[pl.BlockSpec((TT, TOP_K), lambda i: (i, 0)),
                  pl.BlockSpec((TT, d), lambda i: (i, 0)),
                  pl.BlockSpec((TOP_K, TT, d // 2), lambda i: (0, i, 0)),
                  const2((d, 2 * SHARED_FF)), const2((SHARED_FF, d)), const2((1, d)), const2((1, d))],
        out_specs=pl.BlockSpec((TT, d), lambda i: (i, 0)),
        out_shape=jax.ShapeDtypeStruct((n, d), F32),
        compiler_params=_cparams("parallel"),
        name="combine",
    )(wgt_tk, h2, yg, w_si, w_sd, g, b)


def _moe_ln(h2, h2_packed, w_router, r_bias, w_e_in_all, w_e_dn_all, layer, w_s_in, w_s_dn, ln_g, ln_b):
    n, d = h2.shape
    wt = w_router.T
    wt_hi = wt.astype(BF16)
    wt_lo = (wt - wt_hi.astype(F32)).astype(BF16)
    bias_col = jnp.broadcast_to(r_bias.astype(F32)[:, None], (N_EXPERTS, LANES))
    eidx, wgt, rank, cnt = _router(h2, wt_hi, wt_lo, bias_col)
    counts = cnt[:, 0].astype(I32)
    offs = jnp.cumsum(counts) - counts
    offs_col = jnp.broadcast_to(offs.astype(F32)[:, None], (N_EXPERTS, LANES))
    dest = _dest_rows(eidx, rank, offs_col)
    dest_chunks = dest.reshape(TOP_K, n // SC_ROWS, SC_ROWS).transpose(1, 0, 2)
    xs = _dispatch(dest_chunks, h2_packed)
    ys = _gmm(_gmm_schedule(counts, n * TOP_K), xs, w_e_in_all, w_e_dn_all, layer)
    yg = _gather_rows(ys, dest.reshape(-1)).reshape(TOP_K, n, d // 2)
    return _combine(wgt.T, h2, yg, w_s_in.astype(BF16), w_s_dn.astype(BF16), ln_g[None, :], ln_b[None, :])


def _pad_cols(w, width):
    return jnp.pad(w, ((0, 0), (0, width - w.shape[1])))


def kernel(x, w_in_ab, w_gla_gate_up, b_gla_gate, hgrn_norm_g, gla_norm_g, w_out_ab, hgrn_lb_logits, w_in_c, conv_c, b_if_c, w_out_c, w_router, router_bias, w_exp_in, w_exp_down, w_sh_in, w_sh_down, ln_mix_g, ln_mix_b, ln_ffn_g, ln_ffn_b):
    B, T, D = x.shape
    lower_bounds = jnp.cumsum(jax.nn.softmax(hgrn_lb_logits.astype(F32), axis=0), axis=0)
    h = x.reshape(B * T, D)
    for l in range(DEPTH):
        j = l // 2
        ln_g, ln_b = ln_mix_g[l][None, :], ln_mix_b[l][None, :]
        if l % 2 == 0:
            w_in = _pad_cols(w_in_ab[j], AB_WIDTH - GLA_GATE_RANK + LANES).astype(BF16)
            w_up = jnp.pad(w_gla_gate_up[j], ((0, LANES - GLA_GATE_RANK), (0, 0))).astype(BF16)
            h, h_packed = _hgrn_gla_layer(h, B, T, w_in, lower_bounds[l][None, :], w_up, b_gla_gate[j][None, :],
                                          hgrn_norm_g[j][None, :], gla_norm_g[j][None, :],
                                          w_out_ab[j].astype(BF16), ln_g, ln_b)
        else:
            w_in = _pad_cols(w_in_c[j], C_WIDTH - 2 * MLSTM_HEADS + LANES).astype(BF16)
            b_if = jnp.pad(b_if_c[j].astype(F32), (0, LANES - 2 * MLSTM_HEADS))[None, :]
            h, h_packed = _mlstm_layer(h, B, T, w_in, conv_c[j], b_if, w_out_c[j].astype(BF16), ln_g, ln_b)
        h = _moe_ln(h, h_packed, w_router[l], router_bias[l], w_exp_in, w_exp_down, l, w_sh_in[l], w_sh_down[l],
                    ln_ffn_g[l], ln_ffn_b[l])
    return h.reshape(B, T, D)
```

```python
import functools

import jax
import jax.numpy as jnp
from jax import lax
from jax.experimental import pallas as pl
from jax.experimental.pallas import tpu as pltpu
from jax.experimental.pallas import tpu_sc as plsc

F32 = jnp.float32
BF16 = jnp.bfloat16
I32 = jnp.int32

D_MODEL = 1024
DEPTH = 2
HGRN_HEADS = 4
HGRN_HEAD_DIM = 128
HGRN_WIDTH = HGRN_HEADS * HGRN_HEAD_DIM
GLA_HEADS = 4
GLA_KEY_DIM = 64
GLA_VAL_DIM = 128
GLA_KEY_WIDTH = GLA_HEADS * GLA_KEY_DIM
GLA_VAL_WIDTH = GLA_HEADS * GLA_VAL_DIM
GLA_GATE_RANK = 16
GLA_GATE_NORMALIZER = 16.0
LIN_CHUNK = 64
LIN_SUB = 16
AB_WIDTH = 4 * HGRN_WIDTH + 2 * GLA_KEY_WIDTH + 2 * GLA_VAL_WIDTH + GLA_GATE_RANK
MLSTM_HEADS = 4
MLSTM_QK_DIM = 128
MLSTM_V_DIM = 256
MLSTM_QK_WIDTH = MLSTM_HEADS * MLSTM_QK_DIM
MLSTM_V_WIDTH = MLSTM_HEADS * MLSTM_V_DIM
MLSTM_CONV = 4
MLSTM_CHUNK = 128
C_WIDTH = 2 * MLSTM_QK_WIDTH + 2 * MLSTM_V_WIDTH + 2 * MLSTM_HEADS
N_EXPERTS = 256
TOP_K = 8
N_GROUPS = 8
TOPK_GROUPS = 4
GROUP_SIZE = N_EXPERTS // N_GROUPS
EXPERT_FF = 256
SHARED_FF = 256
ROUTED_SCALE = 2.5
ALPHA = (2 * DEPTH) ** 0.25
LN_EPS = 1e-5
RMS_EPS = 1e-6

LANES = 128
VMEM_LIMIT = 56 * 1024 * 1024

MIX_ROWS = 256
MLSTM_ROWS = 512
ROUTE_TOK = 512
COMBINE_TOK = 512
GMM_ROWS = 256
SC_CORES, SC_SUBCORES = 2, 16
SC_WORKERS = SC_CORES * SC_SUBCORES
SC_ROWS = 64


def _cparams(*sem):
    return pltpu.CompilerParams(dimension_semantics=sem, vmem_limit_bytes=VMEM_LIMIT)


def _sigmoid(x):
    return 1.0 / (1.0 + jnp.exp(-x))


def _silu(x):
    return x * _sigmoid(x)


def _log_sigmoid(x):
    return jnp.minimum(x, 0.0) - jnp.log(1.0 + jnp.exp(-jnp.abs(x)))


def _split3(x):
    hi = x.astype(BF16)
    r = x - hi.astype(F32)
    mid = r.astype(BF16)
    lo = (r - mid.astype(F32)).astype(BF16)
    return hi, mid, lo


def _tri_dot_left(tri, x):
    return sum(jnp.dot(tri, p, preferred_element_type=F32) for p in _split3(x))


def _tri_dot_right(x, tri):
    return sum(jnp.dot(p, tri, preferred_element_type=F32) for p in _split3(x))


def _dot_nt(a, b):
    return lax.dot_general(a, b, (((1,), (1,)), ((), ())), preferred_element_type=F32)


def _dot_tn(a, b):
    return lax.dot_general(a, b, (((0,), (0,)), ((), ())), preferred_element_type=F32)


HI16 = -65536


def _pack_bf16_pair(x):
    c = x.shape[1] // 2
    bits = lambda v: lax.bitcast_convert_type(v.astype(BF16).astype(F32), I32)
    return (bits(x[:, c:]) & HI16) | lax.shift_right_logical(bits(x[:, :c]), 16)


def _unpack_bf16_pair(w):
    lo = lax.bitcast_convert_type(lax.shift_left(w, 16), F32)
    hi = lax.bitcast_convert_type(w & HI16, F32)
    return lo.astype(BF16), hi.astype(BF16)


def _layernorm(x, g, b):
    mu = jnp.mean(x, axis=-1, keepdims=True)
    xc = x - mu
    var = jnp.mean(xc * xc, axis=-1, keepdims=True)
    return xc * lax.rsqrt(var + LN_EPS) * g + b


N_UNITS = HGRN_HEADS + GLA_HEADS // 2
HALF_LANES = LANES // 2
LOG2E = 1.4426950408889634


def _pair_selector():
    assert LIN_CHUNK == HALF_LANES == GLA_KEY_DIM
    r = jnp.arange(LIN_SUB * LANES)[:, None]
    c = jnp.arange(LANES)[None, :]
    same_slot = (r // LANES) == (c % LIN_SUB)
    same_half = ((r % LANES) < HALF_LANES) == (c < HALF_LANES)
    return (same_slot & same_half).astype(BF16)


def _pairwise_tiles(q, k, b, lhs_ref, row0):
    c = LIN_SUB
    b2 = b * LOG2E
    for blk in range(LIN_CHUNK // c):
        lo = blk * c
        qb, kb, bb = q[lo:lo + c], k[lo:lo + c], b2[lo:lo + c]
        for j in range(c):
            e = jnp.exp2(jnp.minimum(bb - bb[j:j + 1], 0.0))
            lhs_ref[row0 + lo:row0 + lo + c, j * LANES:(j + 1) * LANES] = (qb * kb[j:j + 1] * e).astype(BF16)


def _unit_chunk(q, k, b, vs, r, st_ref, masks):
    L, c = LIN_CHUNK, LIN_SUB
    lane_lo, col_mod, same_blk_causal = masks
    heads = [lane_lo, ~lane_lo] if len(vs) == 2 else [None]

    def pick(x, m):
        return x if m is None else jnp.where(m, x, jnp.zeros_like(x))

    g = b[L - 1:L, :]
    st = st_ref[...]
    st_b = st.astype(BF16)
    qx = (q * jnp.exp(b)).astype(BF16)
    outs = [_dot_nt(pick(qx, m), st_b) for m in heads]
    k_end = (k * jnp.exp(g - b)).astype(BF16)
    upd = [_dot_tn(v.astype(BF16), k_end) for v in vs]
    st_ref[...] = st * jnp.exp(g) + (upd[0] if len(vs) == 1 else jnp.where(lane_lo, upd[0], upd[1]))

    off_rows = [jnp.zeros((c, LANES), F32)]
    for blk in range(1, L // c):
        lo = blk * c
        ref = b[lo - 1:lo]
        q_in = (q[lo:lo + c] * jnp.exp(b[lo:lo + c] - ref)).astype(BF16)
        k_in = (k * jnp.exp(jnp.minimum(ref - b, 0.0))).astype(BF16)
        if len(vs) == 2:
            k_cat = jnp.concatenate([pick(k_in, heads[0]), pick(k_in, heads[1])], axis=0)
        else:
            k_cat = jnp.concatenate([k_in, jnp.zeros_like(k_in)], axis=0)
        off_rows.append(jnp.where(col_mod < lo, _dot_nt(q_in, k_cat), 0.0))
    a = jnp.where(same_blk_causal, r, jnp.concatenate(off_rows, axis=0)).astype(BF16)
    v_cat = jnp.concatenate([vs[0], vs[-1]], axis=0).astype(BF16)
    return [o + jnp.dot(pick(a, m), v_cat, preferred_element_type=F32) for o, m in zip(outs, heads)]


def _rms_gate(o, g, gate):
    o = o * lax.rsqrt(jnp.mean(o * o, axis=-1, keepdims=True) + RMS_EPS)
    return o * g * _silu(gate)


class _Cols:
    def __init__(self, ref, off, width):
        self.ref, self.off, self.width = ref, off, width

    def __getitem__(self, idx):
        rows, cols = (slice(None), slice(None)) if idx is Ellipsis else idx
        lo = self.off + (cols.start or 0)
        hi = self.off + (self.width if cols.stop is None else cols.stop)
        return self.ref[rows, lo:hi]


def _project_residual_ln(x_ref, y_ref, wout_ref, g_ref, b_ref, h_ref, hp_ref):
    mix = jnp.dot(y_ref[...], wout_ref[...], preferred_element_type=F32)
    h = _layernorm(ALPHA * x_ref[...] + mix, g_ref[...], b_ref[...])
    h_ref[...] = h
    hp_ref[...] = _pack_bf16_pair(h)


def _hgrn_gla_kernel(x_ref, win_ref, lb_ref, wup_ref, bgk_ref, hng_ref, gng_ref, sel_ref, wout_ref, lng_ref, lnb_ref,
                     h_ref, hp_ref, proj_ref, y_ref, lhs_ref, *st_refs):
    @pl.when(pl.program_id(1) == 0)
    def _():
        for r in st_refs:
            r[...] = jnp.zeros_like(r)

    proj_ref[...] = jnp.dot(x_ref[...].astype(BF16), win_ref[...], preferred_element_type=F32)
    W, KW, VW = HGRN_WIDTH, GLA_KEY_WIDTH, GLA_VAL_WIDTH
    hq_ref, hf_ref, hi_ref, hg_ref = (_Cols(proj_ref, i * W, W) for i in range(4))
    gq_ref, gk_ref = _Cols(proj_ref, 4 * W, KW), _Cols(proj_ref, 4 * W + KW, KW)
    gv_ref, gg_ref = _Cols(proj_ref, 4 * W + 2 * KW, VW), _Cols(proj_ref, 4 * W + 2 * KW + VW, VW)
    glr_ref = _Cols(proj_ref, AB_WIDTH - GLA_GATE_RANK, LANES)

    L = LIN_CHUNK
    n_chunks = MIX_ROWS // L
    r_i = lax.broadcasted_iota(I32, (L, LANES), 0)
    c_i = lax.broadcasted_iota(I32, (L, LANES), 1)
    col_mod = jnp.where(c_i < HALF_LANES, c_i, c_i - HALF_LANES)
    same_blk_causal = (col_mod // LIN_SUB == r_i // LIN_SUB) & (col_mod <= r_i)
    c_sub = lax.broadcasted_iota(I32, (LIN_SUB, LANES), 1)
    col_mod_sub = jnp.where(c_sub < HALF_LANES, c_sub, c_sub - HALF_LANES)
    lane_lo = lax.broadcasted_iota(I32, (1, LANES), 1) < HALF_LANES
    masks = (lane_lo, col_mod_sub, same_blk_causal)
    tril = (lax.broadcasted_iota(I32, (L, L), 0) >= lax.broadcasted_iota(I32, (L, L), 1)).astype(BF16)
    lb = lb_ref[...]

    units = []
    for ck in range(n_chunks):
        rs = slice(ck * L, (ck + 1) * L)
        u = jnp.dot(glr_ref[rs, :].astype(BF16), wup_ref[...], preferred_element_type=F32) + bgk_ref[...]
        la_g = _log_sigmoid(u) * (1.0 / GLA_GATE_NORMALIZER)
        qs, ks, las = [], [], []
        for h in range(HGRN_HEADS):
            cs = slice(h * LANES, (h + 1) * LANES)
            z = hf_ref[rs, cs]
            lbh = lb[:, cs]
            qs.append(_silu(hq_ref[rs, cs]))
            ks.append((1.0 - lbh) * _sigmoid(-z))
            las.append(jnp.log(lbh + (1.0 - lbh) * _sigmoid(z)))
        for p in range(GLA_HEADS // 2):
            cs = slice(p * LANES, (p + 1) * LANES)
            qs.append(gq_ref[rs, cs] * (GLA_KEY_DIM ** -0.5))
            ks.append(gk_ref[rs, cs])
            las.append(la_g[:, cs])
        b_all = _tri_dot_left(tril, jnp.concatenate(las, axis=1))
        for un in range(N_UNITS):
            b = b_all[:, un * LANES:(un + 1) * LANES]
            _pairwise_tiles(qs[un], ks[un], b, lhs_ref, (ck * N_UNITS + un) * L)
            units.append((qs[un], ks[un], b))

    r_all = jnp.dot(lhs_ref[...], sel_ref[...], preferred_element_type=F32)

    for ck in range(n_chunks):
        rs = slice(ck * L, (ck + 1) * L)
        for un in range(N_UNITS):
            uc = ck * N_UNITS + un
            q, k, b = units[uc]
            r = r_all[uc * L:(uc + 1) * L]
            if un < HGRN_HEADS:
                cs = slice(un * LANES, (un + 1) * LANES)
                (o,) = _unit_chunk(q, k, b, [hi_ref[rs, cs]], r, st_refs[un], masks)
                y_ref[rs, cs] = _rms_gate(o, hng_ref[:, cs], hg_ref[rs, cs]).astype(y_ref.dtype)
            else:
                p = un - HGRN_HEADS
                vcs = [slice((2 * p + i) * GLA_VAL_DIM, (2 * p + i + 1) * GLA_VAL_DIM) for i in range(2)]
                outs = _unit_chunk(q, k, b, [gv_ref[rs, vc] for vc in vcs], r, st_refs[un], masks)
                for o, vc in zip(outs, vcs):
                    ys = slice(HGRN_WIDTH + vc.start, HGRN_WIDTH + vc.stop)
                    y_ref[rs, ys] = _rms_gate(o, gng_ref[:, vc], gg_ref[rs, vc]).astype(y_ref.dtype)

    _project_residual_ln(x_ref, y_ref, wout_ref, lng_ref, lnb_ref, h_ref, hp_ref)


def _mixer_layer_call(kernel_fn, name, x2, batch, seq, rows, consts, scratch_shapes):
    n, d = x2.shape
    steps = seq // rows
    row_spec = lambda width: pl.BlockSpec((rows, width), lambda b, i: (b * steps + i, 0))
    const_spec = lambda a: pl.BlockSpec(a.shape, lambda b, i: (0,) * a.ndim)
    return pl.pallas_call(
        kernel_fn,
        grid=(batch, steps),
        in_specs=[row_spec(d)] + [const_spec(a) for a in consts],
        out_specs=[row_spec(d), row_spec(d // 2)],
        out_shape=[jax.ShapeDtypeStruct((n, d), F32), jax.ShapeDtypeStruct((n, d // 2), I32)],
        scratch_shapes=scratch_shapes,
        compiler_params=_cparams("parallel", "arbitrary"),
        name=name,
    )(x2, *consts)


def _hgrn_gla_layer(x2, batch, seq, w_in, lb, w_up, b_gk, hgrn_g, gla_g, w_out, ln_g, ln_b):
    R = MIX_ROWS
    n_uc = (R // LIN_CHUNK) * N_UNITS
    scratch = ([pltpu.VMEM((R, w_in.shape[1]), F32), pltpu.VMEM((R, w_out.shape[0]), BF16),
                pltpu.VMEM((n_uc * LIN_CHUNK, LIN_SUB * LANES), BF16)]
               + [pltpu.VMEM((GLA_VAL_DIM, LANES), F32)] * N_UNITS)
    consts = (w_in, lb, w_up, b_gk, hgrn_g, gla_g, _pair_selector(), w_out, ln_g, ln_b)
    return _mixer_layer_call(_hgrn_gla_kernel, "hgrn_gla", x2, batch, seq, R, consts, scratch)


MLSTM_CONV_PAD = 8


def _mlstm_kernel(x_ref, win_ref, cw_ref, bif_ref, wout_ref, lng_ref, lnb_ref, h_ref, hp_ref,
                  proj_ref, y_ref, ext_ref, *state_refs):
    L, R, PAD = MLSTM_CHUNK, MLSTM_ROWS, MLSTM_CONV_PAD
    cn_refs, m_refs = state_refs[:MLSTM_HEADS], state_refs[MLSTM_HEADS:]

    @pl.when(pl.program_id(1) == 0)
    def _():
        ext_ref[0:PAD, :] = jnp.zeros((PAD, ext_ref.shape[1]), F32)
        for r in state_refs:
            r[...] = jnp.zeros_like(r)

    proj_ref[...] = jnp.dot(x_ref[...].astype(BF16), win_ref[...], preferred_element_type=F32)
    QK2, VW = 2 * MLSTM_QK_WIDTH, MLSTM_V_WIDTH
    qk_ref, v_ref = _Cols(proj_ref, 0, QK2), _Cols(proj_ref, QK2, VW)
    og_ref, gt_ref = _Cols(proj_ref, QK2 + VW, VW), _Cols(proj_ref, QK2 + 2 * VW, LANES)

    ext_ref[PAD:PAD + R, :] = qk_ref[...]
    conv = jnp.zeros((R, ext_ref.shape[1]), F32)
    for w in range(MLSTM_CONV):
        conv = conv + cw_ref[w:w + 1, :] * ext_ref[pl.ds(PAD - (MLSTM_CONV - 1) + w, R), :]
    ext_ref[0:PAD, :] = ext_ref[R:R + PAD, :]
    qk_all = _silu(conv)

    r_i = lax.broadcasted_iota(I32, (L, L), 0)
    c_i = lax.broadcasted_iota(I32, (L, L), 1)
    causal = c_i <= r_i
    tril = causal.astype(BF16)
    triu = (r_i <= c_i).astype(BF16)
    ones_col = (lax.broadcasted_iota(I32, (L, LANES), 1) == 0).astype(F32)

    H = MLSTM_HEADS
    for ck in range(R // L):
        rs = slice(ck * L, (ck + 1) * L)
        qk = qk_all[rs]
        gt = gt_ref[rs, :] + bif_ref[...]
        gt_t = gt.T
        b_cols = _tri_dot_left(tril, _log_sigmoid(gt))
        b_rows = _tri_dot_right(_log_sigmoid(gt_t), triu)
        for h in range(H):
            q = qk[:, h * MLSTM_QK_DIM:(h + 1) * MLSTM_QK_DIM].astype(BF16)
            kf = qk[:, MLSTM_QK_WIDTH + h * MLSTM_QK_DIM:MLSTM_QK_WIDTH + (h + 1) * MLSTM_QK_DIM] * (MLSTM_QK_DIM ** -0.5)
            vs = slice(h * MLSTM_V_DIM, (h + 1) * MLSTM_V_DIM)
            v_ext = jnp.concatenate([v_ref[rs, vs], ones_col], axis=1).astype(BF16)
            bc, br = b_cols[:, H + h:H + h + 1], b_rows[H + h:H + h + 1, :]
            ic, ir = gt[:, h:h + 1], gt_t[h:h + 1, :]
            m_prev = m_refs[h][0:1, 0:1]
            g = bc[L - 1:L, :]
            dmat = jnp.where(causal, bc - br + ir, -jnp.inf)
            inter = bc + m_prev
            m_j = jnp.maximum(inter, jnp.max(dmat, axis=-1, keepdims=True))
            s = _dot_nt(q, kf.astype(BF16)) * jnp.exp(dmat - m_j)
            w_inter = jnp.exp(inter - m_j)
            cn = cn_refs[h][...]
            nd = (jnp.dot(s.astype(BF16), v_ext, preferred_element_type=F32)
                  + w_inter * jnp.dot(q, cn.astype(BF16), preferred_element_type=F32))
            num, den = nd[:, :MLSTM_V_DIM], nd[:, MLSTM_V_DIM:MLSTM_V_DIM + 1]
            hid = num / jnp.maximum(jnp.abs(den), jnp.exp(-m_j))
            u = g - bc + ic
            m_new = jnp.maximum(g + m_prev, jnp.max(u, axis=0, keepdims=True))
            wk = (kf * jnp.exp(u - m_new)).astype(BF16)
            cn_refs[h][...] = jnp.exp(g + m_prev - m_new) * cn + _dot_tn(wk, v_ext)
            m_refs[h][...] = jnp.broadcast_to(m_new, m_refs[h].shape)
            y_ref[rs, vs] = (hid * _sigmoid(og_ref[rs, vs])).astype(y_ref.dtype)

    _project_residual_ln(x_ref, y_ref, wout_ref, lng_ref, lnb_ref, h_ref, hp_ref)


def _mlstm_layer(x2, batch, seq, w_in, conv_w, b_if, w_out, ln_g, ln_b):
    R = MLSTM_ROWS
    assert R % MLSTM_CHUNK == 0
    scratch = ([pltpu.VMEM((R, w_in.shape[1]), F32), pltpu.VMEM((R, w_out.shape[0]), BF16),
                pltpu.VMEM((R + MLSTM_CONV_PAD, 2 * MLSTM_QK_WIDTH), F32)]
               + [pltpu.VMEM((MLSTM_QK_DIM, MLSTM_V_DIM + LANES), F32)] * MLSTM_HEADS
               + [pltpu.VMEM((8, LANES), F32)] * MLSTM_HEADS)
    consts = (w_in, conv_w, b_if, w_out, ln_g, ln_b)
    return _mixer_layer_call(_mlstm_kernel, "mlstm", x2, batch, seq, R, consts, scratch)


def _router_kernel(h_ref, whi_ref, wlo_ref, bias_ref, eidx_ref, wgt_ref, rank_ref, cnt_ref, carry_ref):
    TT, E, G, GS = ROUTE_TOK, N_EXPERTS, N_GROUPS, GROUP_SIZE

    @pl.when(pl.program_id(0) == 0)
    def _():
        carry_ref[...] = jnp.zeros_like(carry_ref)

    h = h_ref[...]
    h_hi = h.astype(BF16)
    h_lo = (h - h_hi.astype(F32)).astype(BF16)
    logits = _dot_nt(whi_ref[...], h_hi) + _dot_nt(whi_ref[...], h_lo) + _dot_nt(wlo_ref[...], h_hi)
    scores = _sigmoid(logits)
    biased = scores + bias_ref[:, 0:1]
    neg = -jnp.inf

    io_g = lax.broadcasted_iota(I32, (GS, TT), 0)
    io8 = lax.broadcasted_iota(I32, (G, TT), 0)
    gs = jnp.zeros((G, TT), F32)
    for g in range(G):
        blk = biased[g * GS:(g + 1) * GS, :]
        m1 = jnp.max(blk, axis=0, keepdims=True)
        i1 = jnp.min(jnp.where(blk == m1, io_g, GS), axis=0, keepdims=True)
        m2 = jnp.max(jnp.where(io_g == i1, neg, blk), axis=0, keepdims=True)
        gs = jnp.where(io8 == g, m1 + m2, gs)
    gsel = jnp.zeros((G, TT), F32)
    for _ in range(TOPK_GROUPS):
        m = jnp.max(gs, axis=0, keepdims=True)
        idx = jnp.min(jnp.where(gs == m, io8, G), axis=0, keepdims=True)
        hit = io8 == idx
        gsel = jnp.where(hit, 1.0, gsel)
        gs = jnp.where(hit, neg, gs)
    sel = jnp.concatenate([jnp.broadcast_to(gsel[g:g + 1, :], (GS, TT)) for g in range(G)], axis=0)
    masked = jnp.where(sel > 0.0, biased, neg)

    io_e = lax.broadcasted_iota(I32, (E, TT), 0)
    eidx = jnp.zeros((TOP_K, TT), I32)
    wsel = jnp.zeros((TOP_K, TT), F32)
    chosen = jnp.zeros((E, TT), F32)
    for k in range(TOP_K):
        m = jnp.max(masked, axis=0, keepdims=True)
        idx = jnp.min(jnp.where(masked == m, io_e, E), axis=0, keepdims=True)
        hit = io_e == idx
        sc = jnp.sum(jnp.where(hit, scores, 0.0), axis=0, keepdims=True)
        eidx = jnp.where(io8 == k, idx, eidx)
        wsel = jnp.where(io8 == k, sc, wsel)
        chosen = jnp.where(hit, 1.0, chosen)
        masked = jnp.where(hit, neg, masked)
    wgt_ref[...] = wsel / jnp.sum(wsel, axis=0, keepdims=True) * ROUTED_SCALE
    eidx_ref[...] = eidx

    triu = (lax.broadcasted_iota(I32, (TT, TT), 0) <= lax.broadcasted_iota(I32, (TT, TT), 1)).astype(BF16)
    cum = jnp.dot(chosen.astype(BF16), triu, preferred_element_type=F32)
    carry = carry_ref[:, 0:1]
    before = cum - chosen + carry
    rank = jnp.zeros((TOP_K, TT), F32)
    for k in range(TOP_K):
        r = jnp.sum(jnp.where(io_e == eidx[k:k + 1, :], before, 0.0), axis=0, keepdims=True)
        rank = jnp.where(io8 == k, r, rank)
    rank_ref[...] = rank.astype(I32)
    total = carry + cum[:, TT - 1:TT]
    carry_ref[...] = jnp.broadcast_to(total, carry_ref.shape)
    cnt_ref[...] = jnp.broadcast_to(total, cnt_ref.shape)


def _router(h2, wt_hi, wt_lo, bias_col):
    n, d = h2.shape
    TT, E = ROUTE_TOK, N_EXPERTS
    tok_spec = pl.BlockSpec((TOP_K, TT), lambda i: (0, i))
    return pl.pallas_call(
        _router_kernel,
        grid=(n // TT,),
        in_specs=[pl.BlockSpec((TT, d), lambda i: (i, 0)),
                  pl.BlockSpec((E, d), lambda i: (0, 0)),
                  pl.BlockSpec((E, d), lambda i: (0, 0)),
                  pl.BlockSpec((E, LANES), lambda i: (0, 0))],
        out_specs=[tok_spec, tok_spec, tok_spec, pl.BlockSpec((E, LANES), lambda i: (0, 0))],
        out_shape=[jax.ShapeDtypeStruct((TOP_K, n), I32), jax.ShapeDtypeStruct((TOP_K, n), F32),
                   jax.ShapeDtypeStruct((TOP_K, n), I32), jax.ShapeDtypeStruct((E, LANES), F32)],
        scratch_shapes=[pltpu.VMEM((E, LANES), F32)],
        compiler_params=_cparams("arbitrary"),
        name="router",
    )(h2, wt_hi, wt_lo, bias_col)


def _dest_kernel(eidx_ref, rank_ref, offs_ref, dest_ref):
    TT, E = eidx_ref.shape[1], N_EXPERTS
    io_e = lax.broadcasted_iota(I32, (E, TT), 0)
    io8 = lax.broadcasted_iota(I32, (TOP_K, TT), 0)
    offs = offs_ref[:, 0:1]
    base = jnp.zeros((TOP_K, TT), F32)
    for k in range(TOP_K):
        r = jnp.sum(jnp.where(io_e == eidx_ref[k:k + 1, :], offs, 0.0), axis=0, keepdims=True)
        base = jnp.where(io8 == k, r, base)
    dest_ref[...] = base.astype(I32) + rank_ref[...]


def _dest_rows(eidx, rank, offs_col):
    n = eidx.shape[1]
    TT = 512
    spec = pl.BlockSpec((TOP_K, TT), lambda i: (0, i))
    return pl.pallas_call(
        _dest_kernel,
        grid=(n // TT,),
        in_specs=[spec, spec, pl.BlockSpec((N_EXPERTS, LANES), lambda i: (0, 0))],
        out_specs=spec,
        out_shape=jax.ShapeDtypeStruct((TOP_K, n), I32),
        compiler_params=_cparams("parallel"),
        name="dest_rows",
    )(eidx, rank, offs_col)


def _sc_mesh():
    return plsc.VectorSubcoreMesh(core_axis_name="c", subcore_axis_name="s")


def _sc_worker_id():
    return lax.axis_index("s") * SC_CORES + lax.axis_index("c")


def _dispatch(dest_chunks, x2):
    n, d = x2.shape
    n_chunks = n // SC_ROWS // SC_WORKERS

    @functools.partial(
        pl.kernel, mesh=_sc_mesh(),
        out_type=jax.ShapeDtypeStruct((n * TOP_K, d), x2.dtype),
        scratch_types=[pltpu.VMEM((2, TOP_K, SC_ROWS), I32), pltpu.VMEM((2, SC_ROWS, d), x2.dtype),
                       pltpu.SemaphoreType.DMA((2,))],
    )
    def scatter_rows(x_hbm, idx_hbm, out_hbm, idx_v, rows_v, wsem):
        base = _sc_worker_id() * n_chunks

        def row_scatter(b, k):
            return pltpu.make_async_copy(rows_v.at[b], out_hbm.at[idx_v.at[b, k]], wsem.at[b])

        def drain(b):
            for k in range(TOP_K):
                row_scatter(b, k).wait()

        @pl.loop(0, n_chunks, step=2)
        def _(i):
            for b in range(2):
                c = base + i + b

                @pl.when(i > 0)
                def _():
                    drain(b)

                pltpu.sync_copy(idx_hbm.at[c], idx_v.at[b])
                pltpu.sync_copy(x_hbm.at[pl.ds(c * SC_ROWS, SC_ROWS)], rows_v.at[b])
                for k in range(TOP_K):
                    row_scatter(b, k).start()

        for b in range(2):
            drain(b)

    return scatter_rows(x2, dest_chunks)


def _gather_rows(src, idx):
    d = src.shape[1]
    p = idx.shape[0]
    per_w = p // SC_WORKERS
    n_chunks = per_w // SC_ROWS

    @functools.partial(
        pl.kernel, mesh=_sc_mesh(),
        out_type=jax.ShapeDtypeStruct((p, d), src.dtype),
        scratch_types=[pltpu.VMEM((2, SC_ROWS), I32), pltpu.VMEM((2, SC_ROWS, d), src.dtype),
                       pltpu.SemaphoreType.DMA((2,)), pltpu.SemaphoreType.DMA((2,))],
    )
    def gather_rows(src_hbm, idx_hbm, out_hbm, idx_v, rows_v, gsem, wsem):
        base = _sc_worker_id() * per_w

        def row_gather(b):
            return pltpu.make_async_copy(src_hbm.at[idx_v.at[b]], rows_v.at[b], gsem.at[b])

        def write_back(b, off):
            return pltpu.make_async_copy(rows_v.at[b], out_hbm.at[pl.ds(off, SC_ROWS)], wsem.at[b])

        @pl.loop(0, n_chunks, step=2)
        def _(i):
            for b in range(2):
                off = base + (i + b) * SC_ROWS

                @pl.when(i > 0)
                def _():
                    write_back(b, off).wait()

                pltpu.sync_copy(idx_hbm.at[pl.ds(off, SC_ROWS)], idx_v.at[b])
                row_gather(b).start()
            for b in range(2):
                row_gather(b).wait()
                write_back(b, base + (i + b) * SC_ROWS).start()

        for b in range(2):
            write_back(b, base).wait()

    return gather_rows(src, idx)


GMM_X_SLOTS = 4
GMM_O_SLOTS = 3
GMM_WEIGHT_DMA_PRIORITY = 1


def _gmm_kernel(tile_ref, grp_ref, lo_ref, hi_ref, first_ref, last_ref, newgrp_ref, nextgrp_ref, nreal_ref,
                xs_hbm, win_hbm, wdn_hbm, ys_hbm,
                xbuf, obuf, winbuf, wdnbuf, win_bf, wdn_bf, xsem, osem, wsem, *, layer):
    TM = GMM_ROWS
    n_tiles = xs_hbm.shape[0] // TM
    half = xs_hbm.shape[1]

    def x_copy(t, slot):
        return pltpu.make_async_copy(xs_hbm.at[pl.ds(t * TM, TM)], xbuf.at[slot], xsem.at[slot])

    def o_copy(t, slot):
        return pltpu.make_async_copy(obuf.at[slot], ys_hbm.at[pl.ds(t * TM, TM)], osem.at[slot])

    def w_copies(e, slot):
        return (pltpu.make_async_copy(win_hbm.at[layer, e], winbuf.at[slot], wsem.at[0, slot]),
                pltpu.make_async_copy(wdn_hbm.at[layer, e], wdnbuf.at[slot], wsem.at[1, slot]))

    for t in range(GMM_X_SLOTS - 1):
        x_copy(t, t).start()
    for c in w_copies(grp_ref[0], 0):
        c.start(priority=GMM_WEIGHT_DMA_PRIORITY)

    def visit(v, wslot):
        t = tile_ref[v]
        xslot = t % GMM_X_SLOTS
        oslot = t % GMM_O_SLOTS
        is_first = first_ref[v] == 1
        is_new = newgrp_ref[v] == 1
        wslot = jnp.where(is_new & (v > 0), 1 - wslot, wslot)

        @pl.when(is_first)
        def _():
            x_copy(t, xslot).wait()
            ahead = t + GMM_X_SLOTS - 1

            @pl.when(ahead < n_tiles)
            def _():
                x_copy(ahead, ahead % GMM_X_SLOTS).start()

            @pl.when(t >= GMM_O_SLOTS)
            def _():
                o_copy(t - GMM_O_SLOTS, oslot).wait()

        @pl.when(is_new)
        def _():
            for c in w_copies(grp_ref[v], wslot):
                c.wait()
            win_bf[...] = winbuf[wslot].astype(BF16)
            wdn_bf[...] = wdnbuf[wslot].astype(BF16)
            nxt = nextgrp_ref[v]

            @pl.when(nxt >= 0)
            def _():
                for c in w_copies(nxt, 1 - wslot):
                    c.start(priority=GMM_WEIGHT_DMA_PRIORITY)

        x_lo, x_hi = _unpack_bf16_pair(xbuf[xslot])
        gu = (jnp.dot(x_lo, win_bf[:half, :], preferred_element_type=F32)
              + jnp.dot(x_hi, win_bf[half:, :], preferred_element_type=F32))
        act = (_silu(gu[:, :EXPERT_FF]) * gu[:, EXPERT_FF:]).astype(BF16)
        y = _pack_bf16_pair(jnp.dot(act, wdn_bf[...], preferred_element_type=F32))
        rows = lax.broadcasted_iota(I32, (TM, 1), 0)
        mine = (rows >= lo_ref[v]) & (rows < hi_ref[v])

        @pl.when(is_first)
        def _():
            obuf[oslot] = jnp.where(mine, y, 0)

        @pl.when(jnp.logical_not(is_first))
        def _():
            obuf[oslot] = jnp.where(mine, y, obuf[oslot])

        @pl.when(last_ref[v] == 1)
        def _():
            o_copy(t, oslot).start()

        return wslot

    lax.fori_loop(0, nreal_ref[0], visit, jnp.int32(0))
    for t in range(n_tiles - GMM_O_SLOTS, n_tiles):
        o_copy(t, t % GMM_O_SLOTS).wait()


def _gmm(meta, xs, w_in_all, w_dn_all, layer):
    p, half = xs.shape
    d = 2 * half
    TM = GMM_ROWS
    assert p % TM == 0 and p // TM >= GMM_X_SLOTS
    any_spec = pl.BlockSpec(memory_space=pl.ANY)
    grid_spec = pltpu.PrefetchScalarGridSpec(
        num_scalar_prefetch=len(meta),
        grid=(1,),
        in_specs=[any_spec, any_spec, any_spec],
        out_specs=any_spec,
        scratch_shapes=[pltpu.VMEM((GMM_X_SLOTS, TM, half), I32), pltpu.VMEM((GMM_O_SLOTS, TM, half), I32),
                        pltpu.VMEM((2, d, 2 * EXPERT_FF), F32), pltpu.VMEM((2, EXPERT_FF, d), F32),
                        pltpu.VMEM((d, 2 * EXPERT_FF), BF16), pltpu.VMEM((EXPERT_FF, d), BF16),
                        pltpu.SemaphoreType.DMA((GMM_X_SLOTS,)), pltpu.SemaphoreType.DMA((GMM_O_SLOTS,)),
                        pltpu.SemaphoreType.DMA((2, 2))],
    )
    return pl.pallas_call(
        functools.partial(_gmm_kernel, layer=layer),
        grid_spec=grid_spec,
        out_shape=jax.ShapeDtypeStruct((p, half), I32),
        compiler_params=_cparams("arbitrary"),
        name="gmm",
    )(*meta, xs, w_in_all, w_dn_all)


def _gmm_schedule(counts, n_rows):
    TM = GMM_ROWS
    E = counts.shape[0]
    max_visits = n_rows // TM + E - 1
    ends = jnp.cumsum(counts)
    starts = ends - counts
    first_tile = starts // TM
    n_vis = jnp.where(counts > 0, jnp.maximum(ends - 1, 0) // TM - first_tile + 1, 0)
    vis_end = jnp.cumsum(n_vis)
    vis_start = vis_end - n_vis
    n_real = vis_end[-1]
    idx = jnp.arange(max_visits, dtype=I32)
    v = jnp.minimum(idx, jnp.maximum(n_real - 1, 0))
    real = idx < n_real
    grp = jnp.minimum(jnp.sum((vis_end[None, :] <= v[:, None]).astype(I32), axis=1), E - 1)
    onehot = grp[:, None] == jnp.arange(E, dtype=I32)[None, :]

    def take(table):
        return jnp.sum(jnp.where(onehot, table[None, :], 0), axis=1)

    tile = take(first_tile) + (v - take(vis_start))
    lo = jnp.where(real, jnp.maximum(take(starts), tile * TM) - tile * TM, 0)
    hi = jnp.where(real, jnp.minimum(take(ends), (tile + 1) * TM) - tile * TM, 0)
    prev = lambda a: jnp.concatenate([jnp.full((1,), -1, I32), a[:-1]])
    first = real & (tile != prev(tile))
    newgrp = real & (grp != prev(grp))
    last = real & (jnp.concatenate([first[1:], jnp.ones((1,), bool)]) | (idx == n_real - 1))
    cand = jnp.where(counts > 0, jnp.arange(E, dtype=I32), E)
    later = jnp.concatenate([jnp.flip(lax.cummin(jnp.flip(cand)))[1:], jnp.full((1,), E, I32)])
    nextgrp = take(jnp.where(later < E, later, -1))
    meta = tuple(a.astype(I32) for a in (tile, grp, lo, hi, first, last, newgrp, nextgrp))
    return meta + (n_real.astype(I32).reshape(1),)


def _combine_kernel(wgt_ref, h_ref, yg_ref, wsi_ref, wsd_ref, g_ref, b_ref, o_ref):
    h = h_ref[...]
    gu = jnp.dot(h.astype(BF16), wsi_ref[...], preferred_element_type=F32)
    act = (_silu(gu[:, :SHARED_FF]) * gu[:, SHARED_FF:]).astype(BF16)
    acc = ALPHA * h + jnp.dot(act, wsd_ref[...], preferred_element_type=F32)
    wgt = wgt_ref[...]
    half = yg_ref.shape[2]
    r_lo = jnp.zeros((h.shape[0], half), F32)
    r_hi = jnp.zeros((h.shape[0], half), F32)
    for k in range(TOP_K):
        y_lo, y_hi = _unpack_bf16_pair(yg_ref[k])
        r_lo = r_lo + wgt[:, k:k + 1] * y_lo.astype(F32)
        r_hi = r_hi + wgt[:, k:k + 1] * y_hi.astype(F32)
    acc = acc + jnp.concatenate([r_lo, r_hi], axis=1)
    o_ref[...] = _layernorm(acc, g_ref[...], b_ref[...])


def _combine(wgt_tk, h2, yg, w_si, w_sd, g, b):
    n, d = h2.shape
    TT = COMBINE_TOK
    const2 = lambda shape: pl.BlockSpec(shape, lambda i: (0, 0))
    return pl.pallas_call(
        _combine_kernel,
        grid=(n // TT,),
        in_specs=[pl.BlockSpec((TT, TOP_K), lambda i: (i, 0)),
                  pl.BlockSpec((TT, d), lambda i: (i, 0)),
                  pl.BlockSpec((TOP_K, TT, d // 2), lambda i: (0, i, 0)),
                  const2((d, 2 * SHARED_FF)), const2((SHARED_FF, d)), const2((1, d)), const2((1, d))],
        out_specs=pl.BlockSpec((TT, d), lambda i: (i, 0)),
        out_shape=jax.ShapeDtypeStruct((n, d), F32),
        compiler_params=_cparams("parallel"),
        name="combine",
    )(wgt_tk, h2, yg, w_si, w_sd, g, b)


def _moe_ln(h2, h2_packed, w_router, r_bias, w_e_in_all, w_e_dn_all, layer, w_s_in, w_s_dn, ln_g, ln_b):
    n, d = h2.shape
    wt = w_router.T
    wt_hi = wt.astype(BF16)
    wt_lo = (wt - wt_hi.astype(F32)).astype(BF16)
    bias_col = jnp.broadcast_to(r_bias.astype(F32)[:, None], (N_EXPERTS, LANES))
    eidx, wgt, rank, cnt = _router(h2, wt_hi, wt_lo, bias_col)
    counts = cnt[:, 0].astype(I32)
    offs = jnp.cumsum(counts) - counts
    offs_col = jnp.broadcast_to(offs.astype(F32)[:, None], (N_EXPERTS, LANES))
    dest = _dest_rows(eidx, rank, offs_col)
    dest_chunks = dest.reshape(TOP_K, n // SC_ROWS, SC_ROWS).transpose(1, 0, 2)
    xs = _dispatch(dest_chunks, h2_packed)
    ys = _gmm(_gmm_schedule(counts, n * TOP_K), xs, w_e_in_all, w_e_dn_all, layer)
    yg = _gather_rows(ys, dest.reshape(-1)).reshape(TOP_K, n, d // 2)
    return _combine(wgt.T, h2, yg, w_s_in.astype(BF16), w_s_dn.astype(BF16), ln_g[None, :], ln_b[None, :])


def _pad_cols(w, width):
    return jnp.pad(w, ((0, 0), (0, width - w.shape[1])))


def kernel(x, w_in_ab, w_gla_gate_up, b_gla_gate, hgrn_norm_g, gla_norm_g, w_out_ab, hgrn_lb_logits, w_in_c, conv_c, b_if_c, w_out_c, w_router, router_bias, w_exp_in, w_exp_down, w_sh_in, w_sh_down, ln_mix_g, ln_mix_b, ln_ffn_g, ln_ffn_b):
    B, T, D = x.shape
    lower_bounds = jnp.cumsum(jax.nn.softmax(hgrn_lb_logits.astype(F32), axis=0), axis=0)
    h = x.reshape(B * T, D)
    for l in range(DEPTH):
        j = l // 2
        ln_g, ln_b = ln_mix_g[l][None, :], ln_mix_b[l][None, :]
        if l % 2 == 0:
            w_in = _pad_cols(w_in_ab[j], AB_WIDTH - GLA_GATE_RANK + LANES).astype(BF16)
            w_up = jnp.pad(w_gla_gate_up[j], ((0, LANES - GLA_GATE_RANK), (0, 0))).astype(BF16)
            h, h_packed = _hgrn_gla_layer(h, B, T, w_in, lower_bounds[l][None, :], w_up, b_gla_gate[j][None, :],
                                          hgrn_norm_g[j][None, :], gla_norm_g[j][None, :],
                                          w_out_ab[j].astype(BF16), ln_g, ln_b)
        else:
            w_in = _pad_cols(w_in_c[j], C_WIDTH - 2 * MLSTM_HEADS + LANES).astype(BF16)
            b_if = jnp.pad(b_if_c[j].astype(F32), (0, LANES - 2 * MLSTM_HEADS))[None, :]
            h, h_packed = _mlstm_layer(h, B, T, w_in, conv_c[j], b_if, w_out_c[j].astype(BF16), ln_g, ln_b)
        h = _moe_ln(h, h_packed, w_router[l], router_bias[l], w_exp_in, w_exp_down, l, w_sh_in[l], w_sh_down[l],
                    ln_ffn_g[l], ln_ffn_b[l])
    return h.reshape(B, T, D)
```
